```python
import math
import jax, jax.numpy as jnp
from jax import lax
import numpy as np

D_MODEL = 2048
BATCH = 8
SEQ = 2048
DEPTH = 2
DEC_BATCH = 128
DEC_SEQ = 8
PAST_LEN = 2048
PAGE_SIZE = 128

N_A_LAYERS = DEPTH // 2
N_B_LAYERS = DEPTH - N_A_LAYERS
POOL_WINDOWS = (2, 4, 8, 16)
N_POOL_GROUPS = len(POOL_WINDOWS)
POOL_GROUP_DIM = D_MODEL // N_POOL_GROUPS
POOL_BUF = max(POOL_WINDOWS) - 1
HEAD_DIM = 128
N_HEADS = D_MODEL // HEAD_DIM
N_KV_HEADS = 4
GQA_GROUP = N_HEADS // N_KV_HEADS
N_BRANCH = 3
CMP_BLOCK = 32
CMP_STRIDE = 16
CMP_HIDDEN = HEAD_DIM
SLC_BLOCK = 64
N_SELECT = 16
WINDOW = 512
WIN_QBLOCK = 128
SEL_QBLOCK = 16
N_BUCKETS = 32
MAX_DISTANCE = 128
D_FF = -(-8 * D_MODEL // (3 * 256)) * 256
RMS_EPS = 1e-6
SEL_BIG = 1e3
NEG_INF = -1e30

kernel_name = 'yoco_pool_nsa_decoder_step'


def _rmsnorm(x, g):
    xf = x.astype(jnp.float32)
    y = xf * lax.rsqrt(jnp.mean(xf * xf, axis=-1, keepdims=True) + RMS_EPS)
    return (y * g.astype(jnp.float32)).astype(x.dtype)


def _swiglu(x, w_in, w_out):
    g, u = jnp.split(x @ w_in, 2, axis=-1)
    return (jax.nn.silu(g) * u) @ w_out


def _ffn_sublayer(h, norms_l, w_in, w_out):
    return h + _rmsnorm(_swiglu(_rmsnorm(h, norms_l[2]), w_in, w_out), norms_l[3])


def _pool_mix(ext, pos0):
    S = ext.shape[1] - POOL_BUF
    xf = ext.astype(jnp.float32)
    c = jnp.pad(jnp.cumsum(xf, axis=1), ((0, 0), (1, 0), (0, 0)))
    pos = pos0 + jnp.arange(S)
    outs = []
    for g, w in enumerate(POOL_WINDOWS):
        sl = slice(g * POOL_GROUP_DIM, (g + 1) * POOL_GROUP_DIM)
        s = c[:, POOL_BUF + 1:POOL_BUF + 1 + S, sl] - c[:, POOL_BUF + 1 - w:POOL_BUF + 1 - w + S, sl]
        cnt = jnp.minimum(w, pos + 1).astype(jnp.float32)
        outs.append(s / cnt[None, :, None] - xf[:, POOL_BUF:, sl])
    return jnp.concatenate(outs, axis=-1).astype(ext.dtype)


def _pool_layer(h, buf, w_pool_l, scale_l, norms_l, w_in, w_out, pos0):
    B, S, _ = h.shape
    hn = _rmsnorm(h, norms_l[0])
    ext = jnp.concatenate([buf, hn], axis=1)
    mix = _pool_mix(ext, pos0).reshape(B, S, N_POOL_GROUPS, POOL_GROUP_DIM)
    mix = jnp.einsum('bsgc,gcd->bsgd', mix, w_pool_l).reshape(B, S, D_MODEL) * scale_l
    h = h + _rmsnorm(mix, norms_l[1])
    return _ffn_sublayer(h, norms_l, w_in, w_out), ext[:, -POOL_BUF:]


def _rel_bucket(dist):
    n = jnp.maximum(dist, 0)
    max_exact = N_BUCKETS // 2
    nf = jnp.maximum(n, 1).astype(jnp.float32)
    large = max_exact + (jnp.log(nf / max_exact) / math.log(MAX_DISTANCE / max_exact)
                         * (N_BUCKETS - max_exact)).astype(jnp.int32)
    large = jnp.minimum(large, N_BUCKETS - 1)
    return jnp.where(n < max_exact, n, large)


def _attn_probs(q, k, qpos, kpos, rel_bias, mask):
    Tq, Tk = qpos.shape[0], kpos.shape[0]
    logits = jnp.einsum('bqhgd,bkhd->bhgqk', q, k).astype(jnp.float32) * (HEAD_DIM ** -0.5)
    bias = rel_bias.astype(jnp.float32)[_rel_bucket(qpos[:, None] - kpos[None, :])]
    bias = bias.reshape(Tq, Tk, N_KV_HEADS, GQA_GROUP).transpose(2, 3, 0, 1)
    p = jax.nn.softmax(jnp.where(mask, logits + bias, NEG_INF), axis=-1)
    return jnp.where(mask, p, 0.0)


def _window_mask(qpos, kpos):
    d = qpos[:, None] - kpos[None, :]
    return (d >= 0) & (d < WINDOW) & (kpos[None, :] >= 0)


def _shared_kv(h, kv_norm, w_kv):
    B, S, _ = h.shape
    return (_rmsnorm(h, kv_norm) @ w_kv).reshape(B, S, N_BRANCH, 2, N_KV_HEADS, HEAD_DIM)


def _compress(raw, cmp_pos, cmp_w1, cmp_b1, cmp_w2):
    B, T = raw.shape[:2]
    n_sub = T // CMP_STRIDE
    sub = raw[:, :n_sub * CMP_STRIDE].reshape(B, n_sub, CMP_STRIDE, 2, N_KV_HEADS, HEAD_DIM)
    pe = cmp_pos.transpose(1, 0, 2)[:, :, None, :]
    first = jnp.einsum('bnjche,cjef->bnchf', sub + pe[:CMP_STRIDE], cmp_w1[:, :CMP_STRIDE])
    second = jnp.einsum('bnjche,cjef->bnchf', sub + pe[CMP_STRIDE:], cmp_w1[:, CMP_STRIDE:])
    hid = jax.nn.gelu(first[:, :-1] + second[:, 1:] + cmp_b1[:, None, :])
    return jnp.einsum('bnchf,cfe->bnche', hid, cmp_w2)


def _overlap(n_cmp, n_slc):
    cs = jnp.arange(n_cmp) * CMP_STRIDE
    ce = cs + CMP_BLOCK - 1
    ss = jnp.arange(n_slc) * SLC_BLOCK
    se = ss + SLC_BLOCK - 1
    ov = jnp.minimum(ce[:, None], se[None, :]) - jnp.maximum(cs[:, None], ss[None, :]) + 1
    return jnp.maximum(ov, 0).astype(jnp.float32) / CMP_BLOCK


def _select_blocks(p_c, qpos, n_slc):
    imp = jnp.einsum('bhgqk,kj->bqhj', p_c, _overlap(p_c.shape[-1], n_slc))
    j = jnp.arange(n_slc)[None, :]
    qb = (qpos // SLC_BLOCK)[:, None]
    prio = (2 * (j == qb) + (j == qb - 1) + (j == 0)).astype(jnp.float32)
    score = jnp.where((j <= qb)[None, :, None, :], imp + SEL_BIG * prio[None, :, None, :], -SEL_BIG)
    _, sel = lax.top_k(score, min(N_SELECT, n_slc))
    return sel


def _to_blocks(rows):
    B, T = rows.shape[:2]
    n_slc = -(-T // SLC_BLOCK)
    rows = jnp.pad(rows, ((0, 0), (0, n_slc * SLC_BLOCK - T), (0, 0), (0, 0), (0, 0)))
    return rows.reshape(B, n_slc, SLC_BLOCK, 2, N_KV_HEADS, HEAD_DIM).transpose(0, 3, 4, 1, 2, 5)


def _sel_attend(q, kvb, sel, qpos, rel_bias):
    B = q.shape[0]
    bi = jnp.arange(B)[:, None, None, None]
    hi = jnp.arange(N_KV_HEADS)[None, None, :, None]
    k = kvb[:, 0][bi, hi, sel]
    v = kvb[:, 1][bi, hi, sel]
    logits = jnp.einsum('bchgd,bchkld->bchgkl', q, k).astype(jnp.float32) * (HEAD_DIM ** -0.5)
    kpos = sel[..., None] * SLC_BLOCK + jnp.arange(SLC_BLOCK)
    qp = qpos[None, :, None, None, None]
    table = rel_bias.astype(jnp.float32).reshape(N_BUCKETS, N_KV_HEADS, GQA_GROUP)
    bias = jnp.moveaxis(table[_rel_bucket(qp - kpos), hi[..., None]], -1, 3)
    mask = (kpos <= qp)[:, :, :, None]
    shp = logits.shape
    logits = jnp.where(mask, logits + bias, NEG_INF).reshape(*shp[:4], -1)
    p = jax.nn.softmax(logits, axis=-1).reshape(shp)
    return jnp.einsum('bchgkl,bchkld->bchgd', p.astype(v.dtype), v)


def _sel_prompt(q, sel, kvb, rel_bias):
    B, T = q.shape[:2]
    C = SEL_QBLOCK if T % SEL_QBLOCK == 0 else T
    n = T // C
    qc = jnp.swapaxes(q.reshape(B, n, C, *q.shape[2:]), 0, 1)
    sc = jnp.swapaxes(sel.reshape(B, n, C, *sel.shape[2:]), 0, 1)
    pc = jnp.arange(T).reshape(n, C)
    o = lax.map(lambda a: _sel_attend(a[0], kvb, a[1], a[2], rel_bias), (qc, sc, pc))
    return jnp.swapaxes(o, 0, 1).reshape(q.shape)


def _sel_sample(q, sel, new_rows, cache_slc_kv, page_table, qpos, rel_bias):
    def one(a):
        qb, sb, nb, pt = a
        past = cache_slc_kv[pt].reshape(-1, 2, N_KV_HEADS, HEAD_DIM)
        rows = jnp.concatenate([past, nb], axis=0)[None]
        return _sel_attend(qb[None], _to_blocks(rows), sb[None], qpos, rel_bias)[0]
    return lax.map(one, (q, sel, new_rows, page_table))


def _window_prompt(q, kvw, rel_bias):
    B, T = q.shape[:2]
    nb = T // WIN_QBLOCK
    kvp = jnp.pad(kvw, ((0, 0), (WINDOW, 0), (0, 0), (0, 0), (0, 0)))
    qb = jnp.swapaxes(q.reshape(B, nb, WIN_QBLOCK, *q.shape[2:]), 0, 1)

    def block(a):
        c, qc = a
        start = c * WIN_QBLOCK
        kv = lax.dynamic_slice_in_dim(kvp, start, WINDOW + WIN_QBLOCK, axis=1)
        qpos = start + jnp.arange(WIN_QBLOCK)
        kpos = start - WINDOW + jnp.arange(WINDOW + WIN_QBLOCK)
        p = _attn_probs(qc, kv[:, :, 0], qpos, kpos, rel_bias, _window_mask(qpos, kpos))
        return jnp.einsum('bhgqk,bkhd->bqhgd', p.astype(kv.dtype), kv[:, :, 1])

    o = lax.map(block, (jnp.arange(nb), qb))
    return jnp.swapaxes(o, 0, 1).reshape(q.shape)


def _nsa_layer(h, ckv, qpos, n_slc, sel_fn, win_fn, w_q_l, w_o_l, norms_l, rel_bias, w_in, w_out):
    B, S, _ = h.shape
    hn = _rmsnorm(h, norms_l[0])
    proj = hn @ w_q_l
    q = proj[..., :N_HEADS * HEAD_DIM].reshape(B, S, N_KV_HEADS, GQA_GROUP, HEAD_DIM)
    gates = jax.nn.sigmoid(proj[..., N_HEADS * HEAD_DIM:].astype(jnp.float32))
    gates = gates.reshape(B, S, N_BRANCH, N_KV_HEADS, GQA_GROUP, 1)
    kpos_c = jnp.arange(ckv.shape[1]) * CMP_STRIDE + CMP_BLOCK - 1
    p_c = _attn_probs(q, ckv[:, :, 0], qpos, kpos_c, rel_bias, kpos_c[None, :] <= qpos[:, None])
    o_c = jnp.einsum('bhgqk,bkhd->bqhgd', p_c.astype(ckv.dtype), ckv[:, :, 1])
    sel = _select_blocks(p_c, qpos, n_slc)
    o_s = sel_fn(q, sel)
    o_w = win_fn(q)
    o = gates[:, :, 0] * o_c + gates[:, :, 1] * o_s + gates[:, :, 2] * o_w
    out = o.astype(h.dtype).reshape(B, S, N_HEADS * HEAD_DIM) @ w_o_l
    h = h + _rmsnorm(out, norms_l[1])
    return _ffn_sublayer(h, norms_l, w_in, w_out)


def setup_inputs(seed: int = 0) -> dict:
    key = jax.random.key(seed)
    ks = jax.random.split(key, 24)
    f32 = jnp.float32
    n_pages = PAST_LEN // PAGE_SIZE
    n_phys = -(-5 * DEC_BATCH * n_pages // 4)
    win_buf = min(WINDOW, PAST_LEN)
    qcols = N_HEADS * HEAD_DIM + N_BRANCH * N_HEADS
    kvcols = N_BRANCH * 2 * N_KV_HEADS * HEAD_DIM

    def nrm(k, shape, scale=1.0):
        return scale * jax.random.normal(k, shape, f32)

    perm = jax.random.permutation(ks[6], n_phys)
    page_table = perm[:DEC_BATCH * n_pages].reshape(DEC_BATCH, n_pages).astype(jnp.int32)
    return {
        'x_prompt': nrm(ks[0], (BATCH, SEQ, D_MODEL)),
        'x_sample': nrm(ks[1], (DEC_BATCH, DEC_SEQ, D_MODEL)),
        'state_pool': nrm(ks[2], (N_A_LAYERS, DEC_BATCH, POOL_BUF, D_MODEL)),
        'cache_cmp_kv': nrm(ks[3], (n_phys, PAGE_SIZE, 2, N_KV_HEADS, HEAD_DIM)),
        'cache_slc_kv': nrm(ks[4], (n_phys, PAGE_SIZE, 2, N_KV_HEADS, HEAD_DIM)),
        'state_win_kv': nrm(ks[5], (DEC_BATCH, win_buf, 2, N_KV_HEADS, HEAD_DIM)),
        'page_table': page_table,
        'norms': 1.0 + nrm(ks[7], (DEPTH, 4, D_MODEL), 0.05),
        'w_pool': nrm(ks[8], (N_A_LAYERS, N_POOL_GROUPS, POOL_GROUP_DIM, POOL_GROUP_DIM), POOL_GROUP_DIM ** -0.5),
        'pool_scale': 1.0 + nrm(ks[9], (N_A_LAYERS, D_MODEL), 0.1),
        'kv_norm': 1.0 + nrm(ks[10], (D_MODEL,), 0.05),
        'w_kv': nrm(ks[11], (D_MODEL, kvcols), D_MODEL ** -0.5),
        'cmp_pos': nrm(ks[12], (2, CMP_BLOCK, HEAD_DIM), 0.1),
        'cmp_w1': nrm(ks[13], (2, CMP_BLOCK, HEAD_DIM, CMP_HIDDEN), (CMP_BLOCK * HEAD_DIM) ** -0.5),
        'cmp_b1': nrm(ks[14], (2, CMP_HIDDEN), 0.01),
        'cmp_w2': nrm(ks[15], (2, CMP_HIDDEN, HEAD_DIM), CMP_HIDDEN ** -0.5),
        'w_q': nrm(ks[16], (N_B_LAYERS, D_MODEL, qcols), D_MODEL ** -0.5),
        'w_o': nrm(ks[17], (N_B_LAYERS, N_HEADS * HEAD_DIM, D_MODEL), (N_HEADS * HEAD_DIM) ** -0.5),
        'rel_bias': nrm(ks[18], (N_BUCKETS, N_HEADS), 0.2),
        'w_ffn_in': nrm(ks[19], (DEPTH, D_MODEL, 2 * D_FF), D_MODEL ** -0.5),
        'w_ffn_out': nrm(ks[20], (DEPTH, D_FF, D_MODEL), D_FF ** -0.5),
    }


def reference(x_prompt, x_sample, state_pool, cache_cmp_kv, cache_slc_kv, state_win_kv, page_table,
              norms, w_pool, pool_scale, kv_norm, w_kv, cmp_pos, cmp_w1, cmp_b1, cmp_w2,
              w_q, w_o, rel_bias, w_ffn_in, w_ffn_out):
    B, S, _ = x_prompt.shape
    DB, DS, _ = x_sample.shape
    past_len = page_table.shape[1] * cache_slc_kv.shape[1]
    win_buf = state_win_kv.shape[1]

    h = x_prompt
    qpos_p = jnp.arange(S)
    pool_p = []
    for i in range(DEPTH):
        if i < N_A_LAYERS:
            zeros = jnp.zeros((B, POOL_BUF, D_MODEL), h.dtype)
            h, st = _pool_layer(h, zeros, w_pool[i], pool_scale[i], norms[i], w_ffn_in[i], w_ffn_out[i], 0)
            pool_p.append(st)
        else:
            if i == N_A_LAYERS:
                kv_p = _shared_kv(h, kv_norm, w_kv)
                ckv_p = _compress(kv_p[:, :, 0], cmp_pos, cmp_w1, cmp_b1, cmp_w2)
                kvb_p = _to_blocks(kv_p[:, :, 1])
                kvw_p = kv_p[:, :, 2]
            j = i - N_A_LAYERS
            h = _nsa_layer(h, ckv_p, qpos_p, kvb_p.shape[3],
                           lambda q, sel: _sel_prompt(q, sel, kvb_p, rel_bias),
                           lambda q: _window_prompt(q, kvw_p, rel_bias),
                           w_q[j], w_o[j], norms[i], rel_bias, w_ffn_in[i], w_ffn_out[i])
    y_prompt = h
    pad_front = max(win_buf - S, 0)
    win_state_p = jnp.pad(kvw_p, ((0, 0), (pad_front, 0), (0, 0), (0, 0), (0, 0)))[:, -win_buf:]

    h = x_sample
    qpos_s = past_len + jnp.arange(DS)
    pool_s = []
    for i in range(DEPTH):
        if i < N_A_LAYERS:
            h, st = _pool_layer(h, state_pool[i], w_pool[i], pool_scale[i], norms[i],
                                w_ffn_in[i], w_ffn_out[i], past_len)
            pool_s.append(st)
        else:
            if i == N_A_LAYERS:
                kv_s = _shared_kv(h, kv_norm, w_kv)

                def _cmp_one(a):
                    pt, new = a
                    past = cache_cmp_kv[pt].reshape(-1, 2, N_KV_HEADS, HEAD_DIM)
                    rows = jnp.concatenate([past, new], axis=0)[None]
                    return _compress(rows, cmp_pos, cmp_w1, cmp_b1, cmp_w2)[0]

                ckv_s = lax.map(_cmp_one, (page_table, kv_s[:, :, 0]))
                n_slc_s = -(-(past_len + DS) // SLC_BLOCK)
                win_rows = jnp.concatenate([state_win_kv, kv_s[:, :, 2]], axis=1)
                kpos_w = past_len - win_buf + jnp.arange(win_buf + DS)
                wmask = _window_mask(qpos_s, kpos_w)
            j = i - N_A_LAYERS
            h = _nsa_layer(h, ckv_s, qpos_s, n_slc_s,
                           lambda q, sel: _sel_sample(q, sel, kv_s[:, :, 1], cache_slc_kv, page_table, qpos_s, rel_bias),
                           lambda q: jnp.einsum('bhgqk,bkhd->bqhgd',
                                                _attn_probs(q, win_rows[:, :, 0], qpos_s, kpos_w, rel_bias, wmask).astype(win_rows.dtype),
                                                win_rows[:, :, 1]),
                           w_q[j], w_o[j], norms[i], rel_bias, w_ffn_in[i], w_ffn_out[i])
    y_sample = h
    win_state_s = win_rows[:, -win_buf:]

    return (y_prompt, y_sample, jnp.stack(pool_p), jnp.stack(pool_s),
            kv_p[:, :, 0], kv_s[:, :, 0], kv_p[:, :, 1], kv_s[:, :, 1],
            win_state_p, win_state_s)
```

```python
import functools
import math

import numpy as np
import jax
import jax.numpy as jnp
from jax import lax
from jax.experimental import pallas as pl
from jax.experimental.pallas import tpu as pltpu

F32 = jnp.float32
BF16 = jnp.bfloat16

HEAD_DIM = 128
N_KV_HEADS = 4
GQA_GROUP = 4
N_HEADS = N_KV_HEADS * GQA_GROUP
N_BRANCH = 3
POOL_WINDOWS = (2, 4, 8, 16)
POOL_BUF = max(POOL_WINDOWS) - 1
POOL_HALO = POOL_BUF + 1
CMP_BLOCK = 32
CMP_STRIDE = 16
SLC_BLOCK = 64
N_SELECT = 16
WINDOW = 512
N_BUCKETS = 32
MAX_DISTANCE = 128
RMS_EPS = 1e-6
SEL_BIG = 1e3
NEG_INF = -1e30
PAGE_ROWS = 128
KV_COLS = 2 * N_KV_HEADS * HEAD_DIM

VMEM_LIMIT_BYTES = 56 * 1024 * 1024

_NT = (((1,), (1,)), ((), ()))


def _params(*sem):
    return pltpu.CompilerParams(dimension_semantics=sem, vmem_limit_bytes=VMEM_LIMIT_BYTES)


def _rms(x):
    return x * lax.rsqrt(jnp.mean(x * x, axis=-1, keepdims=True) + RMS_EPS)


def _dot(a, b):
    return jnp.dot(a, b, preferred_element_type=F32)


def _dot_nt(a, b):
    return lax.dot_general(a, b, _NT, preferred_element_type=F32)


def _pool_mix_kernel(h_ref, buf_ref, g_ref, mix_ref, st_ref, ext_ref, *, ts, pos0, group_dim):
    i = pl.program_id(1)

    @pl.when(i == 0)
    def _():
        ext_ref[0:POOL_HALO, :] = buf_ref[0]

    @pl.when(i > 0)
    def _():
        ext_ref[0:POOL_HALO, :] = ext_ref[ts:ts + POOL_HALO, :]

    x = h_ref[0]
    ext_ref[POOL_HALO:POOL_HALO + ts, :] = _rms(x) * g_ref[...]
    pos = pos0 + i * ts + lax.broadcasted_iota(jnp.int32, (ts, 1), 0)
    for gi, w in enumerate(POOL_WINDOWS):
        cols = slice(gi * group_dim, (gi + 1) * group_dim)
        cur = ext_ref[POOL_HALO:POOL_HALO + ts, cols]
        s = cur
        for u in range(1, w):
            s = s + ext_ref[POOL_HALO - u:POOL_HALO - u + ts, cols]
        cnt = jnp.minimum(w, pos + 1).astype(F32)
        mix_ref[0, :, cols] = (s / cnt - cur).astype(mix_ref.dtype)
    st_ref[0] = ext_ref[ts:ts + POOL_HALO, :]


def _pool_mix(h, buf, g, pos0):
    B, S, D = h.shape
    ts = min(S, 256)
    assert S % ts == 0 and (ts >= POOL_HALO or S == ts) and ts % 8 == 0
    buf16 = jnp.pad(buf, ((0, 0), (POOL_HALO - POOL_BUF, 0), (0, 0)))
    mix, st = pl.pallas_call(
        functools.partial(_pool_mix_kernel, ts=ts, pos0=pos0, group_dim=D // len(POOL_WINDOWS)),
        grid=(B, S // ts),
        in_specs=[pl.BlockSpec((1, ts, D), lambda b, i: (b, i, 0)),
                  pl.BlockSpec((1, POOL_HALO, D), lambda b, i: (b, 0, 0)),
                  pl.BlockSpec((1, D), lambda b, i: (0, 0))],
        out_specs=[pl.BlockSpec((1, ts, D), lambda b, i: (b, i, 0)),
                   pl.BlockSpec((1, POOL_HALO, D), lambda b, i: (b, 0, 0))],
        out_shape=[jax.ShapeDtypeStruct((B, S, D), BF16),
                   jax.ShapeDtypeStruct((B, POOL_HALO, D), F32)],
        scratch_shapes=[pltpu.VMEM((POOL_HALO + ts, D), F32)],
        compiler_params=_params("parallel", "arbitrary"),
        name="pool_mix",
    )(h, buf16, g.reshape(1, D))
    return mix, st[:, POOL_HALO - POOL_BUF:]


def _proj_res_kernel(*refs, groups, has_scale):
    if has_scale:
        a_ref, w_ref, scale_ref, g_ref, res_ref, o_ref = refs
    else:
        a_ref, w_ref, g_ref, res_ref, o_ref = refs
    gd = a_ref.shape[1] // groups
    parts = [_dot(a_ref[:, gi * gd:(gi + 1) * gd], w_ref[gi]) for gi in range(groups)]
    y = parts[0] if groups == 1 else jnp.concatenate(parts, axis=-1)
    if has_scale:
        y = y * scale_ref[...]
    o_ref[...] = res_ref[...] + _rms(y) * g_ref[...]


def _proj_res(a, w, scale, g, res, name):
    M, D = a.shape
    groups = w.shape[0]
    tm = min(M, 256)
    assert M % tm == 0
    row = lambda i: (i, 0)
    fixed = lambda i: (0, 0)
    in_specs = [pl.BlockSpec((tm, D), row), pl.BlockSpec(w.shape, lambda i: (0, 0, 0))]
    args = [a, w]
    if scale is not None:
        in_specs.append(pl.BlockSpec((1, D), fixed))
        args.append(scale.reshape(1, D))
    in_specs += [pl.BlockSpec((1, D), fixed), pl.BlockSpec((tm, D), row)]
    args += [g.reshape(1, D), res]
    return pl.pallas_call(
        functools.partial(_proj_res_kernel, groups=groups, has_scale=scale is not None),
        grid=(M // tm,),
        in_specs=in_specs,
        out_specs=pl.BlockSpec((tm, D), row),
        out_shape=jax.ShapeDtypeStruct((M, D), F32),
        compiler_params=_params("parallel"),
        name=name,
    )(*args)


def _ffn_kernel(h_ref, gi_ref, go_ref, wg_ref, wu_ref, wo_ref, out_ref, xn_ref, acc_ref):
    f = pl.program_id(1)

    @pl.when(f == 0)
    def _():
        xn_ref[...] = (_rms(h_ref[...]) * gi_ref[...]).astype(xn_ref.dtype)
        acc_ref[...] = jnp.zeros_like(acc_ref)

    xn = xn_ref[...]
    gate = _dot(xn, wg_ref[...])
    up = _dot(xn, wu_ref[...])
    act = (gate * jax.nn.sigmoid(gate) * up).astype(BF16)
    acc_ref[...] += _dot(act, wo_ref[...])

    @pl.when(f == pl.num_programs(1) - 1)
    def _():
        out_ref[...] = h_ref[...] + _rms(acc_ref[...]) * go_ref[...]


def _ffn(h, g_in, g_out, w_in, w_out):
    M, D = h.shape
    F = w_out.shape[0]
    tm = min(M, 512)
    tf = 512
    assert M % tm == 0 and F % tf == 0
    nf = F // tf
    return pl.pallas_call(
        _ffn_kernel,
        grid=(M // tm, nf),
        in_specs=[pl.BlockSpec((tm, D), lambda i, f: (i, 0)),
                  pl.BlockSpec((1, D), lambda i, f: (0, 0)),
                  pl.BlockSpec((1, D), lambda i, f: (0, 0)),
                  pl.BlockSpec((D, tf), lambda i, f: (0, f)),
                  pl.BlockSpec((D, tf), lambda i, f: (0, nf + f)),
                  pl.BlockSpec((tf, D), lambda i, f: (f, 0))],
        out_specs=pl.BlockSpec((tm, D), lambda i, f: (i, 0)),
        out_shape=jax.ShapeDtypeStruct((M, D), F32),
        scratch_shapes=[pltpu.VMEM((tm, D), BF16), pltpu.VMEM((tm, D), F32)],
        compiler_params=_params("parallel", "arbitrary"),
        name="ffn",
    )(h, g_in.reshape(1, D), g_out.reshape(1, D), w_in, w_in, w_out)


def _norm_proj_kernel(x_ref, g_ref, *refs, acts):
    n = len(acts)
    xn = (_rms(x_ref[...]) * g_ref[...]).astype(BF16)
    for w_ref, o_ref, act in zip(refs[:n], refs[n:], acts):
        y = _dot(xn, w_ref[...])
        if act == "sigmoid":
            y = jax.nn.sigmoid(y)
        o_ref[...] = y.astype(o_ref.dtype)


def _norm_proj(x, g, weights, out_dtypes, acts, name):
    M, D = x.shape
    tm = min(M, 256)
    assert M % tm == 0
    return pl.pallas_call(
        functools.partial(_norm_proj_kernel, acts=acts),
        grid=(M // tm,),
        in_specs=[pl.BlockSpec((tm, D), lambda i: (i, 0)), pl.BlockSpec((1, D), lambda i: (0, 0))]
        + [pl.BlockSpec(w.shape, lambda i: (0, 0)) for w in weights],
        out_specs=[pl.BlockSpec((tm, w.shape[1]), lambda i: (i, 0)) for w in weights],
        out_shape=[jax.ShapeDtypeStruct((M, w.shape[1]), dt) for w, dt in zip(weights, out_dtypes)],
        compiler_params=_params("parallel"),
        name=name,
    )(x, g.reshape(1, D), *weights)


def _gelu_tanh(x):
    return 0.5 * x * (1.0 + jnp.tanh(math.sqrt(2.0 / math.pi) * (x + 0.044715 * (x * x * x))))


def _compress_kernel(pt_ref, *refs, n_pages):
    del pt_ref
    page_refs = refs[:n_pages]
    w1_ref, pos_ref, b1_ref, w2_ref, o_ref = refs[n_pages:]
    sub_per_page = PAGE_ROWS // CMP_STRIDE
    n_sub = n_pages * sub_per_page
    rows = N_KV_HEADS * n_sub
    row_in_head = lax.broadcasted_iota(jnp.int32, (n_sub, 1), 0)
    for c in range(2):
        first = jnp.zeros((rows, HEAD_DIM), F32)
        second = jnp.zeros((rows, HEAD_DIM), F32)
        for j in range(CMP_STRIDE):
            pieces = []
            for h in range(N_KV_HEADS):
                col = j * KV_COLS + (c * N_KV_HEADS + h) * HEAD_DIM
                pieces += [r[0, :, col:col + HEAD_DIM] for r in page_refs]
            x = jnp.concatenate(pieces, axis=0)
            xa = (x + pos_ref[c, j:j + 1, :]).astype(BF16)
            xb = (x + pos_ref[c, CMP_STRIDE + j:CMP_STRIDE + j + 1, :]).astype(BF16)
            first = first + _dot(xa, w1_ref[c, j])
            second = second + _dot(xb, w1_ref[c, CMP_STRIDE + j])
        nxt = pltpu.roll(second, rows - 1, 0)
        hid = _gelu_tanh(first + nxt + b1_ref[c])
        out = _dot(hid.astype(BF16), w2_ref[c])
        for h in range(N_KV_HEADS):
            col = (c * N_KV_HEADS + h) * HEAD_DIM
            blk = out[h * n_sub:(h + 1) * n_sub]
            o_ref[0, :, col:col + HEAD_DIM] = jnp.where(row_in_head < n_sub - 1, blk, 0.0)


def _compress(pages, table, w1, pos, b1, w2):
    nb, n_pages = table.shape
    sub_per_page = PAGE_ROWS // CMP_STRIDE
    n_sub = n_pages * sub_per_page
    wide = CMP_STRIDE * KV_COLS
    page_specs = [pl.BlockSpec((1, sub_per_page, wide), functools.partial(lambda b, pt, p: (pt[b, p], 0, 0), p=p))
                  for p in range(n_pages)]
    const = lambda shape: pl.BlockSpec(shape, lambda b, pt: (0,) * len(shape))
    grid_spec = pltpu.PrefetchScalarGridSpec(
        num_scalar_prefetch=1,
        grid=(nb,),
        in_specs=page_specs + [const(w1.shape), const(pos.shape), const(b1.shape), const(w2.shape)],
        out_specs=pl.BlockSpec((1, n_sub, KV_COLS), lambda b, pt: (b, 0, 0)),
    )
    return pl.pallas_call(
        functools.partial(_compress_kernel, n_pages=n_pages),
        grid_spec=grid_spec,
        out_shape=jax.ShapeDtypeStruct((nb, n_sub, KV_COLS), F32),
        compiler_params=_params("parallel"),
        name="compress",
    )(table, *([pages] * n_pages), w1, pos, b1, w2)


def _rel_bucket_np(dist):
    n = np.maximum(dist, 0)
    max_exact = N_BUCKETS // 2
    nf = np.maximum(n, 1).astype(np.float32)
    large = max_exact + (np.log(nf / np.float32(max_exact)) / np.float32(math.log(MAX_DISTANCE / max_exact))
                         * np.float32(N_BUCKETS - max_exact)).astype(np.int32)
    large = np.minimum(large, N_BUCKETS - 1)
    return np.where(n < max_exact, n, large).astype(np.int32)


def _bias_table_kernel(rb_ref, bk_ref, o_ref):
    bk = bk_ref[...]
    for h in range(N_HEADS):
        v = jnp.full(bk.shape, rb_ref[0, h], F32)
        for b in range(1, N_BUCKETS):
            v = jnp.where(bk == b, rb_ref[b, h], v)
        o_ref[h] = v


def _bias_table(rel_bias, bucket):
    R, C = bucket.shape
    tr = min(R, 128)
    assert R % tr == 0
    return pl.pallas_call(
        _bias_table_kernel,
        grid=(R // tr,),
        in_specs=[pl.BlockSpec(memory_space=pltpu.SMEM), pl.BlockSpec((tr, C), lambda r: (r, 0))],
        out_specs=pl.BlockSpec((N_HEADS, tr, C), lambda r: (0, r, 0)),
        out_shape=jax.ShapeDtypeStruct((N_HEADS, R, C), F32),
        compiler_params=_params("parallel"),
        name="bias_table",
    )(rel_bias, jnp.asarray(bucket))


def _overlap_np(n_cmp_pad, n_slc, lanes):
    n_cmp = n_cmp_pad - 1
    cs = np.arange(n_cmp) * CMP_STRIDE
    ce = cs + CMP_BLOCK - 1
    ss = np.arange(n_slc) * SLC_BLOCK
    se = ss + SLC_BLOCK - 1
    ov = np.minimum(ce[:, None], se[None, :]) - np.maximum(cs[:, None], ss[None, :]) + 1
    out = np.zeros((n_cmp_pad, lanes), np.float32)
    out[:n_cmp, :n_slc] = np.maximum(ov, 0).astype(np.float32) / CMP_BLOCK
    return out


def _expand_np(n_slc, lanes, n_keys):
    e = (np.arange(n_keys)[None, :] // SLC_BLOCK == np.arange(lanes)[:, None]) & (np.arange(lanes)[:, None] < n_slc)
    return e.astype(np.float32)


def _cmp_branch(q4, kc, vc, bias_c, qpos, n_sub, ov):
    t = qpos.shape[0]
    n_idx = lax.broadcasted_iota(jnp.int32, (1, n_sub), 1)
    valid = (n_idx * CMP_STRIDE + (CMP_BLOCK - 1) <= qpos) & (n_idx < n_sub - 1)
    valid4 = jnp.concatenate([valid] * GQA_GROUP, axis=0)
    logits = jnp.where(valid4, _dot_nt(q4, kc) * (HEAD_DIM ** -0.5) + bias_c, NEG_INF)
    m = jnp.max(logits, axis=-1, keepdims=True)
    p = jnp.exp(logits - m)
    p = p / jnp.sum(p, axis=-1, keepdims=True)
    p = jnp.where(valid4, p, 0.0)
    o_c = _dot(p.astype(BF16), vc)
    psum = p[0:t]
    for g in range(1, GQA_GROUP):
        psum = psum + p[g * t:(g + 1) * t]
    imp = jnp.dot(psum, ov, precision=lax.Precision.HIGHEST, preferred_element_type=F32)
    return o_c, imp


def _select_mask(imp, qpos, n_slc):
    t, lanes = imp.shape
    j = lax.broadcasted_iota(jnp.int32, (1, lanes), 1)
    qb = qpos >> int(math.log2(SLC_BLOCK))
    prio = (2.0 * (j == qb).astype(F32) + (j == qb - 1).astype(F32) + (j == 0).astype(F32))
    score = jnp.where(j <= qb, imp + SEL_BIG * prio, -SEL_BIG)
    score = jnp.where(j < n_slc, score, -3e38)
    rank = jnp.zeros((t, lanes), jnp.int32)
    for jj in range(n_slc):
        col = score[:, jj:jj + 1]
        ahead = (col > score) | ((col == score) & (jj < j))
        rank = rank + ahead.astype(jnp.int32)
    return ((rank < min(N_SELECT, n_slc)) & (j < n_slc)).astype(F32)


def _gated_out(o_c, o_s, o_w, gates, t, write):
    for g in range(GQA_GROUP):
        rows = slice(g * t, (g + 1) * t)
        o = (gates[:, g:g + 1] * o_c[rows]
             + gates[:, GQA_GROUP + g:GQA_GROUP + g + 1] * o_s[rows]
             + gates[:, 2 * GQA_GROUP + g:2 * GQA_GROUP + g + 1] * o_w[rows])
        write(g, o)


def _attn_prompt_kernel(q_ref, gate_ref, ck_ref, cv_ref, sk_ref, sv_ref, wk_ref, wv_ref,
                        toe_ref, cmpb_ref, ov_ref, e_ref, o_ref,
                        mask_ref, m_ref, l_ref, acc_ref, *, n_slc, n_sub):
    tq = PAGE_ROWS
    rows = GQA_GROUP * tq
    i = pl.program_id(2)
    scale = HEAD_DIM ** -0.5
    qblk = q_ref[0]
    q4 = jnp.concatenate([qblk[:, g * HEAD_DIM:(g + 1) * HEAD_DIM] for g in range(GQA_GROUP)], axis=0)
    t_idx = lax.broadcasted_iota(jnp.int32, (tq, 1), 0)
    qpos = i * tq + t_idx

    kc = ck_ref[0].astype(BF16)
    vc = cv_ref[0].astype(BF16)
    o_c, imp = _cmp_branch(q4, kc, vc, cmpb_ref[...].reshape(rows, n_sub), qpos, n_sub, ov_ref[...])
    sel = _select_mask(imp, qpos, n_slc)
    allowed = _dot(sel.astype(BF16), e_ref[...])
    madd = (allowed - 1.0) * (-NEG_INF)
    for kk in range(mask_ref.shape[0]):
        mask_ref[kk] = madd[:, kk * tq:(kk + 1) * tq]

    k_idx = lax.broadcasted_iota(jnp.int32, (1, tq), 1)

    def attend(k_ref, v_ref, lo, use_sel):
        m_ref[...] = jnp.full(m_ref.shape, NEG_INF, F32)
        l_ref[...] = jnp.zeros(l_ref.shape, F32)
        acc_ref[...] = jnp.zeros(acc_ref.shape, F32)

        def body(k, carry):
            k0 = pl.multiple_of(k * tq, tq)
            kt = k_ref[0, pl.ds(k0, tq), :].astype(BF16)
            vt = v_ref[0, pl.ds(k0, tq), :].astype(BF16)
            dd = jnp.minimum(i - k, 2)
            bias = jnp.concatenate([toe_ref[g * 3 + dd] for g in range(GQA_GROUP)], axis=0)
            s = _dot_nt(q4, kt) * scale + bias
            dist = qpos - (k * tq + k_idx)
            ok = dist >= 0
            if use_sel:
                add = mask_ref[k]
            else:
                ok = ok & (dist < WINDOW)
                add = None
            ok4 = jnp.concatenate([ok] * GQA_GROUP, axis=0)
            if add is not None:
                s = s + jnp.concatenate([add] * GQA_GROUP, axis=0)
            s = jnp.where(ok4, s, NEG_INF)
            m_old = m_ref[...]
            m_new = jnp.maximum(m_old, jnp.max(s, axis=-1, keepdims=True))
            alpha = jnp.exp(m_old - m_new)
            p = jnp.exp(s - m_new)
            l_ref[...] = alpha * l_ref[...] + jnp.sum(p, axis=-1, keepdims=True)
            acc_ref[...] = alpha * acc_ref[...] + _dot(p.astype(BF16), vt)
            m_ref[...] = m_new
            return carry

        lax.fori_loop(lo, i + 1, body, 0)
        return acc_ref[...] / l_ref[...]

    o_s = attend(sk_ref, sv_ref, 0, True)
    o_w = attend(wk_ref, wv_ref, jnp.maximum(i - WINDOW // tq, 0), False)

    def write(g, o):
        o_ref[0, :, g * HEAD_DIM:(g + 1) * HEAD_DIM] = o.astype(o_ref.dtype)

    _gated_out(o_c, o_s, o_w, gate_ref[0, 0], tq, write)


def _attn_prompt(q, gates_r, ckv, slc, win, toe, cmpb, n_slc):
    B, S, _ = q.shape
    tq = PAGE_ROWS
    n_sub = ckv.shape[1]
    nq = S // tq
    assert S % tq == 0 and n_sub <= tq and n_slc <= tq
    lanes = tq
    ov = jnp.asarray(_overlap_np(n_sub, n_slc, lanes))
    e = jnp.asarray(_expand_np(n_slc, lanes, S), dtype=BF16)
    gw = GQA_GROUP * HEAD_DIM
    kcol = lambda b, h, i: (b, 0, h)
    vcol = lambda b, h, i: (b, 0, N_KV_HEADS + h)
    rows = GQA_GROUP * tq
    return pl.pallas_call(
        functools.partial(_attn_prompt_kernel, n_slc=n_slc, n_sub=n_sub),
        grid=(B, N_KV_HEADS, nq),
        in_specs=[pl.BlockSpec((1, tq, gw), lambda b, h, i: (b, i, h)),
                  pl.BlockSpec((1, 1, tq, gates_r.shape[3]), lambda b, h, i: (b, h, i, 0)),
                  pl.BlockSpec((1, n_sub, HEAD_DIM), kcol), pl.BlockSpec((1, n_sub, HEAD_DIM), vcol),
                  pl.BlockSpec((1, S, HEAD_DIM), kcol), pl.BlockSpec((1, S, HEAD_DIM), vcol),
                  pl.BlockSpec((1, S, HEAD_DIM), kcol), pl.BlockSpec((1, S, HEAD_DIM), vcol),
                  pl.BlockSpec((GQA_GROUP * 3, tq, tq), lambda b, h, i: (h, 0, 0)),
                  pl.BlockSpec((GQA_GROUP, tq, n_sub), lambda b, h, i: (h, i, 0)),
                  pl.BlockSpec(ov.shape, lambda b, h, i: (0, 0)),
                  pl.BlockSpec(e.shape, lambda b, h, i: (0, 0))],
        out_specs=pl.BlockSpec((1, tq, gw), lambda b, h, i: (b, i, h)),
        out_shape=jax.ShapeDtypeStruct(q.shape, BF16),
        scratch_shapes=[pltpu.VMEM((nq, tq, tq), F32),
                        pltpu.VMEM((rows, 1), F32), pltpu.VMEM((rows, 1), F32),
                        pltpu.VMEM((rows, HEAD_DIM), F32)],
        compiler_params=_params("parallel", "parallel", "arbitrary"),
        name="attn_prompt",
    )(q, gates_r, ckv, ckv, slc, slc, win, win, toe, cmpb, ov, e)


def _attn_sample_kernel(pt_ref, *refs, n_pages, n_slc, n_sub, past_len, win_buf, ds):
    del pt_ref
    page_refs = refs[:n_pages]
    (q_ref, gate_ref, ckv_ref, snew_ref, wst_ref, wnew_ref, bias_ref, ov_ref, e_ref,
     o_ref, wout_ref) = refs[n_pages:]
    scale = HEAD_DIM ** -0.5
    nk_s = (n_pages + 1) * PAGE_ROWS
    nk_w = win_buf + PAGE_ROWS
    t_idx = lax.broadcasted_iota(jnp.int32, (ds, 1), 0)
    qpos = past_len + t_idx
    pad = jnp.zeros((PAGE_ROWS - ds, HEAD_DIM), F32)
    gates = gate_ref[0]

    wout_ref[0, 0:win_buf - ds, :] = wst_ref[0, ds:win_buf, :]
    wout_ref[0, win_buf - ds:win_buf, :] = wnew_ref[0]

    ks_pos = lax.broadcasted_iota(jnp.int32, (1, nk_s), 1)
    dist_w = win_buf + t_idx - lax.broadcasted_iota(jnp.int32, (1, nk_w), 1)
    ok_w = (dist_w >= 0) & (dist_w < WINDOW)
    ok_w4 = jnp.concatenate([ok_w] * GQA_GROUP, axis=0)

    def softmax_av(logits, ok4, v):
        logits = jnp.where(ok4, logits, NEG_INF)
        m = jnp.max(logits, axis=-1, keepdims=True)
        p = jnp.exp(logits - m)
        p = p / jnp.sum(p, axis=-1, keepdims=True)
        return _dot(p.astype(BF16), v)

    for hk in range(N_KV_HEADS):
        kcols = slice(hk * HEAD_DIM, (hk + 1) * HEAD_DIM)
        vcols = slice((N_KV_HEADS + hk) * HEAD_DIM, (N_KV_HEADS + hk + 1) * HEAD_DIM)
        qcol = hk * GQA_GROUP * HEAD_DIM
        q4 = jnp.concatenate([q_ref[0, :, qcol + g * HEAD_DIM:qcol + (g + 1) * HEAD_DIM]
                              for g in range(GQA_GROUP)], axis=0)
        hb = bias_ref[hk * GQA_GROUP:(hk + 1) * GQA_GROUP].reshape(GQA_GROUP * ds, bias_ref.shape[2])

        kc = ckv_ref[0, :, kcols].astype(BF16)
        vc = ckv_ref[0, :, vcols].astype(BF16)
        o_c, imp = _cmp_branch(q4, kc, vc, hb[:, nk_s + nk_w:nk_s + nk_w + n_sub], qpos, n_sub, ov_ref[...])
        sel = _select_mask(imp, qpos, n_slc)
        allowed = _dot(sel.astype(BF16), e_ref[...]) > 0.5
        ok_s = allowed & (ks_pos <= qpos)
        ok_s4 = jnp.concatenate([ok_s] * GQA_GROUP, axis=0)

        k_s = jnp.concatenate([r[0, :, kcols] for r in page_refs]
                              + [snew_ref[0, :, kcols], pad], axis=0).astype(BF16)
        v_s = jnp.concatenate([r[0, :, vcols] for r in page_refs]
                              + [snew_ref[0, :, vcols], pad], axis=0).astype(BF16)
        o_s = softmax_av(_dot_nt(q4, k_s) * scale + hb[:, 0:nk_s], ok_s4, v_s)

        k_w = jnp.concatenate([wst_ref[0, :, kcols], wnew_ref[0, :, kcols], pad], axis=0).astype(BF16)
        v_w = jnp.concatenate([wst_ref[0, :, vcols], wnew_ref[0, :, vcols], pad], axis=0).astype(BF16)
        o_w = softmax_av(_dot_nt(q4, k_w) * scale + hb[:, nk_s:nk_s + nk_w], ok_w4, v_w)

        def write(g, o, qcol=qcol):
            o_ref[0, :, qcol + g * HEAD_DIM:qcol + (g + 1) * HEAD_DIM] = o.astype(o_ref.dtype)

        gcol = lambda br: gates[:, br * N_HEADS + hk * GQA_GROUP:br * N_HEADS + (hk + 1) * GQA_GROUP]
        _gated_out(o_c, o_s, o_w, jnp.concatenate([gcol(0), gcol(1), gcol(2)], axis=1), ds, write)


def _attn_sample(q, gates, ckv, cache_slc, table, slc_new, win_state, win_new, bias, n_slc, past_len):
    DB, ds, _ = q.shape
    n_pages = table.shape[1]
    n_sub = ckv.shape[1]
    win_buf = win_state.shape[1]
    lanes = PAGE_ROWS
    assert ds % 8 == 0 and ds <= PAGE_ROWS and n_slc <= lanes and n_sub <= lanes and win_buf % 8 == 0
    assert win_buf <= past_len and win_buf > ds
    nk_s = (n_pages + 1) * PAGE_ROWS
    ov = jnp.asarray(_overlap_np(n_sub, n_slc, lanes))
    e = jnp.asarray(_expand_np(n_slc, lanes, nk_s), dtype=BF16)
    page_specs = [pl.BlockSpec((1, PAGE_ROWS, KV_COLS), functools.partial(lambda b, pt, p: (pt[b, p], 0, 0), p=p))
                  for p in range(n_pages)]
    per_b = lambda shape: pl.BlockSpec((1,) + shape[1:], lambda b, pt: (b,) + (0,) * (len(shape) - 1))
    const = lambda shape: pl.BlockSpec(shape, lambda b, pt: (0,) * len(shape))
    grid_spec = pltpu.PrefetchScalarGridSpec(
        num_scalar_prefetch=1,
        grid=(DB,),
        in_specs=page_specs + [per_b(q.shape), per_b(gates.shape), per_b(ckv.shape), per_b(slc_new.shape),
                               per_b(win_state.shape), per_b(win_new.shape),
                               const(bias.shape), const(ov.shape), const(e.shape)],
        out_specs=[per_b(q.shape), per_b(win_state.shape)],
    )
    return pl.pallas_call(
        functools.partial(_attn_sample_kernel, n_pages=n_pages, n_slc=n_slc, n_sub=n_sub,
                          past_len=past_len, win_buf=win_buf, ds=ds),
        grid_spec=grid_spec,
        out_shape=[jax.ShapeDtypeStruct(q.shape, BF16), jax.ShapeDtypeStruct(win_state.shape, F32)],
        compiler_params=_params("parallel"),
        name="attn_sample",
    )(table, *([cache_slc] * n_pages), q, gates, ckv, slc_new, win_state, win_new, bias, ov, e)


def _prompt_buckets(S, n_sub):
    tq = PAGE_ROWS
    t = np.arange(tq)[:, None]
    k = np.arange(tq)[None, :]
    toe = np.concatenate([_rel_bucket_np(t - k + d * tq) for d in range(3)], axis=0)
    qpos = np.arange(S)[:, None]
    kpos_c = np.arange(n_sub)[None, :] * CMP_STRIDE + CMP_BLOCK - 1
    return toe, _rel_bucket_np(qpos - kpos_c)


def _sample_buckets(ds, past_len, n_pages, win_buf, n_sub):
    qpos = past_len + np.arange(ds)[:, None]
    ks = np.arange((n_pages + 1) * PAGE_ROWS)[None, :]
    kw = past_len - win_buf + np.arange(win_buf + PAGE_ROWS)[None, :]
    kc = np.arange(n_sub)[None, :] * CMP_STRIDE + CMP_BLOCK - 1
    return np.concatenate([_rel_bucket_np(qpos - ks), _rel_bucket_np(qpos - kw), _rel_bucket_np(qpos - kc)], axis=1)


def kernel(x_prompt, x_sample, state_pool, cache_cmp_kv, cache_slc_kv, state_win_kv, page_table, norms, w_pool,
           pool_scale, kv_norm, w_kv, cmp_pos, cmp_w1, cmp_b1, cmp_w2, w_q, w_o, rel_bias, w_ffn_in, w_ffn_out):
    B, S, D = x_prompt.shape
    DB, DS, _ = x_sample.shape
    n_pages = page_table.shape[1]
    past_len = n_pages * cache_slc_kv.shape[1]
    win_buf = state_win_kv.shape[1]
    depth = norms.shape[0]
    assert depth == 2 and w_pool.shape[0] == 1 and w_q.shape[0] == 1, "one pooling layer then one attention layer"
    assert cache_slc_kv.shape[1] == PAGE_ROWS and S % PAGE_ROWS == 0
    assert past_len % CMP_STRIDE == 0 and DS < CMP_STRIDE and D == N_HEADS * HEAD_DIM
    assert S >= win_buf

    qdim = N_HEADS * HEAD_DIM
    n_gate = N_BRANCH * N_HEADS
    w_pool_b = w_pool[0].astype(BF16)
    w_in_b = w_ffn_in.astype(BF16)
    w_out_b = w_ffn_out.astype(BF16)
    w_kv_b = [w_kv[:, k * KV_COLS:(k + 1) * KV_COLS].astype(BF16) for k in range(N_BRANCH)]
    w_qq = w_q[0, :, :qdim].astype(BF16)
    w_qg = jnp.pad(w_q[0, :, qdim:], ((0, 0), (0, 128 - n_gate))).astype(BF16)
    w_o_b = w_o.astype(BF16)
    w1_b = cmp_w1.astype(BF16)
    w2_b = cmp_w2.astype(BF16)
    b1 = cmp_b1.reshape(2, 1, HEAD_DIM)

    def trunk_front(x, buf, pos0):
        Bx, Sx, _ = x.shape
        M = Bx * Sx
        mix, st = _pool_mix(x, buf, norms[0, 0], pos0)
        h = _proj_res(mix.reshape(M, D), w_pool_b, pool_scale[0], norms[0, 1], x.reshape(M, D), "pool_proj")
        h1 = _ffn(h, norms[0, 2], norms[0, 3], w_in_b[0], w_out_b[0])
        rows = _norm_proj(h1, kv_norm, w_kv_b, [F32] * N_BRANCH, [None] * N_BRANCH, "kv_proj")
        q, gates = _norm_proj(h1, norms[1, 0], [w_qq, w_qg], [BF16, F32], [None, "sigmoid"], "q_proj")
        return h1, st, rows, q, gates

    def trunk_back(h1, o):
        h2 = _proj_res(o, w_o_b, None, norms[1, 1], h1, "o_proj")
        return _ffn(h2, norms[1, 2], norms[1, 3], w_in_b[1], w_out_b[1])

    sub_per_page = PAGE_ROWS // CMP_STRIDE
    wide = CMP_STRIDE * KV_COLS
    kv5 = lambda r, b, s: r.reshape(b, s, 2, N_KV_HEADS, HEAD_DIM)

    h1, st_p, (cmp_p, slc_p, win_p), q_p, gates_p = trunk_front(x_prompt, jnp.zeros((B, POOL_BUF, D), F32), 0)
    pages_p = S // PAGE_ROWS
    ckv_p = _compress(cmp_p.reshape(B * pages_p, sub_per_page, wide),
                      jnp.arange(B * pages_p, dtype=jnp.int32).reshape(B, pages_p), w1_b, cmp_pos, b1, w2_b)
    n_sub_p = ckv_p.shape[1]
    toe_bk, cmp_bk = _prompt_buckets(S, n_sub_p)
    toe = _bias_table(rel_bias, toe_bk).reshape(N_HEADS * 3, PAGE_ROWS, PAGE_ROWS)
    cmpb = _bias_table(rel_bias, cmp_bk)
    gates_r = gates_p[:, :n_gate].reshape(B, S, N_BRANCH, N_KV_HEADS, GQA_GROUP)
    gates_r = jnp.pad(gates_r.transpose(0, 3, 1, 2, 4).reshape(B, N_KV_HEADS, S, N_BRANCH * GQA_GROUP),
                      ((0, 0), (0, 0), (0, 0), (0, 16 - N_BRANCH * GQA_GROUP)))
    o_p = _attn_prompt(q_p.reshape(B, S, qdim), gates_r, ckv_p, slc_p.reshape(B, S, KV_COLS),
                       win_p.reshape(B, S, KV_COLS), toe, cmpb, -(-S // SLC_BLOCK))
    y_prompt = trunk_back(h1, o_p.reshape(B * S, qdim)).reshape(B, S, D)
    win_state_p = win_p.reshape(B, S, KV_COLS)[:, S - win_buf:]

    h1s, st_s, (cmp_s, slc_s, win_s), q_s, gates_s = trunk_front(x_sample, state_pool[0], past_len)
    n_phys = cache_cmp_kv.shape[0]
    ckv_s = _compress(cache_cmp_kv.reshape(n_phys, sub_per_page, wide), page_table, w1_b, cmp_pos, b1, w2_b)
    n_sub_s = ckv_s.shape[1]
    bias_s = _bias_table(rel_bias, _sample_buckets(DS, past_len, n_pages, win_buf, n_sub_s))
    o_s, win_state_s = _attn_sample(
        q_s.reshape(DB, DS, qdim), gates_s.reshape(DB, DS, 128), ckv_s,
        cache_slc_kv.reshape(n_phys, PAGE_ROWS, KV_COLS), page_table,
        slc_s.reshape(DB, DS, KV_COLS), state_win_kv.reshape(DB, win_buf, KV_COLS), win_s.reshape(DB, DS, KV_COLS),
        bias_s, -(-(past_len + DS) // SLC_BLOCK), past_len)
    y_sample = trunk_back(h1s, o_s.reshape(DB * DS, qdim)).reshape(DB, DS, D)

    return (y_prompt, y_sample, st_p[None], st_s[None],
            kv5(cmp_p, B, S), kv5(cmp_s, DB, DS), kv5(slc_p, B, S), kv5(slc_s, DB, DS),
            kv5(win_state_p, B, win_buf), kv5(win_state_s, DB, win_buf))
```

```python
import functools
import math

import numpy as np
import jax
import jax.numpy as jnp
from jax import lax
from jax.experimental import pallas as pl
from jax.experimental.pallas import tpu as pltpu

F32 = jnp.float32
BF16 = jnp.bfloat16

HEAD_DIM = 128
N_KV_HEADS = 4
GQA_GROUP = 4
N_HEADS = N_KV_HEADS * GQA_GROUP
N_BRANCH = 3
POOL_WINDOWS = (2, 4, 8, 16)
POOL_BUF = max(POOL_WINDOWS) - 1
POOL_HALO = POOL_BUF + 1
CMP_BLOCK = 32
CMP_STRIDE = 16
SLC_BLOCK = 64
N_SELECT = 16
WINDOW = 512
N_BUCKETS = 32
MAX_DISTANCE = 128
RMS_EPS = 1e-6
SEL_BIG = 1e3
NEG_INF = -1e30
PAGE_ROWS = 128
KV_COLS = 2 * N_KV_HEADS * HEAD_DIM
ROW_SPLIT = 2 * N_KV_HEADS
GATE_PAD = 16

VMEM_LIMIT_BYTES = 56 * 1024 * 1024

_NT = (((1,), (1,)), ((), ()))


def _params(*sem):
    return pltpu.CompilerParams(dimension_semantics=sem, vmem_limit_bytes=VMEM_LIMIT_BYTES)


def _rms(x):
    return x * lax.rsqrt(jnp.mean(x * x, axis=-1, keepdims=True) + RMS_EPS)


def _dot(a, b):
    return jnp.dot(a, b, preferred_element_type=F32)


def _dot_nt(a, b):
    return lax.dot_general(a, b, _NT, preferred_element_type=F32)


def _pool_mix_kernel(h_ref, buf_ref, g_ref, mix_ref, st_ref, ext_ref, *, ts, pos0, group_dim):
    i = pl.program_id(1)

    @pl.when(i == 0)
    def _():
        ext_ref[0:POOL_HALO, :] = buf_ref[0]

    @pl.when(i > 0)
    def _():
        ext_ref[0:POOL_HALO, :] = ext_ref[ts:ts + POOL_HALO, :]

    x = h_ref[0]
    ext_ref[POOL_HALO:POOL_HALO + ts, :] = _rms(x) * g_ref[...]
    pos = pos0 + i * ts + lax.broadcasted_iota(jnp.int32, (ts, 1), 0)
    for gi, w in enumerate(POOL_WINDOWS):
        cols = slice(gi * group_dim, (gi + 1) * group_dim)
        cur = ext_ref[POOL_HALO:POOL_HALO + ts, cols]
        s = cur
        for u in range(1, w):
            s = s + ext_ref[POOL_HALO - u:POOL_HALO - u + ts, cols]
        cnt = jnp.minimum(w, pos + 1).astype(F32)
        mix_ref[0, :, cols] = (s / cnt - cur).astype(mix_ref.dtype)
    st_ref[0] = ext_ref[ts:ts + POOL_HALO, :]


def _pool_mix(h, buf, g, pos0):
    B, S, D = h.shape
    ts = min(S, 256)
    assert S % ts == 0 and (ts >= POOL_HALO or S == ts) and ts % 8 == 0
    buf16 = jnp.pad(buf, ((0, 0), (POOL_HALO - POOL_BUF, 0), (0, 0)))
    mix, st = pl.pallas_call(
        functools.partial(_pool_mix_kernel, ts=ts, pos0=pos0, group_dim=D // len(POOL_WINDOWS)),
        grid=(B, S // ts),
        in_specs=[pl.BlockSpec((1, ts, D), lambda b, i: (b, i, 0)),
                  pl.BlockSpec((1, POOL_HALO, D), lambda b, i: (b, 0, 0)),
                  pl.BlockSpec((1, D), lambda b, i: (0, 0))],
        out_specs=[pl.BlockSpec((1, ts, D), lambda b, i: (b, i, 0)),
                   pl.BlockSpec((1, POOL_HALO, D), lambda b, i: (b, 0, 0))],
        out_shape=[jax.ShapeDtypeStruct((B, S, D), BF16),
                   jax.ShapeDtypeStruct((B, POOL_HALO, D), F32)],
        scratch_shapes=[pltpu.VMEM((POOL_HALO + ts, D), F32)],
        compiler_params=_params("parallel", "arbitrary"),
        name="pool_mix",
    )(h, buf16, g.reshape(1, D))
    return mix, st[:, POOL_HALO - POOL_BUF:]


def _proj_res_kernel(*refs, groups, has_scale):
    if has_scale:
        a_ref, w_ref, scale_ref, g_ref, res_ref, o_ref = refs
    else:
        a_ref, w_ref, g_ref, res_ref, o_ref = refs
    gd = a_ref.shape[1] // groups
    parts = [_dot(a_ref[:, gi * gd:(gi + 1) * gd], w_ref[gi]) for gi in range(groups)]
    y = parts[0] if groups == 1 else jnp.concatenate(parts, axis=-1)
    if has_scale:
        y = y * scale_ref[...]
    o_ref[...] = res_ref[...] + _rms(y) * g_ref[...]


def _proj_res(a, w, scale, g, res, name):
    M, D = a.shape
    groups = w.shape[0]
    tm = min(M, 256)
    assert M % tm == 0
    row = lambda i: (i, 0)
    fixed = lambda i: (0, 0)
    in_specs = [pl.BlockSpec((tm, D), row), pl.BlockSpec(w.shape, lambda i: (0, 0, 0))]
    args = [a, w]
    if scale is not None:
        in_specs.append(pl.BlockSpec((1, D), fixed))
        args.append(scale.reshape(1, D))
    in_specs += [pl.BlockSpec((1, D), fixed), pl.BlockSpec((tm, D), row)]
    args += [g.reshape(1, D), res]
    return pl.pallas_call(
        functools.partial(_proj_res_kernel, groups=groups, has_scale=scale is not None),
        grid=(M // tm,),
        in_specs=in_specs,
        out_specs=pl.BlockSpec((tm, D), row),
        out_shape=jax.ShapeDtypeStruct((M, D), F32),
        compiler_params=_params("parallel"),
        name=name,
    )(*args)


def _ffn_kernel(h_ref, gi_ref, go_ref, wg_ref, wu_ref, wo_ref, out_ref, xn_ref, acc_ref):
    f = pl.program_id(1)

    @pl.when(f == 0)
    def _():
        xn_ref[...] = (_rms(h_ref[...]) * gi_ref[...]).astype(xn_ref.dtype)
        acc_ref[...] = jnp.zeros_like(acc_ref)

    xn = xn_ref[...]
    gate = _dot(xn, wg_ref[...])
    up = _dot(xn, wu_ref[...])
    act = (gate * jax.nn.sigmoid(gate) * up).astype(BF16)
    acc_ref[...] += _dot(act, wo_ref[...])

    @pl.when(f == pl.num_programs(1) - 1)
    def _():
        out_ref[...] = h_ref[...] + _rms(acc_ref[...]) * go_ref[...]


def _ffn(h, g_in, g_out, w_in, w_out):
    M, D = h.shape
    F = w_out.shape[0]
    tm = min(M, 512)
    tf = 512
    assert M % tm == 0 and F % tf == 0
    nf = F // tf
    return pl.pallas_call(
        _ffn_kernel,
        grid=(M // tm, nf),
        in_specs=[pl.BlockSpec((tm, D), lambda i, f: (i, 0)),
                  pl.BlockSpec((1, D), lambda i, f: (0, 0)),
                  pl.BlockSpec((1, D), lambda i, f: (0, 0)),
                  pl.BlockSpec((D, tf), lambda i, f: (0, f)),
                  pl.BlockSpec((D, tf), lambda i, f: (0, nf + f)),
                  pl.BlockSpec((tf, D), lambda i, f: (f, 0))],
        out_specs=pl.BlockSpec((tm, D), lambda i, f: (i, 0)),
        out_shape=jax.ShapeDtypeStruct((M, D), F32),
        scratch_shapes=[pltpu.VMEM((tm, D), BF16), pltpu.VMEM((tm, D), F32)],
        compiler_params=_params("parallel", "arbitrary"),
        name="ffn",
    )(h, g_in.reshape(1, D), g_out.reshape(1, D), w_in, w_in, w_out)


def _kv_proj_kernel(x_ref, g_ref, *refs, emit_aux):
    w_refs, out_refs = refs[:N_BRANCH], refs[N_BRANCH:]
    tm = x_ref.shape[0]
    xn = (_rms(x_ref[...]) * g_ref[...]).astype(BF16)
    for br in range(N_BRANCH):
        y = _dot(xn, w_refs[br][...])
        for j in range(ROW_SPLIT):
            out_refs[br][pl.ds(j, tm, stride=ROW_SPLIT), :] = y[:, j * HEAD_DIM:(j + 1) * HEAD_DIM]
        if emit_aux and br >= 1:
            kb_ref = out_refs[N_BRANCH + 2 * (br - 1)]
            vt_ref = out_refs[N_BRANCH + 2 * (br - 1) + 1]
            for h in range(N_KV_HEADS):
                kb_ref[h] = y[:, h * HEAD_DIM:(h + 1) * HEAD_DIM].astype(BF16)
                vcol = (N_KV_HEADS + h) * HEAD_DIM
                for r in range(tm // PAGE_ROWS):
                    v = y[r * PAGE_ROWS:(r + 1) * PAGE_ROWS, vcol:vcol + HEAD_DIM]
                    vt_ref[h, r] = v.T.astype(BF16)


def _kv_proj(x, g, weights, emit_aux):
    M, D = x.shape
    tm = min(M, 256)
    assert M % tm == 0 and (tm % PAGE_ROWS == 0 or not emit_aux)
    out_specs = [pl.BlockSpec((tm * ROW_SPLIT, HEAD_DIM), lambda i: (i, 0))] * N_BRANCH
    out_shape = [jax.ShapeDtypeStruct((M * ROW_SPLIT, HEAD_DIM), F32)] * N_BRANCH
    if emit_aux:
        tiles = tm // PAGE_ROWS
        for _ in range(2):
            out_specs += [pl.BlockSpec((N_KV_HEADS, tm, HEAD_DIM), lambda i: (0, i, 0)),
                          pl.BlockSpec((N_KV_HEADS, tiles, HEAD_DIM, PAGE_ROWS), lambda i: (0, i, 0, 0))]
            out_shape += [jax.ShapeDtypeStruct((N_KV_HEADS, M, HEAD_DIM), BF16),
                          jax.ShapeDtypeStruct((N_KV_HEADS, M // PAGE_ROWS, HEAD_DIM, PAGE_ROWS), BF16)]
    return pl.pallas_call(
        functools.partial(_kv_proj_kernel, emit_aux=emit_aux),
        grid=(M // tm,),
        in_specs=[pl.BlockSpec((tm, D), lambda i: (i, 0)), pl.BlockSpec((1, D), lambda i: (0, 0))]
        + [pl.BlockSpec(w.shape, lambda i: (0, 0)) for w in weights],
        out_specs=out_specs,
        out_shape=out_shape,
        compiler_params=_params("parallel"),
        name="kv_proj",
    )(x, g.reshape(1, D), *weights)


def _q_proj_kernel(x_ref, g_ref, wq_ref, wg_ref, q_ref, gate_ref, *, transposed):
    tm = x_ref.shape[0]
    xn = (_rms(x_ref[...]) * g_ref[...]).astype(BF16)
    q = _dot(xn, wq_ref[...]) * (HEAD_DIM ** -0.5)
    gates = jax.nn.sigmoid(_dot(xn, wg_ref[...]))
    if not transposed:
        q_ref[...] = q.astype(BF16)
        gate_ref[...] = gates
        return
    for r in range(tm // PAGE_ROWS):
        rows = slice(r * PAGE_ROWS, (r + 1) * PAGE_ROWS)
        gt = gates[rows].T
        for hk in range(N_KV_HEADS):
            gate_ref[hk, r] = gt[hk * GATE_PAD:(hk + 1) * GATE_PAD]
            for g in range(GQA_GROUP):
                col = (hk * GQA_GROUP + g) * HEAD_DIM
                q_ref[hk, r, :, g * PAGE_ROWS:(g + 1) * PAGE_ROWS] = q[rows, col:col + HEAD_DIM].T.astype(BF16)


def _q_proj(x, g, wq, wg, transposed):
    M, D = x.shape
    tm = min(M, 256)
    assert M % tm == 0 and (tm % PAGE_ROWS == 0 or not transposed)
    tiles = tm // PAGE_ROWS
    gw = GQA_GROUP * PAGE_ROWS
    if transposed:
        out_specs = [pl.BlockSpec((N_KV_HEADS, tiles, HEAD_DIM, gw), lambda i: (0, i, 0, 0)),
                     pl.BlockSpec((N_KV_HEADS, tiles, GATE_PAD, PAGE_ROWS), lambda i: (0, i, 0, 0))]
        out_shape = [jax.ShapeDtypeStruct((N_KV_HEADS, M // PAGE_ROWS, HEAD_DIM, gw), BF16),
                     jax.ShapeDtypeStruct((N_KV_HEADS, M // PAGE_ROWS, GATE_PAD, PAGE_ROWS), F32)]
    else:
        out_specs = [pl.BlockSpec((tm, wq.shape[1]), lambda i: (i, 0)),
                     pl.BlockSpec((tm, wg.shape[1]), lambda i: (i, 0))]
        out_shape = [jax.ShapeDtypeStruct((M, wq.shape[1]), BF16), jax.ShapeDtypeStruct((M, wg.shape[1]), F32)]
    return pl.pallas_call(
        functools.partial(_q_proj_kernel, transposed=transposed),
        grid=(M // tm,),
        in_specs=[pl.BlockSpec((tm, D), lambda i: (i, 0)), pl.BlockSpec((1, D), lambda i: (0, 0)),
                  pl.BlockSpec(wq.shape, lambda i: (0, 0)), pl.BlockSpec(wg.shape, lambda i: (0, 0))],
        out_specs=out_specs,
        out_shape=out_shape,
        compiler_params=_params("parallel"),
        name="q_proj",
    )(x, g.reshape(1, D), wq, wg)


def _gelu_tanh(x):
    return 0.5 * x * (1.0 + jnp.tanh(math.sqrt(2.0 / math.pi) * (x + 0.044715 * (x * x * x))))


def _compress_kernel(pt_ref, *refs, n_pages):
    del pt_ref
    page_refs = refs[:n_pages]
    w1_ref, pos_ref, b1_ref, w2_ref, o_ref = refs[n_pages:]
    sub_per_page = PAGE_ROWS // CMP_STRIDE
    n_sub = n_pages * sub_per_page
    rows = N_KV_HEADS * n_sub
    sub_stride = CMP_STRIDE * ROW_SPLIT
    row_in_head = lax.broadcasted_iota(jnp.int32, (n_sub, 1), 0)
    for c in range(2):
        first = jnp.zeros((rows, HEAD_DIM), F32)
        second = jnp.zeros((rows, HEAD_DIM), F32)
        for j in range(CMP_STRIDE):
            pieces = []
            for h in range(N_KV_HEADS):
                start = j * ROW_SPLIT + c * N_KV_HEADS + h
                pieces += [r[pl.ds(start, sub_per_page, stride=sub_stride), :] for r in page_refs]
            x = jnp.concatenate(pieces, axis=0)
            xa = (x + pos_ref[c, j:j + 1, :]).astype(BF16)
            xb = (x + pos_ref[c, CMP_STRIDE + j:CMP_STRIDE + j + 1, :]).astype(BF16)
            first = first + _dot(xa, w1_ref[c, j])
            second = second + _dot(xb, w1_ref[c, CMP_STRIDE + j])
        nxt = pltpu.roll(second, rows - 1, 0)
        hid = _gelu_tanh(first + nxt + b1_ref[c])
        out = _dot(hid.astype(BF16), w2_ref[c])
        for h in range(N_KV_HEADS):
            col = (c * N_KV_HEADS + h) * HEAD_DIM
            blk = out[h * n_sub:(h + 1) * n_sub]
            o_ref[0, :, col:col + HEAD_DIM] = jnp.where(row_in_head < n_sub - 1, blk, 0.0)


def _compress(pages, table, w1, pos, b1, w2):
    nb, n_pages = table.shape
    n_sub = n_pages * (PAGE_ROWS // CMP_STRIDE)
    page_rows2d = PAGE_ROWS * ROW_SPLIT
    page_specs = [pl.BlockSpec((page_rows2d, HEAD_DIM), functools.partial(lambda b, pt, p: (pt[b, p], 0), p=p))
                  for p in range(n_pages)]
    const = lambda shape: pl.BlockSpec(shape, lambda b, pt: (0,) * len(shape))
    grid_spec = pltpu.PrefetchScalarGridSpec(
        num_scalar_prefetch=1,
        grid=(nb,),
        in_specs=page_specs + [const(w1.shape), const(pos.shape), const(b1.shape), const(w2.shape)],
        out_specs=pl.BlockSpec((1, n_sub, KV_COLS), lambda b, pt: (b, 0, 0)),
    )
    return pl.pallas_call(
        functools.partial(_compress_kernel, n_pages=n_pages),
        grid_spec=grid_spec,
        out_shape=jax.ShapeDtypeStruct((nb, n_sub, KV_COLS), F32),
        compiler_params=_params("parallel"),
        name="compress",
    )(table, *([pages] * n_pages), w1, pos, b1, w2)


def _rel_bucket_np(dist):
    n = np.maximum(dist, 0)
    max_exact = N_BUCKETS // 2
    nf = np.maximum(n, 1).astype(np.float32)
    large = max_exact + (np.log(nf / np.float32(max_exact)) / np.float32(math.log(MAX_DISTANCE / max_exact))
                         * np.float32(N_BUCKETS - max_exact)).astype(np.int32)
    large = np.minimum(large, N_BUCKETS - 1)
    return np.where(n < max_exact, n, large).astype(np.int32)


def _bias_table_kernel(rb_ref, bk_ref, o_ref, *, minus_far):
    bk = bk_ref[...]
    for h in range(N_HEADS):
        v = jnp.full(bk.shape, rb_ref[0, h], F32)
        for b in range(1, N_BUCKETS):
            v = jnp.where(bk == b, rb_ref[b, h], v)
        if minus_far:
            v = v - rb_ref[N_BUCKETS - 1, h]
        o_ref[h] = v


def _bias_table(rel_bias, bucket, minus_far=False):
    R, C = bucket.shape
    tr = min(R, 128)
    assert R % tr == 0
    return pl.pallas_call(
        functools.partial(_bias_table_kernel, minus_far=minus_far),
        grid=(R // tr,),
        in_specs=[pl.BlockSpec(memory_space=pltpu.SMEM), pl.BlockSpec((tr, C), lambda r: (r, 0))],
        out_specs=pl.BlockSpec((N_HEADS, tr, C), lambda r: (0, r, 0)),
        out_shape=jax.ShapeDtypeStruct((N_HEADS, R, C), F32),
        compiler_params=_params("parallel"),
        name="bias_table",
    )(rel_bias, jnp.asarray(bucket))


def _overlap_np(n_sub, n_slc):
    n_cmp = n_sub - 1
    cs = np.arange(n_cmp) * CMP_STRIDE
    ce = cs + CMP_BLOCK - 1
    ss = np.arange(n_slc) * SLC_BLOCK
    se = ss + SLC_BLOCK - 1
    ov = np.minimum(ce[:, None], se[None, :]) - np.maximum(cs[:, None], ss[None, :]) + 1
    out = np.zeros((n_sub, n_slc), np.float32)
    out[:n_cmp] = np.maximum(ov, 0).astype(np.float32) / CMP_BLOCK
    return out


def _pad_to(a, shape):
    return np.pad(a, [(0, s - d) for d, s in zip(a.shape, shape)])


def _score(imp, j, qb, n_slc):
    prio = 2.0 * (j == qb).astype(F32) + (j == qb - 1).astype(F32) + (j == 0).astype(F32)
    score = jnp.where(j <= qb, imp + SEL_BIG * prio, -SEL_BIG)
    return jnp.where(j < n_slc, score, -3e38)


def _attn_prompt_kernel(qt_ref, gt_ref, ck_ref, cv_ref, sk_ref, svt_ref, wk_ref, wvt_ref,
                        toe_ref, cmpb_ref, ovt_ref, o_ref,
                        sel_ref, m_ref, l_ref, acc_ref, *, n_slc, n_sub):
    tq = PAGE_ROWS
    i = pl.program_id(2)
    qt = qt_ref[0, 0]
    t_idx = lax.broadcasted_iota(jnp.int32, (1, tq), 1)
    qpos = i * tq + t_idx
    lane4 = lambda x: jnp.concatenate([x] * GQA_GROUP, axis=1)

    kc = ck_ref[0].astype(BF16)
    vct = cv_ref[0].T.astype(BF16)
    n_idx = lax.broadcasted_iota(jnp.int32, (n_sub, 1), 0)
    valid = lane4((n_idx * CMP_STRIDE + (CMP_BLOCK - 1) <= qpos) & (n_idx < n_sub - 1))
    bias_c = jnp.concatenate([cmpb_ref[g, 0] for g in range(GQA_GROUP)], axis=1)
    logits = jnp.where(valid, _dot(kc, qt) + bias_c, NEG_INF)
    p = jnp.exp(logits - jnp.max(logits, axis=0, keepdims=True))
    p = p * (1.0 / jnp.sum(p, axis=0, keepdims=True))
    p = jnp.where(valid, p, 0.0)
    o_c = _dot(vct, p.astype(BF16))
    psum = p[:, 0:tq]
    for g in range(1, GQA_GROUP):
        psum = psum + p[:, g * tq:(g + 1) * tq]
    imp = jnp.dot(ovt_ref[...], psum, precision=lax.Precision.HIGHEST, preferred_element_type=F32)
    nsp = imp.shape[0]
    j = lax.broadcasted_iota(jnp.int32, (nsp, 1), 0)
    score = _score(imp, j, qpos >> int(math.log2(SLC_BLOCK)), n_slc)
    rank = jnp.zeros((nsp, tq), jnp.int32)
    for jj in range(n_slc):
        row = score[jj:jj + 1, :]
        rank = rank + ((row > score) | ((row == score) & (jj < j))).astype(jnp.int32)
    madd = jnp.where(rank < min(N_SELECT, n_slc), 0.0, NEG_INF)
    for jj in range(n_slc):
        sel_ref[jj] = jnp.broadcast_to(madd[jj:jj + 1, :], (8, tq))

    k_row = lax.broadcasted_iota(jnp.int32, (tq, 1), 0)
    halves = SLC_BLOCK // 8

    def chunk(k, kinds, k_ref, vt_ref, use_sel, fresh):
        nt = len(kinds)
        k0 = pl.multiple_of(k * tq, tq)
        s = _dot(k_ref[0, pl.ds(k0, nt * tq), :], qt)
        parts = []
        for jx, kind in enumerate(kinds):
            sj = s[jx * tq:(jx + 1) * tq]
            if kind in ("near", "diag"):
                d = 0 if kind == "diag" else 1
                sj = sj + jnp.concatenate([toe_ref[g * 2 + d] for g in range(GQA_GROUP)], axis=1)
            if use_sel:
                blk = [jnp.concatenate([sel_ref[2 * (k + jx) + b]] * halves, axis=0)
                       for b in range(tq // SLC_BLOCK)]
                sj = sj + lane4(jnp.concatenate(blk, axis=0))
            if kind == "diag":
                sj = jnp.where(lane4(k_row <= t_idx), sj, NEG_INF)
            elif kind == "edge":
                sj = jnp.where(lane4(k_row > t_idx), sj, NEG_INF)
            parts.append(sj)
        m_new = functools.reduce(jnp.maximum, parts)
        m_new = jnp.max(m_new, axis=0, keepdims=True)
        if not fresh:
            m_old = m_ref[...]
            m_new = jnp.maximum(m_old, m_new)
            alpha = jnp.exp(m_old - m_new)
        ps = [jnp.exp(sj - m_new) for sj in parts]
        lsum = jnp.sum(functools.reduce(jnp.add, ps), axis=0, keepdims=True)
        vt = jnp.concatenate([vt_ref[0, k + jx] for jx in range(nt)], axis=1)
        pv = _dot(vt, jnp.concatenate(ps, axis=0).astype(BF16))
        if fresh:
            acc_ref[...] = pv * (1.0 / lsum)
        else:
            l_ref[...] = alpha * l_ref[...] + lsum
            acc_ref[...] = alpha * acc_ref[...] + pv
            m_ref[...] = m_new

    def tail_kinds(n_far):
        return ("far",) * n_far + ("near", "diag")

    far_step = 4
    m_ref[...] = jnp.full(m_ref.shape, NEG_INF, F32)
    l_ref[...] = jnp.zeros(l_ref.shape, F32)
    acc_ref[...] = jnp.zeros(acc_ref.shape, F32)

    def far_body(c, carry):
        chunk(c * far_step, ("far",) * far_step, sk_ref, svt_ref, True, False)
        return carry

    lax.fori_loop(0, jnp.maximum(i - 1, 0) // far_step, far_body, 0)

    @pl.when(i == 0)
    def _():
        chunk(0, ("diag",), sk_ref, svt_ref, True, False)

    for r in range(far_step):
        @pl.when((i >= 1) & (lax.rem(i + far_step - 1, far_step) == r))
        def _(r=r):
            chunk(i - 1 - r, tail_kinds(r), sk_ref, svt_ref, True, False)

    o_s = acc_ref[...] * (1.0 / l_ref[...])

    wt = WINDOW // tq
    nq = pl.num_programs(2)
    for c in range(wt):
        @pl.when((i == c) & (c < nq))
        def _(c=c):
            chunk(0, tail_kinds(c - 1) if c >= 1 else ("diag",), wk_ref, wvt_ref, False, True)

    @pl.when(i >= wt)
    def _():
        chunk(i - wt, ("edge",) + tail_kinds(wt - 2), wk_ref, wvt_ref, False, True)

    o_w = acc_ref[...]

    gt = gt_ref[0, 0]
    for g in range(GQA_GROUP):
        lanes = slice(g * tq, (g + 1) * tq)
        o = (gt[g:g + 1] * o_c[:, lanes] + gt[GQA_GROUP + g:GQA_GROUP + g + 1] * o_s[:, lanes]
             + gt[2 * GQA_GROUP + g:2 * GQA_GROUP + g + 1] * o_w[:, lanes])
        o_ref[0, :, g * HEAD_DIM:(g + 1) * HEAD_DIM] = o.T.astype(o_ref.dtype)


def _attn_prompt(qt, gt, ckv, sk, svt, wk, wvt, toe, cmpb, B, S):
    tq = PAGE_ROWS
    nq = S // tq
    n_sub = ckv.shape[1]
    n_slc = -(-S // SLC_BLOCK)
    nsp = -(-n_slc // 8) * 8
    assert S % tq == 0 and 2 * nq <= nsp + 1
    ovt = jnp.asarray(_pad_to(_overlap_np(n_sub, n_slc).T, (nsp, n_sub)))
    gw = GQA_GROUP * tq
    return pl.pallas_call(
        functools.partial(_attn_prompt_kernel, n_slc=n_slc, n_sub=n_sub),
        grid=(B, N_KV_HEADS, nq),
        in_specs=[pl.BlockSpec((1, 1, HEAD_DIM, gw), lambda b, h, i: (h, b * nq + i, 0, 0)),
                  pl.BlockSpec((1, 1, GATE_PAD, tq), lambda b, h, i: (h, b * nq + i, 0, 0)),
                  pl.BlockSpec((1, n_sub, HEAD_DIM), lambda b, h, i: (b, 0, h)),
                  pl.BlockSpec((1, n_sub, HEAD_DIM), lambda b, h, i: (b, 0, N_KV_HEADS + h)),
                  pl.BlockSpec((1, S, HEAD_DIM), lambda b, h, i: (h, b, 0)),
                  pl.BlockSpec((1, nq, HEAD_DIM, tq), lambda b, h, i: (h, b, 0, 0)),
                  pl.BlockSpec((1, S, HEAD_DIM), lambda b, h, i: (h, b, 0)),
                  pl.BlockSpec((1, nq, HEAD_DIM, tq), lambda b, h, i: (h, b, 0, 0)),
                  pl.BlockSpec((GQA_GROUP * 2, tq, tq), lambda b, h, i: (h, 0, 0)),
                  pl.BlockSpec((GQA_GROUP, 1, n_sub, tq), lambda b, h, i: (h, i, 0, 0)),
                  pl.BlockSpec(ovt.shape, lambda b, h, i: (0, 0))],
        out_specs=pl.BlockSpec((1, tq, GQA_GROUP * HEAD_DIM), lambda b, h, i: (b, i, h)),
        out_shape=jax.ShapeDtypeStruct((B, S, N_HEADS * HEAD_DIM), BF16),
        scratch_shapes=[pltpu.VMEM((nsp, 8, tq), F32),
                        pltpu.VMEM((1, gw), F32), pltpu.VMEM((1, gw), F32),
                        pltpu.VMEM((HEAD_DIM, gw), F32)],
        compiler_params=_params("parallel", "parallel", "arbitrary"),
        name="attn_prompt",
    )(qt, gt, ckv, ckv, sk, svt, wk, wvt, toe, cmpb, ovt)


def _attn_sample_kernel(pt_ref, *refs, n_pages, n_slc, n_sub, past_len, win_buf, ds):
    del pt_ref
    page_refs = refs[:n_pages]
    (q_ref, gate_ref, ckv_ref, snew_ref, wst_ref, wnew_ref, bias_ref, ov_ref, e_ref,
     o_ref, wout_ref) = refs[n_pages:]
    nk_s = (n_pages + 1) * PAGE_ROWS
    nk_w = win_buf + PAGE_ROWS
    t_idx = lax.broadcasted_iota(jnp.int32, (ds, 1), 0)
    qpos = past_len + t_idx
    pad = jnp.zeros((PAGE_ROWS - ds, HEAD_DIM), F32)
    gates = gate_ref[0]
    row4 = lambda x: jnp.concatenate([x] * GQA_GROUP, axis=0)
    head_rows = lambda r, col, n: r[pl.ds(col, n, stride=ROW_SPLIT), :]

    shift = ds * ROW_SPLIT
    wout_ref[0:win_buf * ROW_SPLIT - shift, :] = wst_ref[shift:win_buf * ROW_SPLIT, :]
    wout_ref[win_buf * ROW_SPLIT - shift:win_buf * ROW_SPLIT, :] = wnew_ref[...]

    ks_pos = lax.broadcasted_iota(jnp.int32, (1, nk_s), 1)
    dist_w = win_buf + t_idx - lax.broadcasted_iota(jnp.int32, (1, nk_w), 1)
    ok_w4 = row4((dist_w >= 0) & (dist_w < WINDOW))
    n_idx = lax.broadcasted_iota(jnp.int32, (1, n_sub), 1)
    ok_c4 = row4((n_idx * CMP_STRIDE + (CMP_BLOCK - 1) <= qpos) & (n_idx < n_sub - 1))
    lanes = ov_ref.shape[1]
    j = lax.broadcasted_iota(jnp.int32, (1, lanes), 1)
    qb = qpos >> int(math.log2(SLC_BLOCK))

    def softmax(logits, ok4):
        logits = jnp.where(ok4, logits, NEG_INF)
        p = jnp.exp(logits - jnp.max(logits, axis=-1, keepdims=True))
        return p / jnp.sum(p, axis=-1, keepdims=True)

    for hk in range(N_KV_HEADS):
        kcol, vcol = hk, N_KV_HEADS + hk
        qcol = hk * GQA_GROUP * HEAD_DIM
        q4 = jnp.concatenate([q_ref[0, :, qcol + g * HEAD_DIM:qcol + (g + 1) * HEAD_DIM]
                              for g in range(GQA_GROUP)], axis=0)
        hb = bias_ref[hk * GQA_GROUP:(hk + 1) * GQA_GROUP].reshape(GQA_GROUP * ds, bias_ref.shape[2])

        kc = ckv_ref[0, :, kcol * HEAD_DIM:(kcol + 1) * HEAD_DIM].astype(BF16)
        vc = ckv_ref[0, :, vcol * HEAD_DIM:(vcol + 1) * HEAD_DIM].astype(BF16)
        p = jnp.where(ok_c4, softmax(_dot_nt(q4, kc) + hb[:, nk_s + nk_w:nk_s + nk_w + n_sub], ok_c4), 0.0)
        o_c = _dot(p.astype(BF16), vc)
        psum = p[0:ds]
        for g in range(1, GQA_GROUP):
            psum = psum + p[g * ds:(g + 1) * ds]
        imp = jnp.dot(psum, ov_ref[...], precision=lax.Precision.HIGHEST, preferred_element_type=F32)
        score = _score(imp, j, qb, n_slc)
        rank = jnp.zeros((ds, lanes), jnp.int32)
        for jj in range(n_slc):
            col = score[:, jj:jj + 1]
            rank = rank + ((col > score) | ((col == score) & (jj < j))).astype(jnp.int32)
        sel = ((rank < min(N_SELECT, n_slc)) & (j < n_slc)).astype(BF16)
        ok_s4 = row4((_dot(sel, e_ref[...]) > 0.5) & (ks_pos <= qpos))

        k_s = jnp.concatenate([head_rows(r, kcol, PAGE_ROWS) for r in page_refs]
                              + [head_rows(snew_ref, kcol, ds), pad], axis=0).astype(BF16)
        v_s = jnp.concatenate([head_rows(r, vcol, PAGE_ROWS) for r in page_refs]
                              + [head_rows(snew_ref, vcol, ds), pad], axis=0).astype(BF16)
        o_s = _dot(softmax(_dot_nt(q4, k_s) + hb[:, 0:nk_s], ok_s4).astype(BF16), v_s)

        k_w = jnp.concatenate([head_rows(wst_ref, kcol, win_buf), head_rows(wnew_ref, kcol, ds), pad],
                              axis=0).astype(BF16)
        v_w = jnp.concatenate([head_rows(wst_ref, vcol, win_buf), head_rows(wnew_ref, vcol, ds), pad],
                              axis=0).astype(BF16)
        o_w = _dot(softmax(_dot_nt(q4, k_w) + hb[:, nk_s:nk_s + nk_w], ok_w4).astype(BF16), v_w)

        gcol = hk * GATE_PAD
        for g in range(GQA_GROUP):
            rows = slice(g * ds, (g + 1) * ds)
            gate = lambda br, g=g: gates[:, gcol + br * GQA_GROUP + g:gcol + br * GQA_GROUP + g + 1]
            o = gate(0) * o_c[rows] + gate(1) * o_s[rows] + gate(2) * o_w[rows]
            o_ref[0, :, qcol + g * HEAD_DIM:qcol + (g + 1) * HEAD_DIM] = o.astype(o_ref.dtype)


def _attn_sample(q, gates, ckv, cache_slc, table, slc_new, win_state, win_new, bias, past_len, win_buf):
    DB, ds, _ = q.shape
    n_pages = table.shape[1]
    n_sub = ckv.shape[1]
    n_slc = -(-(past_len + ds) // SLC_BLOCK)
    lanes = PAGE_ROWS
    assert ds % 8 == 0 and ds <= PAGE_ROWS and n_slc <= lanes and n_sub <= lanes
    assert win_buf <= past_len and win_buf > ds
    nk_s = (n_pages + 1) * PAGE_ROWS
    ov = jnp.asarray(_pad_to(_overlap_np(n_sub, n_slc), (n_sub, lanes)))
    e = (np.arange(nk_s)[None, :] // SLC_BLOCK == np.arange(lanes)[:, None]) & (np.arange(lanes)[:, None] < n_slc)
    e = jnp.asarray(e.astype(np.float32), dtype=BF16)
    page_specs = [pl.BlockSpec((PAGE_ROWS * ROW_SPLIT, HEAD_DIM),
                               functools.partial(lambda b, pt, p: (pt[b, p], 0), p=p)) for p in range(n_pages)]
    per_b3 = lambda shape: pl.BlockSpec((1,) + shape[1:], lambda b, pt: (b, 0, 0))
    rows_b = lambda n: pl.BlockSpec((n * ROW_SPLIT, HEAD_DIM), lambda b, pt: (b, 0))
    const = lambda shape: pl.BlockSpec(shape, lambda b, pt: (0,) * len(shape))
    grid_spec = pltpu.PrefetchScalarGridSpec(
        num_scalar_prefetch=1,
        grid=(DB,),
        in_specs=page_specs + [per_b3(q.shape), per_b3(gates.shape), per_b3(ckv.shape), rows_b(ds),
                               rows_b(win_buf), rows_b(ds), const(bias.shape), const(ov.shape), const(e.shape)],
        out_specs=[per_b3(q.shape), rows_b(win_buf)],
    )
    return pl.pallas_call(
        functools.partial(_attn_sample_kernel, n_pages=n_pages, n_slc=n_slc, n_sub=n_sub,
                          past_len=past_len, win_buf=win_buf, ds=ds),
        grid_spec=grid_spec,
        out_shape=[jax.ShapeDtypeStruct(q.shape, BF16), jax.ShapeDtypeStruct(win_state.shape, F32)],
        compiler_params=_params("parallel"),
        name="attn_sample",
    )(table, *([cache_slc] * n_pages), q, gates, ckv, slc_new, win_state, win_new, bias, ov, e)


def _prompt_buckets(S, n_sub):
    tq = PAGE_ROWS
    k = np.arange(tq)[:, None]
    t = np.arange(tq)[None, :]
    toe = np.concatenate([_rel_bucket_np(t - k + d * tq) for d in range(2)], axis=0)
    kpos_c = np.arange(n_sub)[:, None] * CMP_STRIDE + CMP_BLOCK - 1
    cmp = np.concatenate([_rel_bucket_np(i * tq + t - kpos_c) for i in range(S // tq)], axis=0)
    return toe, cmp


def _sample_buckets(ds, past_len, n_pages, win_buf, n_sub):
    qpos = past_len + np.arange(ds)[:, None]
    ks = np.arange((n_pages + 1) * PAGE_ROWS)[None, :]
    kw = past_len - win_buf + np.arange(win_buf + PAGE_ROWS)[None, :]
    kc = np.arange(n_sub)[None, :] * CMP_STRIDE + CMP_BLOCK - 1
    return np.concatenate([_rel_bucket_np(qpos - ks), _rel_bucket_np(qpos - kw), _rel_bucket_np(qpos - kc)], axis=1)


def kernel(x_prompt, x_sample, state_pool, cache_cmp_kv, cache_slc_kv, state_win_kv, page_table, norms, w_pool,
           pool_scale, kv_norm, w_kv, cmp_pos, cmp_w1, cmp_b1, cmp_w2, w_q, w_o, rel_bias, w_ffn_in, w_ffn_out):
    B, S, D = x_prompt.shape
    DB, DS, _ = x_sample.shape
    n_pages = page_table.shape[1]
    past_len = n_pages * cache_slc_kv.shape[1]
    win_buf = state_win_kv.shape[1]
    assert norms.shape[0] == 2 and w_pool.shape[0] == 1 and w_q.shape[0] == 1, "one pooling + one attention layer"
    assert cache_slc_kv.shape[1] == PAGE_ROWS and S % PAGE_ROWS == 0
    assert past_len % CMP_STRIDE == 0 and DS < CMP_STRIDE and D == N_HEADS * HEAD_DIM
    assert S >= win_buf and N_BRANCH * GQA_GROUP <= GATE_PAD

    qdim = N_HEADS * HEAD_DIM
    w_pool_b = w_pool[0].astype(BF16)
    w_in_b = w_ffn_in.astype(BF16)
    w_out_b = w_ffn_out.astype(BF16)
    w_kv_b = [w_kv[:, k * KV_COLS:(k + 1) * KV_COLS].astype(BF16) for k in range(N_BRANCH)]
    w_qq = w_q[0, :, :qdim].astype(BF16)
    w_qg = w_q[0, :, qdim:].reshape(D, N_BRANCH, N_KV_HEADS, GQA_GROUP).transpose(0, 2, 1, 3)
    w_qg = jnp.pad(w_qg.reshape(D, N_KV_HEADS, N_BRANCH * GQA_GROUP),
                   ((0, 0), (0, 0), (0, GATE_PAD - N_BRANCH * GQA_GROUP)))
    w_qg = jnp.pad(w_qg.reshape(D, N_KV_HEADS * GATE_PAD), ((0, 0), (0, 128 - N_KV_HEADS * GATE_PAD))).astype(BF16)
    w_o_b = w_o.astype(BF16)
    w1_b = cmp_w1.astype(BF16)
    w2_b = cmp_w2.astype(BF16)
    b1 = cmp_b1.reshape(2, 1, HEAD_DIM)

    def layer0(x, buf, pos0):
        Bx, Sx, _ = x.shape
        M = Bx * Sx
        mix, st = _pool_mix(x, buf, norms[0, 0], pos0)
        h = _proj_res(mix.reshape(M, D), w_pool_b, pool_scale[0], norms[0, 1], x.reshape(M, D), "pool_proj")
        return _ffn(h, norms[0, 2], norms[0, 3], w_in_b[0], w_out_b[0]), st

    def layer1_tail(h1, o):
        h2 = _proj_res(o, w_o_b, None, norms[1, 1], h1, "o_proj")
        return _ffn(h2, norms[1, 2], norms[1, 3], w_in_b[1], w_out_b[1])

    rows2d = lambda a: a.reshape(-1, HEAD_DIM)
    kv5 = lambda r, b, s: r.reshape(b, s, 2, N_KV_HEADS, HEAD_DIM)

    h1, st_p = layer0(x_prompt, jnp.zeros((B, POOL_BUF, D), F32), 0)
    cmp_p, slc_p, win_p, sk, svt, wk, wvt = _kv_proj(h1, kv_norm, w_kv_b, True)
    qt, gt = _q_proj(h1, norms[1, 0], w_qq, w_qg, True)
    pages_p = S // PAGE_ROWS
    ckv_p = _compress(cmp_p, jnp.arange(B * pages_p, dtype=jnp.int32).reshape(B, pages_p), w1_b, cmp_pos, b1, w2_b)
    n_sub_p = ckv_p.shape[1]
    toe_bk, cmp_bk = _prompt_buckets(S, n_sub_p)
    toe = _bias_table(rel_bias, toe_bk, minus_far=True).reshape(N_HEADS * 2, PAGE_ROWS, PAGE_ROWS)
    cmpb = _bias_table(rel_bias, cmp_bk).reshape(N_HEADS, pages_p, n_sub_p, PAGE_ROWS)
    o_p = _attn_prompt(qt, gt, ckv_p, sk, svt, wk, wvt, toe, cmpb, B, S)
    y_prompt = layer1_tail(h1, o_p.reshape(B * S, qdim)).reshape(B, S, D)
    win_state_p = win_p.reshape(B, S * ROW_SPLIT, HEAD_DIM)[:, (S - win_buf) * ROW_SPLIT:]

    h1s, st_s = layer0(x_sample, state_pool[0], past_len)
    cmp_s, slc_s, win_s = _kv_proj(h1s, kv_norm, w_kv_b, False)
    q_s, gates_s = _q_proj(h1s, norms[1, 0], w_qq, w_qg, False)
    ckv_s = _compress(rows2d(cache_cmp_kv), page_table, w1_b, cmp_pos, b1, w2_b)
    bias_s = _bias_table(rel_bias, _sample_buckets(DS, past_len, n_pages, win_buf, ckv_s.shape[1]))
    o_s, win_state_s = _attn_sample(
        q_s.reshape(DB, DS, qdim), gates_s.reshape(DB, DS, 128), ckv_s, rows2d(cache_slc_kv), page_table,
        slc_s, rows2d(state_win_kv), win_s, bias_s, past_len, win_buf)
    y_sample = layer1_tail(h1s, o_s.reshape(DB * DS, qdim)).reshape(DB, DS, D)

    return (y_prompt, y_sample, st_p[None], st_s[None],
            kv5(cmp_p, B, S), kv5(cmp_s, DB, DS), kv5(slc_p, B, S), kv5(slc_s, DB, DS),
            kv5(win_state_p, B, win_buf), kv5(win_state_s, DB, win_buf))
```

```python
import functools
import math

import numpy as np
import jax
import jax.numpy as jnp
from jax import lax
from jax.experimental import pallas as pl
from jax.experimental.pallas import tpu as pltpu

F32 = jnp.float32
BF16 = jnp.bfloat16

HEAD_DIM = 128
N_KV_HEADS = 4
GQA_GROUP = 4
N_HEADS = N_KV_HEADS * GQA_GROUP
N_BRANCH = 3
POOL_WINDOWS = (2, 4, 8, 16)
POOL_BUF = max(POOL_WINDOWS) - 1
POOL_HALO = POOL_BUF + 1
CMP_BLOCK = 32
CMP_STRIDE = 16
SLC_BLOCK = 64
N_SELECT = 16
WINDOW = 512
N_BUCKETS = 32
MAX_DISTANCE = 128
RMS_EPS = 1e-6
SEL_BIG = 1e3
NEG_INF = -1e30
LOG2E = math.log2(math.e)
PAGE_ROWS = 128
KV_COLS = 2 * N_KV_HEADS * HEAD_DIM
ROW_SPLIT = 2 * N_KV_HEADS
GATE_PAD = 16
SEL_FAR_TILES = 4

VMEM_LIMIT_BYTES = 56 * 1024 * 1024

_NT = (((1,), (1,)), ((), ()))


def _params(*sem):
    return pltpu.CompilerParams(dimension_semantics=sem, vmem_limit_bytes=VMEM_LIMIT_BYTES)


def _rms(x):
    return x * lax.rsqrt(jnp.mean(x * x, axis=-1, keepdims=True) + RMS_EPS)


def _dot(a, b):
    return jnp.dot(a, b, preferred_element_type=F32)


def _dot_nt(a, b):
    return lax.dot_general(a, b, _NT, preferred_element_type=F32)


def _pool_mix_kernel(h_ref, buf_ref, g_ref, mix_ref, st_ref, ext_ref, *, ts, pos0, group_dim):
    i = pl.program_id(1)

    @pl.when(i == 0)
    def _():
        ext_ref[0:POOL_HALO, :] = buf_ref[0]

    @pl.when(i > 0)
    def _():
        ext_ref[0:POOL_HALO, :] = ext_ref[ts:ts + POOL_HALO, :]

    x = h_ref[0]
    ext_ref[POOL_HALO:POOL_HALO + ts, :] = _rms(x) * g_ref[...]
    pos = pos0 + i * ts + lax.broadcasted_iota(jnp.int32, (ts, 1), 0)
    for gi, w in enumerate(POOL_WINDOWS):
        cols = slice(gi * group_dim, (gi + 1) * group_dim)
        cur = ext_ref[POOL_HALO:POOL_HALO + ts, cols]
        s = cur
        for u in range(1, w):
            s = s + ext_ref[POOL_HALO - u:POOL_HALO - u + ts, cols]
        cnt = jnp.minimum(w, pos + 1).astype(F32)
        mix_ref[0, :, cols] = (s / cnt - cur).astype(mix_ref.dtype)
    st_ref[0] = ext_ref[ts:ts + POOL_HALO, :]


def _pool_mix(h, buf, g, pos0):
    B, S, D = h.shape
    ts = min(S, 256)
    assert S % ts == 0 and (ts >= POOL_HALO or S == ts) and ts % 8 == 0
    buf16 = jnp.pad(buf, ((0, 0), (POOL_HALO - POOL_BUF, 0), (0, 0)))
    mix, st = pl.pallas_call(
        functools.partial(_pool_mix_kernel, ts=ts, pos0=pos0, group_dim=D // len(POOL_WINDOWS)),
        grid=(B, S // ts),
        in_specs=[pl.BlockSpec((1, ts, D), lambda b, i: (b, i, 0)),
                  pl.BlockSpec((1, POOL_HALO, D), lambda b, i: (b, 0, 0)),
                  pl.BlockSpec((1, D), lambda b, i: (0, 0))],
        out_specs=[pl.BlockSpec((1, ts, D), lambda b, i: (b, i, 0)),
                   pl.BlockSpec((1, POOL_HALO, D), lambda b, i: (b, 0, 0))],
        out_shape=[jax.ShapeDtypeStruct((B, S, D), BF16),
                   jax.ShapeDtypeStruct((B, POOL_HALO, D), F32)],
        scratch_shapes=[pltpu.VMEM((POOL_HALO + ts, D), F32)],
        compiler_params=_params("parallel", "arbitrary"),
        name="pool_mix",
    )(h, buf16, g.reshape(1, D))
    return mix, st[:, POOL_HALO - POOL_BUF:]


def _proj_res_kernel(*refs, groups, has_scale):
    if has_scale:
        a_ref, w_ref, scale_ref, g_ref, res_ref, o_ref = refs
    else:
        a_ref, w_ref, g_ref, res_ref, o_ref = refs
    gd = a_ref.shape[1] // groups
    parts = [_dot(a_ref[:, gi * gd:(gi + 1) * gd], w_ref[gi]) for gi in range(groups)]
    y = parts[0] if groups == 1 else jnp.concatenate(parts, axis=-1)
    if has_scale:
        y = y * scale_ref[...]
    o_ref[...] = res_ref[...] + _rms(y) * g_ref[...]


def _proj_res(a, w, scale, g, res, name):
    M, D = a.shape
    groups = w.shape[0]
    tm = min(M, 256)
    assert M % tm == 0
    row = lambda i: (i, 0)
    fixed = lambda i: (0, 0)
    in_specs = [pl.BlockSpec((tm, D), row), pl.BlockSpec(w.shape, lambda i: (0, 0, 0))]
    args = [a, w]
    if scale is not None:
        in_specs.append(pl.BlockSpec((1, D), fixed))
        args.append(scale.reshape(1, D))
    in_specs += [pl.BlockSpec((1, D), fixed), pl.BlockSpec((tm, D), row)]
    args += [g.reshape(1, D), res]
    return pl.pallas_call(
        functools.partial(_proj_res_kernel, groups=groups, has_scale=scale is not None),
        grid=(M // tm,),
        in_specs=in_specs,
        out_specs=pl.BlockSpec((tm, D), row),
        out_shape=jax.ShapeDtypeStruct((M, D), F32),
        compiler_params=_params("parallel"),
        name=name,
    )(*args)


def _ffn_kernel(h_ref, gi_ref, go_ref, wg_ref, wu_ref, wo_ref, out_ref, xn_ref, acc_ref):
    f = pl.program_id(1)

    @pl.when(f == 0)
    def _():
        xn_ref[...] = (_rms(h_ref[...]) * gi_ref[...]).astype(xn_ref.dtype)
        acc_ref[...] = jnp.zeros_like(acc_ref)

    xn = xn_ref[...]
    gate = _dot(xn, wg_ref[...])
    up = _dot(xn, wu_ref[...])
    act = (gate * jax.nn.sigmoid(gate) * up).astype(BF16)
    acc_ref[...] += _dot(act, wo_ref[...])

    @pl.when(f == pl.num_programs(1) - 1)
    def _():
        out_ref[...] = h_ref[...] + _rms(acc_ref[...]) * go_ref[...]


def _ffn(h, g_in, g_out, w_in, w_out):
    M, D = h.shape
    F = w_out.shape[0]
    tm = min(M, 512)
    tf = 512
    assert M % tm == 0 and F % tf == 0
    nf = F // tf
    return pl.pallas_call(
        _ffn_kernel,
        grid=(M // tm, nf),
        in_specs=[pl.BlockSpec((tm, D), lambda i, f: (i, 0)),
                  pl.BlockSpec((1, D), lambda i, f: (0, 0)),
                  pl.BlockSpec((1, D), lambda i, f: (0, 0)),
                  pl.BlockSpec((D, tf), lambda i, f: (0, f)),
                  pl.BlockSpec((D, tf), lambda i, f: (0, nf + f)),
                  pl.BlockSpec((tf, D), lambda i, f: (f, 0))],
        out_specs=pl.BlockSpec((tm, D), lambda i, f: (i, 0)),
        out_shape=jax.ShapeDtypeStruct((M, D), F32),
        scratch_shapes=[pltpu.VMEM((tm, D), BF16), pltpu.VMEM((tm, D), F32)],
        compiler_params=_params("parallel", "arbitrary"),
        name="ffn",
    )(h, g_in.reshape(1, D), g_out.reshape(1, D), w_in, w_in, w_out)


def _kv_proj_kernel(x_ref, g_ref, *refs, emit_aux, seq):
    w_refs, out_refs = refs[:N_BRANCH], refs[N_BRANCH:]
    tm = x_ref.shape[0]
    xn = (_rms(x_ref[...]) * g_ref[...]).astype(BF16)
    for br in range(N_BRANCH):
        y = _dot(xn, w_refs[br][...])
        for j in range(ROW_SPLIT):
            out_refs[br][pl.ds(j, tm, stride=ROW_SPLIT), :] = y[:, j * HEAD_DIM:(j + 1) * HEAD_DIM]
        if emit_aux and br >= 1:
            kb_ref = out_refs[N_BRANCH + 2 * (br - 1)]
            vt_ref = out_refs[N_BRANCH + 2 * (br - 1) + 1]
            if br == 1:
                pos = lax.rem(pl.program_id(0) * tm, seq) + lax.broadcasted_iota(jnp.int32, (tm, 1), 0)
                col = lax.broadcasted_iota(jnp.int32, (1, HEAD_DIM), 1)
                onehot = (col == (pos >> int(math.log2(SLC_BLOCK)))).astype(BF16)
            for h in range(N_KV_HEADS):
                kh = y[:, h * HEAD_DIM:(h + 1) * HEAD_DIM].astype(BF16)
                kb_ref[h] = jnp.concatenate([kh, onehot], axis=1) if br == 1 else kh
                vcol = (N_KV_HEADS + h) * HEAD_DIM
                for r in range(tm // PAGE_ROWS):
                    v = y[r * PAGE_ROWS:(r + 1) * PAGE_ROWS, vcol:vcol + HEAD_DIM]
                    vt_ref[h, r] = v.T.astype(BF16)


def _kv_proj(x, g, weights, emit_aux, seq):
    M, D = x.shape
    tm = min(M, 256)
    assert M % tm == 0 and (not emit_aux or (tm % PAGE_ROWS == 0 and seq % tm == 0
                                             and -(-seq // SLC_BLOCK) <= HEAD_DIM))
    out_specs = [pl.BlockSpec((tm * ROW_SPLIT, HEAD_DIM), lambda i: (i, 0))] * N_BRANCH
    out_shape = [jax.ShapeDtypeStruct((M * ROW_SPLIT, HEAD_DIM), F32)] * N_BRANCH
    if emit_aux:
        tiles = tm // PAGE_ROWS
        for kw in (2 * HEAD_DIM, HEAD_DIM):
            out_specs += [pl.BlockSpec((N_KV_HEADS, tm, kw), lambda i: (0, i, 0)),
                          pl.BlockSpec((N_KV_HEADS, tiles, HEAD_DIM, PAGE_ROWS), lambda i: (0, i, 0, 0))]
            out_shape += [jax.ShapeDtypeStruct((N_KV_HEADS, M, kw), BF16),
                          jax.ShapeDtypeStruct((N_KV_HEADS, M // PAGE_ROWS, HEAD_DIM, PAGE_ROWS), BF16)]
    return pl.pallas_call(
        functools.partial(_kv_proj_kernel, emit_aux=emit_aux, seq=seq),
        grid=(M // tm,),
        in_specs=[pl.BlockSpec((tm, D), lambda i: (i, 0)), pl.BlockSpec((1, D), lambda i: (0, 0))]
        + [pl.BlockSpec(w.shape, lambda i: (0, 0)) for w in weights],
        out_specs=out_specs,
        out_shape=out_shape,
        compiler_params=_params("parallel"),
        name="kv_proj",
    )(x, g.reshape(1, D), *weights)


def _q_proj_kernel(x_ref, g_ref, wq_ref, wg_ref, q_ref, gate_ref, *, transposed):
    tm = x_ref.shape[0]
    xn = (_rms(x_ref[...]) * g_ref[...]).astype(BF16)
    q = _dot(xn, wq_ref[...]) * (HEAD_DIM ** -0.5 * (LOG2E if transposed else 1.0))
    gates = jax.nn.sigmoid(_dot(xn, wg_ref[...]))
    if not transposed:
        q_ref[...] = q.astype(BF16)
        gate_ref[...] = gates
        return
    for r in range(tm // PAGE_ROWS):
        rows = slice(r * PAGE_ROWS, (r + 1) * PAGE_ROWS)
        gt = gates[rows].T
        for hk in range(N_KV_HEADS):
            gate_ref[hk, r] = gt[hk * GATE_PAD:(hk + 1) * GATE_PAD]
            for g in range(GQA_GROUP):
                col = (hk * GQA_GROUP + g) * HEAD_DIM
                q_ref[hk, r, :, g * PAGE_ROWS:(g + 1) * PAGE_ROWS] = q[rows, col:col + HEAD_DIM].T.astype(BF16)


def _q_proj(x, g, wq, wg, transposed):
    M, D = x.shape
    tm = min(M, 256)
    assert M % tm == 0 and (tm % PAGE_ROWS == 0 or not transposed)
    tiles = tm // PAGE_ROWS
    gw = GQA_GROUP * PAGE_ROWS
    if transposed:
        out_specs = [pl.BlockSpec((N_KV_HEADS, tiles, HEAD_DIM, gw), lambda i: (0, i, 0, 0)),
                     pl.BlockSpec((N_KV_HEADS, tiles, GATE_PAD, PAGE_ROWS), lambda i: (0, i, 0, 0))]
        out_shape = [jax.ShapeDtypeStruct((N_KV_HEADS, M // PAGE_ROWS, HEAD_DIM, gw), BF16),
                     jax.ShapeDtypeStruct((N_KV_HEADS, M // PAGE_ROWS, GATE_PAD, PAGE_ROWS), F32)]
    else:
        out_specs = [pl.BlockSpec((tm, wq.shape[1]), lambda i: (i, 0)),
                     pl.BlockSpec((tm, wg.shape[1]), lambda i: (i, 0))]
        out_shape = [jax.ShapeDtypeStruct((M, wq.shape[1]), BF16), jax.ShapeDtypeStruct((M, wg.shape[1]), F32)]
    return pl.pallas_call(
        functools.partial(_q_proj_kernel, transposed=transposed),
        grid=(M // tm,),
        in_specs=[pl.BlockSpec((tm, D), lambda i: (i, 0)), pl.BlockSpec((1, D), lambda i: (0, 0)),
                  pl.BlockSpec(wq.shape, lambda i: (0, 0)), pl.BlockSpec(wg.shape, lambda i: (0, 0))],
        out_specs=out_specs,
        out_shape=out_shape,
        compiler_params=_params("parallel"),
        name="q_proj",
    )(x, g.reshape(1, D), wq, wg)


def _gelu_tanh(x):
    return 0.5 * x * (1.0 + jnp.tanh(math.sqrt(2.0 / math.pi) * (x + 0.044715 * (x * x * x))))


def _cmp_bias_kernel(pos_ref, w1_ref, b1_ref, o_ref):
    for c in range(2):
        acc = jnp.zeros((8, HEAD_DIM), F32)
        for j in range(CMP_BLOCK):
            pj = jnp.broadcast_to(pos_ref[c, j:j + 1, :], (8, HEAD_DIM)).astype(BF16)
            acc = acc + _dot(pj, w1_ref[c, j])
        o_ref[c] = acc + b1_ref[c]


def _cmp_bias(pos, w1, b1):
    return pl.pallas_call(
        _cmp_bias_kernel,
        out_shape=jax.ShapeDtypeStruct((2, 8, HEAD_DIM), F32),
        compiler_params=pltpu.CompilerParams(vmem_limit_bytes=VMEM_LIMIT_BYTES),
        name="cmp_bias",
    )(pos, w1, b1)


def _compress_kernel(pt_ref, *refs, n_pages):
    del pt_ref
    page_refs = refs[:n_pages]
    w1_ref, bias_ref, w2_ref, o_ref, scr_ref, sum_ref = refs[n_pages:]
    sub_per_page = PAGE_ROWS // CMP_STRIDE
    n_sub = n_pages * sub_per_page
    sub_rows = CMP_STRIDE * ROW_SPLIT
    pages_per_chunk = 4 if n_pages % 4 == 0 else 1
    chunk_rows = pages_per_chunk * sub_per_page * ROW_SPLIT
    is_k = (lax.broadcasted_iota(jnp.int32, (chunk_rows, 1), 0) & (ROW_SPLIT - 1)) < N_KV_HEADS
    bias = jnp.where(is_k[0:ROW_SPLIT], bias_ref[0], bias_ref[1])
    bias = jnp.concatenate([bias] * (chunk_rows // ROW_SPLIT), axis=0)

    def by_side(x):
        return jnp.concatenate([jnp.where(is_k, x, 0.0), jnp.where(is_k, 0.0, x)], axis=1).astype(BF16)

    for ck in range(n_pages // pages_per_chunk):
        chunk_pages = page_refs[ck * pages_per_chunk:(ck + 1) * pages_per_chunk]
        acc = jnp.zeros((chunk_rows, 2 * HEAD_DIM), F32)
        for j in range(CMP_STRIDE):
            x = jnp.concatenate([r[n * sub_rows + j * ROW_SPLIT:n * sub_rows + (j + 1) * ROW_SPLIT, :]
                                 for r in chunk_pages for n in range(sub_per_page)], axis=0)
            acc = acc + _dot(by_side(x), w1_ref[j])
        scr_ref[ck * chunk_rows:(ck + 1) * chunk_rows, :] = acc
    total = n_sub * ROW_SPLIT
    for ck in range(n_pages // pages_per_chunk):
        rows = slice(ck * chunk_rows, (ck + 1) * chunk_rows)
        first = scr_ref[rows, 0:HEAD_DIM]
        nxt_rows = slice(ck * chunk_rows + ROW_SPLIT, min((ck + 1) * chunk_rows + ROW_SPLIT, total))
        nxt = scr_ref[nxt_rows, HEAD_DIM:2 * HEAD_DIM]
        if nxt.shape[0] < chunk_rows:
            nxt = jnp.concatenate([nxt, jnp.zeros((chunk_rows - nxt.shape[0], HEAD_DIM), F32)], axis=0)
        hid = _gelu_tanh(first + nxt + bias)
        out = _dot(by_side(hid), w2_ref[...])
        if (ck + 1) * chunk_rows == total:
            row = lax.broadcasted_iota(jnp.int32, (chunk_rows, 1), 0)
            out = jnp.where(row < chunk_rows - ROW_SPLIT, out, 0.0)
        sum_ref[rows, :] = out
    for ch in range(ROW_SPLIT):
        o_ref[0, :, ch * HEAD_DIM:(ch + 1) * HEAD_DIM] = sum_ref[pl.ds(ch, n_sub, stride=ROW_SPLIT), :]


def _compress(pages, table, w1cat, bias, w2cat):
    nb, n_pages = table.shape
    n_sub = n_pages * (PAGE_ROWS // CMP_STRIDE)
    page_rows2d = PAGE_ROWS * ROW_SPLIT
    page_specs = [pl.BlockSpec((page_rows2d, HEAD_DIM), functools.partial(lambda b, pt, p: (pt[b, p], 0), p=p))
                  for p in range(n_pages)]
    const = lambda shape: pl.BlockSpec(shape, lambda b, pt: (0,) * len(shape))
    grid_spec = pltpu.PrefetchScalarGridSpec(
        num_scalar_prefetch=1,
        grid=(nb,),
        in_specs=page_specs + [const(w1cat.shape), const(bias.shape), const(w2cat.shape)],
        out_specs=pl.BlockSpec((1, n_sub, KV_COLS), lambda b, pt: (b, 0, 0)),
        scratch_shapes=[pltpu.VMEM((n_sub * ROW_SPLIT, 2 * HEAD_DIM), F32),
                        pltpu.VMEM((n_sub * ROW_SPLIT, HEAD_DIM), F32)],
    )
    return pl.pallas_call(
        functools.partial(_compress_kernel, n_pages=n_pages),
        grid_spec=grid_spec,
        out_shape=jax.ShapeDtypeStruct((nb, n_sub, KV_COLS), F32),
        compiler_params=_params("parallel"),
        name="compress",
    )(table, *([pages] * n_pages), w1cat, bias, w2cat)


def _rel_bucket_np(dist):
    n = np.maximum(dist, 0)
    max_exact = N_BUCKETS // 2
    nf = np.maximum(n, 1).astype(np.float32)
    large = max_exact + (np.log(nf / np.float32(max_exact)) / np.float32(math.log(MAX_DISTANCE / max_exact))
                         * np.float32(N_BUCKETS - max_exact)).astype(np.int32)
    large = np.minimum(large, N_BUCKETS - 1)
    return np.where(n < max_exact, n, large).astype(np.int32)


def _bias_table_kernel(rb_ref, bk_ref, o_ref, *, minus_far, scale):
    bk = bk_ref[...]
    for h in range(N_HEADS):
        v = jnp.full(bk.shape, rb_ref[0, h], F32)
        for b in range(1, N_BUCKETS):
            v = jnp.where(bk == b, rb_ref[b, h], v)
        if minus_far:
            v = v - rb_ref[N_BUCKETS - 1, h]
        o_ref[h] = v if scale == 1.0 else v * scale


def _bias_table(rel_bias, bucket, minus_far=False, scale=1.0):
    R, C = bucket.shape
    tr = min(R, 128)
    assert R % tr == 0
    return pl.pallas_call(
        functools.partial(_bias_table_kernel, minus_far=minus_far, scale=scale),
        grid=(R // tr,),
        in_specs=[pl.BlockSpec(memory_space=pltpu.SMEM), pl.BlockSpec((tr, C), lambda r: (r, 0))],
        out_specs=pl.BlockSpec((N_HEADS, tr, C), lambda r: (0, r, 0)),
        out_shape=jax.ShapeDtypeStruct((N_HEADS, R, C), F32),
        compiler_params=_params("parallel"),
        name="bias_table",
    )(rel_bias, jnp.asarray(bucket))


def _overlap_np(n_sub, n_slc):
    n_cmp = n_sub - 1
    cs = np.arange(n_cmp) * CMP_STRIDE
    ce = cs + CMP_BLOCK - 1
    ss = np.arange(n_slc) * SLC_BLOCK
    se = ss + SLC_BLOCK - 1
    ov = np.minimum(ce[:, None], se[None, :]) - np.maximum(cs[:, None], ss[None, :]) + 1
    out = np.zeros((n_sub, n_slc), np.float32)
    out[:n_cmp] = np.maximum(ov, 0).astype(np.float32) / CMP_BLOCK
    return out


def _pad_to(a, shape):
    return np.pad(a, [(0, s - d) for d, s in zip(a.shape, shape)])


def _score(imp, j, qb, n_slc):
    prio = 2.0 * (j == qb).astype(F32) + (j == qb - 1).astype(F32) + (j == 0).astype(F32)
    score = jnp.where(j <= qb, imp + SEL_BIG * prio, -SEL_BIG)
    return jnp.where(j < n_slc, score, -3e38)


def _attn_prompt_kernel(qt_ref, gt_ref, ck_ref, cv_ref, sk_ref, svt_ref, wk_ref, wvt_ref,
                        toe_ref, cmpb_ref, ovt_ref, o_ref,
                        m_ref, l_ref, acc_ref, accw_ref, *, n_slc, n_sub):
    tq = PAGE_ROWS
    i = pl.program_id(2)
    qt = qt_ref[0, 0]
    t_idx = lax.broadcasted_iota(jnp.int32, (1, tq), 1)
    qpos = i * tq + t_idx
    lane4 = lambda x: jnp.concatenate([x] * GQA_GROUP, axis=1)

    kc = ck_ref[0].astype(BF16)
    vct = cv_ref[0].T.astype(BF16)
    n_idx = lax.broadcasted_iota(jnp.int32, (n_sub, 1), 0)
    valid = lane4((n_idx * CMP_STRIDE + (CMP_BLOCK - 1) <= qpos) & (n_idx < n_sub - 1))
    bias_c = jnp.concatenate([cmpb_ref[g, 0] for g in range(GQA_GROUP)], axis=1)
    logits = jnp.where(valid, _dot(kc, qt) + bias_c, NEG_INF)
    p = jnp.exp2(logits - jnp.max(logits, axis=0, keepdims=True))
    p = p * (1.0 / jnp.sum(p, axis=0, keepdims=True))
    p = jnp.where(valid, p, 0.0)
    o_c = _dot(vct, p.astype(BF16))
    psum = p[:, 0:tq]
    for g in range(1, GQA_GROUP):
        psum = psum + p[:, g * tq:(g + 1) * tq]
    imp = jnp.dot(ovt_ref[...], psum, precision=lax.Precision.HIGHEST, preferred_element_type=F32)
    nsp = imp.shape[0]
    j = lax.broadcasted_iota(jnp.int32, (nsp, 1), 0)
    score = _score(imp, j, qpos >> int(math.log2(SLC_BLOCK)), n_slc)
    rank = jnp.zeros((nsp, tq), jnp.int32)
    for jj in range(n_slc):
        row = score[jj:jj + 1, :]
        rank = rank + ((row > score) | ((row == score) & (jj < j))).astype(jnp.int32)
    madd = jnp.where(rank < min(N_SELECT, n_slc), 0.0, NEG_INF)
    qt_sel = jnp.concatenate([qt, lane4(madd).astype(BF16),
                              jnp.zeros((HEAD_DIM - nsp, GQA_GROUP * tq), BF16)], axis=0)

    k_row = lax.broadcasted_iota(jnp.int32, (tq, 1), 0)

    def chunk(k, kinds, k_ref, vt_ref, q_op, out_ref):
        nt = len(kinds)
        k0 = pl.multiple_of(k * tq, tq)
        s = _dot(k_ref[0, pl.ds(k0, nt * tq), :], q_op)
        parts = []
        for jx, kind in enumerate(kinds):
            sj = s[jx * tq:(jx + 1) * tq]
            if kind in ("near", "diag"):
                d = 0 if kind == "diag" else 1
                sj = sj + jnp.concatenate([toe_ref[g * 2 + d] for g in range(GQA_GROUP)], axis=1)
            if kind == "diag":
                sj = jnp.where(lane4(k_row <= t_idx), sj, NEG_INF)
            elif kind == "edge":
                sj = jnp.where(lane4(k_row > t_idx), sj, NEG_INF)
            parts.append(sj)
        m_new = functools.reduce(jnp.maximum, parts)
        m_new = jnp.max(m_new, axis=0, keepdims=True)
        if out_ref is None:
            m_old = m_ref[...]
            m_new = jnp.maximum(m_old, m_new)
            alpha = jnp.exp2(m_old - m_new)
        ps = [jnp.exp2(sj - m_new) for sj in parts]
        lsum = jnp.sum(functools.reduce(jnp.add, ps), axis=0, keepdims=True)
        vt = jnp.concatenate([vt_ref[0, k + jx] for jx in range(nt)], axis=1)
        pv = _dot(vt, jnp.concatenate(ps, axis=0).astype(BF16))
        if out_ref is None:
            l_ref[...] = alpha * l_ref[...] + lsum
            acc_ref[...] = alpha * acc_ref[...] + pv
            m_ref[...] = m_new
        else:
            out_ref[...] = pv * (1.0 / lsum)

    def tail_kinds(n_far):
        return ("far",) * n_far + ("near", "diag")

    far_step = SEL_FAR_TILES
    wt = WINDOW // tq
    m_ref[...] = jnp.full(m_ref.shape, NEG_INF, F32)
    l_ref[...] = jnp.zeros(l_ref.shape, F32)
    acc_ref[...] = jnp.zeros(acc_ref.shape, F32)

    n_far = jnp.maximum(i - 1, 0) // far_step

    def far_body(c, carry):
        chunk(c * far_step, ("far",) * far_step, sk_ref, svt_ref, qt_sel, None)
        return carry

    lax.fori_loop(0, n_far, far_body, 0)

    for c in range(wt):
        @pl.when(i == c)
        def _(c=c):
            kinds = tail_kinds(c - 1) if c >= 1 else ("diag",)
            chunk(0, kinds, sk_ref, svt_ref, qt_sel, None)
            chunk(0, kinds, wk_ref, wvt_ref, qt, accw_ref)

    for r in range(far_step):
        @pl.when((i >= wt) & (lax.rem(i + far_step - 1, far_step) == r))
        def _(r=r):
            chunk(i - 1 - r, tail_kinds(r), sk_ref, svt_ref, qt_sel, None)
            chunk(i - wt, ("edge",) + tail_kinds(wt - 2), wk_ref, wvt_ref, qt, accw_ref)

    o_s = acc_ref[...] * (1.0 / l_ref[...])
    o_w = accw_ref[...]

    gt = gt_ref[0, 0]
    for g in range(GQA_GROUP):
        lanes = slice(g * tq, (g + 1) * tq)
        o = (gt[g:g + 1] * o_c[:, lanes] + gt[GQA_GROUP + g:GQA_GROUP + g + 1] * o_s[:, lanes]
             + gt[2 * GQA_GROUP + g:2 * GQA_GROUP + g + 1] * o_w[:, lanes])
        o_ref[0, :, g * HEAD_DIM:(g + 1) * HEAD_DIM] = o.T.astype(o_ref.dtype)


def _attn_prompt(qt, gt, ckv, sk, svt, wk, wvt, toe, cmpb, B, S):
    tq = PAGE_ROWS
    nq = S // tq
    n_sub = ckv.shape[1]
    n_slc = -(-S // SLC_BLOCK)
    nsp = -(-n_slc // 16) * 16
    assert S % tq == 0 and nsp <= HEAD_DIM and nq > WINDOW // tq
    ovt = jnp.asarray(_pad_to(_overlap_np(n_sub, n_slc).T, (nsp, n_sub)))
    gw = GQA_GROUP * tq
    return pl.pallas_call(
        functools.partial(_attn_prompt_kernel, n_slc=n_slc, n_sub=n_sub),
        grid=(B, N_KV_HEADS, nq),
        in_specs=[pl.BlockSpec((1, 1, HEAD_DIM, gw), lambda b, h, i: (h, b * nq + i, 0, 0)),
                  pl.BlockSpec((1, 1, GATE_PAD, tq), lambda b, h, i: (h, b * nq + i, 0, 0)),
                  pl.BlockSpec((1, n_sub, HEAD_DIM), lambda b, h, i: (b, 0, h)),
                  pl.BlockSpec((1, n_sub, HEAD_DIM), lambda b, h, i: (b, 0, N_KV_HEADS + h)),
                  pl.BlockSpec((1, S, 2 * HEAD_DIM), lambda b, h, i: (h, b, 0)),
                  pl.BlockSpec((1, nq, HEAD_DIM, tq), lambda b, h, i: (h, b, 0, 0)),
                  pl.BlockSpec((1, S, HEAD_DIM), lambda b, h, i: (h, b, 0)),
                  pl.BlockSpec((1, nq, HEAD_DIM, tq), lambda b, h, i: (h, b, 0, 0)),
                  pl.BlockSpec((GQA_GROUP * 2, tq, tq), lambda b, h, i: (h, 0, 0)),
                  pl.BlockSpec((GQA_GROUP, 1, n_sub, tq), lambda b, h, i: (h, i, 0, 0)),
                  pl.BlockSpec(ovt.shape, lambda b, h, i: (0, 0))],
        out_specs=pl.BlockSpec((1, tq, GQA_GROUP * HEAD_DIM), lambda b, h, i: (b, i, h)),
        out_shape=jax.ShapeDtypeStruct((B, S, N_HEADS * HEAD_DIM), BF16),
        scratch_shapes=[pltpu.VMEM((1, gw), F32), pltpu.VMEM((1, gw), F32),
                        pltpu.VMEM((HEAD_DIM, gw), F32), pltpu.VMEM((HEAD_DIM, gw), F32)],
        compiler_params=_params("parallel", "parallel", "arbitrary"),
        name="attn_prompt",
    )(qt, gt, ckv, ckv, sk, svt, wk, wvt, toe, cmpb, ovt)


def _attn_sample_kernel(pt_ref, *refs, n_pages, n_slc, n_sub, past_len, win_buf, ds):
    del pt_ref
    page_refs = refs[:n_pages]
    (q_ref, gate_ref, ckv_ref, snew_ref, wst_ref, wnew_ref, bias_ref, ov_ref, e_ref,
     o_ref, wout_ref) = refs[n_pages:]
    nk_s = (n_pages + 1) * PAGE_ROWS
    nk_w = win_buf + PAGE_ROWS
    t_idx = lax.broadcasted_iota(jnp.int32, (ds, 1), 0)
    qpos = past_len + t_idx
    pad = jnp.zeros((PAGE_ROWS - ds, HEAD_DIM), F32)
    gates = gate_ref[0]
    row4 = lambda x: jnp.concatenate([x] * GQA_GROUP, axis=0)
    head_rows = lambda r, col, n: r[pl.ds(col, n, stride=ROW_SPLIT), :]

    shift = ds * ROW_SPLIT
    wout_ref[0:win_buf * ROW_SPLIT - shift, :] = wst_ref[shift:win_buf * ROW_SPLIT, :]
    wout_ref[win_buf * ROW_SPLIT - shift:win_buf * ROW_SPLIT, :] = wnew_ref[...]

    ks_pos = lax.broadcasted_iota(jnp.int32, (1, nk_s), 1)
    dist_w = win_buf + t_idx - lax.broadcasted_iota(jnp.int32, (1, nk_w), 1)
    ok_w4 = row4((dist_w >= 0) & (dist_w < WINDOW))
    n_idx = lax.broadcasted_iota(jnp.int32, (1, n_sub), 1)
    ok_c4 = row4((n_idx * CMP_STRIDE + (CMP_BLOCK - 1) <= qpos) & (n_idx < n_sub - 1))
    lanes = ov_ref.shape[1]
    j = lax.broadcasted_iota(jnp.int32, (1, lanes), 1)
    qb = qpos >> int(math.log2(SLC_BLOCK))

    def softmax(logits, ok4):
        logits = jnp.where(ok4, logits, NEG_INF)
        p = jnp.exp(logits - jnp.max(logits, axis=-1, keepdims=True))
        return p / jnp.sum(p, axis=-1, keepdims=True)

    for hk in range(N_KV_HEADS):
        kcol, vcol = hk, N_KV_HEADS + hk
        qcol = hk * GQA_GROUP * HEAD_DIM
        q4 = jnp.concatenate([q_ref[0, :, qcol + g * HEAD_DIM:qcol + (g + 1) * HEAD_DIM]
                              for g in range(GQA_GROUP)], axis=0)
        hb = bias_ref[hk * GQA_GROUP:(hk + 1) * GQA_GROUP].reshape(GQA_GROUP * ds, bias_ref.shape[2])

        kc = ckv_ref[0, :, kcol * HEAD_DIM:(kcol + 1) * HEAD_DIM].astype(BF16)
        vc = ckv_ref[0, :, vcol * HEAD_DIM:(vcol + 1) * HEAD_DIM].astype(BF16)
        p = jnp.where(ok_c4, softmax(_dot_nt(q4, kc) + hb[:, nk_s + nk_w:nk_s + nk_w + n_sub], ok_c4), 0.0)
        o_c = _dot(p.astype(BF16), vc)
        psum = p[0:ds]
        for g in range(1, GQA_GROUP):
            psum = psum + p[g * ds:(g + 1) * ds]
        imp = jnp.dot(psum, ov_ref[...], precision=lax.Precision.HIGHEST, preferred_element_type=F32)
        score = _score(imp, j, qb, n_slc)
        rank = jnp.zeros((ds, lanes), jnp.int32)
        for jj in range(n_slc):
            col = score[:, jj:jj + 1]
            rank = rank + ((col > score) | ((col == score) & (jj < j))).astype(jnp.int32)
        sel = ((rank < min(N_SELECT, n_slc)) & (j < n_slc)).astype(BF16)
        ok_s4 = row4((_dot(sel, e_ref[...]) > 0.5) & (ks_pos <= qpos))

        k_s = jnp.concatenate([head_rows(r, kcol, PAGE_ROWS) for r in page_refs]
                              + [head_rows(snew_ref, kcol, ds), pad], axis=0).astype(BF16)
        v_s = jnp.concatenate([head_rows(r, vcol, PAGE_ROWS) for r in page_refs]
                              + [head_rows(snew_ref, vcol, ds), pad], axis=0).astype(BF16)
        o_s = _dot(softmax(_dot_nt(q4, k_s) + hb[:, 0:nk_s], ok_s4).astype(BF16), v_s)

        k_w = jnp.concatenate([head_rows(wst_ref, kcol, win_buf), head_rows(wnew_ref, kcol, ds), pad],
                              axis=0).astype(BF16)
        v_w = jnp.concatenate([head_rows(wst_ref, vcol, win_buf), head_rows(wnew_ref, vcol, ds), pad],
                              axis=0).astype(BF16)
        o_w = _dot(softmax(_dot_nt(q4, k_w) + hb[:, nk_s:nk_s + nk_w], ok_w4).astype(BF16), v_w)

        gcol = hk * GATE_PAD
        for g in range(GQA_GROUP):
            rows = slice(g * ds, (g + 1) * ds)
            gate = lambda br, g=g: gates[:, gcol + br * GQA_GROUP + g:gcol + br * GQA_GROUP + g + 1]
            o = gate(0) * o_c[rows] + gate(1) * o_s[rows] + gate(2) * o_w[rows]
            o_ref[0, :, qcol + g * HEAD_DIM:qcol + (g + 1) * HEAD_DIM] = o.astype(o_ref.dtype)


def _attn_sample(q, gates, ckv, cache_slc, table, slc_new, win_state, win_new, bias, past_len, win_buf):
    DB, ds, _ = q.shape
    n_pages = table.shape[1]
    n_sub = ckv.shape[1]
    n_slc = -(-(past_len + ds) // SLC_BLOCK)
    lanes = PAGE_ROWS
    assert ds % 8 == 0 and ds <= PAGE_ROWS and n_slc <= lanes and n_sub <= lanes
    assert win_buf <= past_len and win_buf > ds
    nk_s = (n_pages + 1) * PAGE_ROWS
    ov = jnp.asarray(_pad_to(_overlap_np(n_sub, n_slc), (n_sub, lanes)))
    e = (np.arange(nk_s)[None, :] // SLC_BLOCK == np.arange(lanes)[:, None]) & (np.arange(lanes)[:, None] < n_slc)
    e = jnp.asarray(e.astype(np.float32), dtype=BF16)
    page_specs = [pl.BlockSpec((PAGE_ROWS * ROW_SPLIT, HEAD_DIM),
                               functools.partial(lambda b, pt, p: (pt[b, p], 0), p=p)) for p in range(n_pages)]
    per_b3 = lambda shape: pl.BlockSpec((1,) + shape[1:], lambda b, pt: (b, 0, 0))
    rows_b = lambda n: pl.BlockSpec((n * ROW_SPLIT, HEAD_DIM), lambda b, pt: (b, 0))
    const = lambda shape: pl.BlockSpec(shape, lambda b, pt: (0,) * len(shape))
    grid_spec = pltpu.PrefetchScalarGridSpec(
        num_scalar_prefetch=1,
        grid=(DB,),
        in_specs=page_specs + [per_b3(q.shape), per_b3(gates.shape), per_b3(ckv.shape), rows_b(ds),
                               rows_b(win_buf), rows_b(ds), const(bias.shape), const(ov.shape), const(e.shape)],
        out_specs=[per_b3(q.shape), rows_b(win_buf)],
    )
    return pl.pallas_call(
        functools.partial(_attn_sample_kernel, n_pages=n_pages, n_slc=n_slc, n_sub=n_sub,
                          past_len=past_len, win_buf=win_buf, ds=ds),
        grid_spec=grid_spec,
        out_shape=[jax.ShapeDtypeStruct(q.shape, BF16), jax.ShapeDtypeStruct(win_state.shape, F32)],
        compiler_params=_params("parallel"),
        name="attn_sample",
    )(table, *([cache_slc] * n_pages), q, gates, ckv, slc_new, win_state, win_new, bias, ov, e)


def _prompt_buckets(S, n_sub):
    tq = PAGE_ROWS
    k = np.arange(tq)[:, None]
    t = np.arange(tq)[None, :]
    toe = np.concatenate([_rel_bucket_np(t - k + d * tq) for d in range(2)], axis=0)
    kpos_c = np.arange(n_sub)[:, None] * CMP_STRIDE + CMP_BLOCK - 1
    cmp = np.concatenate([_rel_bucket_np(i * tq + t - kpos_c) for i in range(S // tq)], axis=0)
    return toe, cmp


def _sample_buckets(ds, past_len, n_pages, win_buf, n_sub):
    qpos = past_len + np.arange(ds)[:, None]
    ks = np.arange((n_pages + 1) * PAGE_ROWS)[None, :]
    kw = past_len - win_buf + np.arange(win_buf + PAGE_ROWS)[None, :]
    kc = np.arange(n_sub)[None, :] * CMP_STRIDE + CMP_BLOCK - 1
    return np.concatenate([_rel_bucket_np(qpos - ks), _rel_bucket_np(qpos - kw), _rel_bucket_np(qpos - kc)], axis=1)


def kernel(x_prompt, x_sample, state_pool, cache_cmp_kv, cache_slc_kv, state_win_kv, page_table, norms, w_pool,
           pool_scale, kv_norm, w_kv, cmp_pos, cmp_w1, cmp_b1, cmp_w2, w_q, w_o, rel_bias, w_ffn_in, w_ffn_out):
    B, S, D = x_prompt.shape
    DB, DS, _ = x_sample.shape
    n_pages = page_table.shape[1]
    past_len = n_pages * cache_slc_kv.shape[1]
    win_buf = state_win_kv.shape[1]
    assert norms.shape[0] == 2 and w_pool.shape[0] == 1 and w_q.shape[0] == 1, "one pooling + one attention layer"
    assert cache_slc_kv.shape[1] == PAGE_ROWS and S % PAGE_ROWS == 0
    assert past_len % CMP_STRIDE == 0 and DS < CMP_STRIDE and D == N_HEADS * HEAD_DIM
    assert S >= win_buf and N_BRANCH * GQA_GROUP <= GATE_PAD

    qdim = N_HEADS * HEAD_DIM
    w_pool_b = w_pool[0].astype(BF16)
    w_in_b = w_ffn_in.astype(BF16)
    w_out_b = w_ffn_out.astype(BF16)
    w_kv_b = [w_kv[:, k * KV_COLS:(k + 1) * KV_COLS].astype(BF16) for k in range(N_BRANCH)]
    w_qq = w_q[0, :, :qdim].astype(BF16)
    w_qg = w_q[0, :, qdim:].reshape(D, N_BRANCH, N_KV_HEADS, GQA_GROUP).transpose(0, 2, 1, 3)
    w_qg = jnp.pad(w_qg.reshape(D, N_KV_HEADS, N_BRANCH * GQA_GROUP),
                   ((0, 0), (0, 0), (0, GATE_PAD - N_BRANCH * GQA_GROUP)))
    w_qg = jnp.pad(w_qg.reshape(D, N_KV_HEADS * GATE_PAD), ((0, 0), (0, 128 - N_KV_HEADS * GATE_PAD))).astype(BF16)
    w_o_b = w_o.astype(BF16)
    w1_b = cmp_w1.astype(BF16)
    w1cat = w1_b.reshape(2, 2, CMP_STRIDE, HEAD_DIM, HEAD_DIM).transpose(2, 0, 3, 1, 4)
    w1cat = w1cat.reshape(CMP_STRIDE, 2 * HEAD_DIM, 2 * HEAD_DIM)
    w2cat = cmp_w2.astype(BF16).reshape(2 * HEAD_DIM, HEAD_DIM)
    cmp_bias = _cmp_bias(cmp_pos, w1_b, cmp_b1.reshape(2, 1, HEAD_DIM))

    def layer0(x, buf, pos0):
        Bx, Sx, _ = x.shape
        M = Bx * Sx
        mix, st = _pool_mix(x, buf, norms[0, 0], pos0)
        h = _proj_res(mix.reshape(M, D), w_pool_b, pool_scale[0], norms[0, 1], x.reshape(M, D), "pool_proj")
        return _ffn(h, norms[0, 2], norms[0, 3], w_in_b[0], w_out_b[0]), st

    def layer1_tail(h1, o):
        h2 = _proj_res(o, w_o_b, None, norms[1, 1], h1, "o_proj")
        return _ffn(h2, norms[1, 2], norms[1, 3], w_in_b[1], w_out_b[1])

    rows2d = lambda a: a.reshape(-1, HEAD_DIM)
    kv5 = lambda r, b, s: r.reshape(b, s, 2, N_KV_HEADS, HEAD_DIM)

    h1, st_p = layer0(x_prompt, jnp.zeros((B, POOL_BUF, D), F32), 0)
    cmp_p, slc_p, win_p, sk, svt, wk, wvt = _kv_proj(h1, kv_norm, w_kv_b, True, S)
    qt, gt = _q_proj(h1, norms[1, 0], w_qq, w_qg, True)
    pages_p = S // PAGE_ROWS
    ckv_p = _compress(cmp_p, jnp.arange(B * pages_p, dtype=jnp.int32).reshape(B, pages_p), w1cat, cmp_bias, w2cat)
    n_sub_p = ckv_p.shape[1]
    toe_bk, cmp_bk = _prompt_buckets(S, n_sub_p)
    toe = _bias_table(rel_bias, toe_bk, True, LOG2E).reshape(N_HEADS * 2, PAGE_ROWS, PAGE_ROWS)
    cmpb = _bias_table(rel_bias, cmp_bk, False, LOG2E).reshape(N_HEADS, pages_p, n_sub_p, PAGE_ROWS)
    o_p = _attn_prompt(qt, gt, ckv_p, sk, svt, wk, wvt, toe, cmpb, B, S)
    y_prompt = layer1_tail(h1, o_p.reshape(B * S, qdim)).reshape(B, S, D)
    win_state_p = win_p.reshape(B, S * ROW_SPLIT, HEAD_DIM)[:, (S - win_buf) * ROW_SPLIT:]

    h1s, st_s = layer0(x_sample, state_pool[0], past_len)
    cmp_s, slc_s, win_s = _kv_proj(h1s, kv_norm, w_kv_b, False, DS)
    q_s, gates_s = _q_proj(h1s, norms[1, 0], w_qq, w_qg, False)
    ckv_s = _compress(rows2d(cache_cmp_kv), page_table, w1cat, cmp_bias, w2cat)
    bias_s = _bias_table(rel_bias, _sample_buckets(DS, past_len, n_pages, win_buf, ckv_s.shape[1]))
    o_s, win_state_s = _attn_sample(
        q_s.reshape(DB, DS, qdim), gates_s.reshape(DB, DS, 128), ckv_s, rows2d(cache_slc_kv), page_table,
        slc_s, rows2d(state_win_kv), win_s, bias_s, past_len, win_buf)
    y_sample = layer1_tail(h1s, o_s.reshape(DB * DS, qdim)).reshape(DB, DS, D)

    return (y_prompt, y_sample, st_p[None], st_s[None],
            kv5(cmp_p, B, S), kv5(cmp_s, DB, DS), kv5(slc_p, B, S), kv5(slc_s, DB, DS),
            kv5(win_state_p, B, win_buf), kv5(win_state_s, DB, win_buf))
```

```python
import functools
import math

import numpy as np
import jax
import jax.numpy as jnp
from jax import lax
from jax.experimental import pallas as pl
from jax.experimental.pallas import tpu as pltpu

F32 = jnp.float32
BF16 = jnp.bfloat16

HEAD_DIM = 128
N_KV_HEADS = 4
GQA_GROUP = 4
N_HEADS = N_KV_HEADS * GQA_GROUP
N_BRANCH = 3
POOL_WINDOWS = (2, 4, 8, 16)
POOL_BUF = max(POOL_WINDOWS) - 1
POOL_HALO = POOL_BUF + 1
CMP_BLOCK = 32
CMP_STRIDE = 16
SLC_BLOCK = 64
N_SELECT = 16
WINDOW = 512
N_BUCKETS = 32
MAX_DISTANCE = 128
RMS_EPS = 1e-6
SEL_BIG = 1e3
NEG_INF = -1e30
LOG2E = math.log2(math.e)
PAGE_ROWS = 128
KV_COLS = 2 * N_KV_HEADS * HEAD_DIM
ROW_SPLIT = 2 * N_KV_HEADS
GATE_PAD = 16
SEL_FAR_TILES = 4
PROMPT_HEADS_PER_STEP = 4

VMEM_LIMIT_BYTES = 56 * 1024 * 1024

_NT = (((1,), (1,)), ((), ()))


def _params(*sem):
    return pltpu.CompilerParams(dimension_semantics=sem, vmem_limit_bytes=VMEM_LIMIT_BYTES)


def _rms(x):
    return x * lax.rsqrt(jnp.mean(x * x, axis=-1, keepdims=True) + RMS_EPS)


def _dot(a, b):
    return jnp.dot(a, b, preferred_element_type=F32)


def _dot_nt(a, b):
    return lax.dot_general(a, b, _NT, preferred_element_type=F32)


def _pool_mix_kernel(h_ref, buf_ref, g_ref, mix_ref, st_ref, ext_ref, *, ts, pos0, group_dim):
    i = pl.program_id(1)

    @pl.when(i == 0)
    def _():
        ext_ref[0:POOL_HALO, :] = buf_ref[0]

    @pl.when(i > 0)
    def _():
        ext_ref[0:POOL_HALO, :] = ext_ref[ts:ts + POOL_HALO, :]

    x = h_ref[0]
    ext_ref[POOL_HALO:POOL_HALO + ts, :] = _rms(x) * g_ref[...]
    pos = pos0 + i * ts + lax.broadcasted_iota(jnp.int32, (ts, 1), 0)
    for gi, w in enumerate(POOL_WINDOWS):
        cols = slice(gi * group_dim, (gi + 1) * group_dim)
        cur = ext_ref[POOL_HALO:POOL_HALO + ts, cols]
        s = cur
        for u in range(1, w):
            s = s + ext_ref[POOL_HALO - u:POOL_HALO - u + ts, cols]
        cnt = jnp.minimum(w, pos + 1).astype(F32)
        mix_ref[0, :, cols] = (s / cnt - cur).astype(mix_ref.dtype)
    st_ref[0] = ext_ref[ts:ts + POOL_HALO, :]


def _pool_mix(h, buf, g, pos0):
    B, S, D = h.shape
    ts = min(S, 256)
    assert S % ts == 0 and (ts >= POOL_HALO or S == ts) and ts % 8 == 0
    buf16 = jnp.pad(buf, ((0, 0), (POOL_HALO - POOL_BUF, 0), (0, 0)))
    mix, st = pl.pallas_call(
        functools.partial(_pool_mix_kernel, ts=ts, pos0=pos0, group_dim=D // len(POOL_WINDOWS)),
        grid=(B, S // ts),
        in_specs=[pl.BlockSpec((1, ts, D), lambda b, i: (b, i, 0)),
                  pl.BlockSpec((1, POOL_HALO, D), lambda b, i: (b, 0, 0)),
                  pl.BlockSpec((1, D), lambda b, i: (0, 0))],
        out_specs=[pl.BlockSpec((1, ts, D), lambda b, i: (b, i, 0)),
                   pl.BlockSpec((1, POOL_HALO, D), lambda b, i: (b, 0, 0))],
        out_shape=[jax.ShapeDtypeStruct((B, S, D), BF16),
                   jax.ShapeDtypeStruct((B, POOL_HALO, D), F32)],
        scratch_shapes=[pltpu.VMEM((POOL_HALO + ts, D), F32)],
        compiler_params=_params("parallel", "arbitrary"),
        name="pool_mix",
    )(h, buf16, g.reshape(1, D))
    return mix, st[:, POOL_HALO - POOL_BUF:]


def _proj_res_kernel(*refs, groups, has_scale):
    if has_scale:
        a_ref, w_ref, scale_ref, g_ref, res_ref, o_ref = refs
    else:
        a_ref, w_ref, g_ref, res_ref, o_ref = refs
    gd = a_ref.shape[1] // groups
    parts = [_dot(a_ref[:, gi * gd:(gi + 1) * gd], w_ref[gi]) for gi in range(groups)]
    y = parts[0] if groups == 1 else jnp.concatenate(parts, axis=-1)
    if has_scale:
        y = y * scale_ref[...]
    o_ref[...] = res_ref[...] + _rms(y) * g_ref[...]


def _proj_res(a, w, scale, g, res, name):
    M, D = a.shape
    groups = w.shape[0]
    tm = min(M, 256)
    assert M % tm == 0
    row = lambda i: (i, 0)
    fixed = lambda i: (0, 0)
    in_specs = [pl.BlockSpec((tm, D), row), pl.BlockSpec(w.shape, lambda i: (0, 0, 0))]
    args = [a, w]
    if scale is not None:
        in_specs.append(pl.BlockSpec((1, D), fixed))
        args.append(scale.reshape(1, D))
    in_specs += [pl.BlockSpec((1, D), fixed), pl.BlockSpec((tm, D), row)]
    args += [g.reshape(1, D), res]
    return pl.pallas_call(
        functools.partial(_proj_res_kernel, groups=groups, has_scale=scale is not None),
        grid=(M // tm,),
        in_specs=in_specs,
        out_specs=pl.BlockSpec((tm, D), row),
        out_shape=jax.ShapeDtypeStruct((M, D), F32),
        compiler_params=_params("parallel"),
        name=name,
    )(*args)


def _ffn_kernel(h_ref, gi_ref, go_ref, wg_ref, wu_ref, wo_ref, out_ref, xn_ref, acc_ref):
    f = pl.program_id(1)

    @pl.when(f == 0)
    def _():
        xn_ref[...] = (_rms(h_ref[...]) * gi_ref[...]).astype(xn_ref.dtype)
        acc_ref[...] = jnp.zeros_like(acc_ref)

    xn = xn_ref[...]
    gate = _dot(xn, wg_ref[...])
    up = _dot(xn, wu_ref[...])
    act = (gate * jax.nn.sigmoid(gate) * up).astype(BF16)
    acc_ref[...] += _dot(act, wo_ref[...])

    @pl.when(f == pl.num_programs(1) - 1)
    def _():
        out_ref[...] = h_ref[...] + _rms(acc_ref[...]) * go_ref[...]


def _ffn(h, g_in, g_out, w_in, w_out):
    M, D = h.shape
    F = w_out.shape[0]
    tm = min(M, 512)
    tf = 512
    assert M % tm == 0 and F % tf == 0
    nf = F // tf
    return pl.pallas_call(
        _ffn_kernel,
        grid=(M // tm, nf),
        in_specs=[pl.BlockSpec((tm, D), lambda i, f: (i, 0)),
                  pl.BlockSpec((1, D), lambda i, f: (0, 0)),
                  pl.BlockSpec((1, D), lambda i, f: (0, 0)),
                  pl.BlockSpec((D, tf), lambda i, f: (0, f)),
                  pl.BlockSpec((D, tf), lambda i, f: (0, nf + f)),
                  pl.BlockSpec((tf, D), lambda i, f: (f, 0))],
        out_specs=pl.BlockSpec((tm, D), lambda i, f: (i, 0)),
        out_shape=jax.ShapeDtypeStruct((M, D), F32),
        scratch_shapes=[pltpu.VMEM((tm, D), BF16), pltpu.VMEM((tm, D), F32)],
        compiler_params=_params("parallel", "arbitrary"),
        name="ffn",
    )(h, g_in.reshape(1, D), g_out.reshape(1, D), w_in, w_in, w_out)


def _kv_proj_kernel(x_ref, g_ref, *refs, emit_aux, seq):
    w_refs, out_refs = refs[:N_BRANCH], refs[N_BRANCH:]
    tm = x_ref.shape[0]
    xn = (_rms(x_ref[...]) * g_ref[...]).astype(BF16)
    for br in range(N_BRANCH):
        y = _dot(xn, w_refs[br][...])
        for j in range(ROW_SPLIT):
            out_refs[br][pl.ds(j, tm, stride=ROW_SPLIT), :] = y[:, j * HEAD_DIM:(j + 1) * HEAD_DIM]
        if emit_aux and br >= 1:
            kb_ref = out_refs[N_BRANCH + 2 * (br - 1)]
            vt_ref = out_refs[N_BRANCH + 2 * (br - 1) + 1]
            if br == 1:
                pos = lax.rem(pl.program_id(0) * tm, seq) + lax.broadcasted_iota(jnp.int32, (tm, 1), 0)
                col = lax.broadcasted_iota(jnp.int32, (1, HEAD_DIM), 1)
                onehot = (col == (pos >> int(math.log2(SLC_BLOCK)))).astype(BF16)
            for h in range(N_KV_HEADS):
                kh = y[:, h * HEAD_DIM:(h + 1) * HEAD_DIM].astype(BF16)
                kb_ref[h] = jnp.concatenate([kh, onehot], axis=1) if br == 1 else kh
                vcol = (N_KV_HEADS + h) * HEAD_DIM
                for r in range(tm // PAGE_ROWS):
                    v = y[r * PAGE_ROWS:(r + 1) * PAGE_ROWS, vcol:vcol + HEAD_DIM]
                    vt_ref[h, r] = v.T.astype(BF16)


def _kv_proj(x, g, weights, emit_aux, seq):
    M, D = x.shape
    tm = min(M, 256)
    assert M % tm == 0 and (not emit_aux or (tm % PAGE_ROWS == 0 and seq % tm == 0
                                             and -(-seq // SLC_BLOCK) <= HEAD_DIM))
    out_specs = [pl.BlockSpec((tm * ROW_SPLIT, HEAD_DIM), lambda i: (i, 0))] * N_BRANCH
    out_shape = [jax.ShapeDtypeStruct((M * ROW_SPLIT, HEAD_DIM), F32)] * N_BRANCH
    if emit_aux:
        tiles = tm // PAGE_ROWS
        for kw in (2 * HEAD_DIM, HEAD_DIM):
            out_specs += [pl.BlockSpec((N_KV_HEADS, tm, kw), lambda i: (0, i, 0)),
                          pl.BlockSpec((N_KV_HEADS, tiles, HEAD_DIM, PAGE_ROWS), lambda i: (0, i, 0, 0))]
            out_shape += [jax.ShapeDtypeStruct((N_KV_HEADS, M, kw), BF16),
                          jax.ShapeDtypeStruct((N_KV_HEADS, M // PAGE_ROWS, HEAD_DIM, PAGE_ROWS), BF16)]
    return pl.pallas_call(
        functools.partial(_kv_proj_kernel, emit_aux=emit_aux, seq=seq),
        grid=(M // tm,),
        in_specs=[pl.BlockSpec((tm, D), lambda i: (i, 0)), pl.BlockSpec((1, D), lambda i: (0, 0))]
        + [pl.BlockSpec(w.shape, lambda i: (0, 0)) for w in weights],
        out_specs=out_specs,
        out_shape=out_shape,
        compiler_params=_params("parallel"),
        name="kv_proj",
    )(x, g.reshape(1, D), *weights)


def _q_proj_kernel(x_ref, g_ref, wq_ref, wg_ref, q_ref, gate_ref, *, transposed):
    tm = x_ref.shape[0]
    xn = (_rms(x_ref[...]) * g_ref[...]).astype(BF16)
    q = _dot(xn, wq_ref[...]) * (HEAD_DIM ** -0.5 * (LOG2E if transposed else 1.0))
    gates = jax.nn.sigmoid(_dot(xn, wg_ref[...]))
    if not transposed:
        q_ref[...] = q.astype(BF16)
        gate_ref[...] = gates
        return
    for r in range(tm // PAGE_ROWS):
        rows = slice(r * PAGE_ROWS, (r + 1) * PAGE_ROWS)
        gt = gates[rows].T
        for hk in range(N_KV_HEADS):
            gate_ref[hk, r] = gt[hk * GATE_PAD:(hk + 1) * GATE_PAD]
            for g in range(GQA_GROUP):
                col = (hk * GQA_GROUP + g) * HEAD_DIM
                q_ref[hk, r, :, g * PAGE_ROWS:(g + 1) * PAGE_ROWS] = q[rows, col:col + HEAD_DIM].T.astype(BF16)


def _q_proj(x, g, wq, wg, transposed):
    M, D = x.shape
    tm = min(M, 256)
    assert M % tm == 0 and (tm % PAGE_ROWS == 0 or not transposed)
    tiles = tm // PAGE_ROWS
    gw = GQA_GROUP * PAGE_ROWS
    if transposed:
        out_specs = [pl.BlockSpec((N_KV_HEADS, tiles, HEAD_DIM, gw), lambda i: (0, i, 0, 0)),
                     pl.BlockSpec((N_KV_HEADS, tiles, GATE_PAD, PAGE_ROWS), lambda i: (0, i, 0, 0))]
        out_shape = [jax.ShapeDtypeStruct((N_KV_HEADS, M // PAGE_ROWS, HEAD_DIM, gw), BF16),
                     jax.ShapeDtypeStruct((N_KV_HEADS, M // PAGE_ROWS, GATE_PAD, PAGE_ROWS), F32)]
    else:
        out_specs = [pl.BlockSpec((tm, wq.shape[1]), lambda i: (i, 0)),
                     pl.BlockSpec((tm, wg.shape[1]), lambda i: (i, 0))]
        out_shape = [jax.ShapeDtypeStruct((M, wq.shape[1]), BF16), jax.ShapeDtypeStruct((M, wg.shape[1]), F32)]
    return pl.pallas_call(
        functools.partial(_q_proj_kernel, transposed=transposed),
        grid=(M // tm,),
        in_specs=[pl.BlockSpec((tm, D), lambda i: (i, 0)), pl.BlockSpec((1, D), lambda i: (0, 0)),
                  pl.BlockSpec(wq.shape, lambda i: (0, 0)), pl.BlockSpec(wg.shape, lambda i: (0, 0))],
        out_specs=out_specs,
        out_shape=out_shape,
        compiler_params=_params("parallel"),
        name="q_proj",
    )(x, g.reshape(1, D), wq, wg)


def _gelu_tanh(x):
    return 0.5 * x * (1.0 + jnp.tanh(math.sqrt(2.0 / math.pi) * (x + 0.044715 * (x * x * x))))


def _cmp_bias_kernel(pos_ref, w1_ref, b1_ref, o_ref):
    for c in range(2):
        acc = jnp.zeros((8, HEAD_DIM), F32)
        for j in range(CMP_BLOCK):
            pj = jnp.broadcast_to(pos_ref[c, j:j + 1, :], (8, HEAD_DIM)).astype(BF16)
            acc = acc + _dot(pj, w1_ref[c, j])
        o_ref[c] = acc + b1_ref[c]


def _cmp_bias(pos, w1, b1):
    return pl.pallas_call(
        _cmp_bias_kernel,
        out_shape=jax.ShapeDtypeStruct((2, 8, HEAD_DIM), F32),
        compiler_params=pltpu.CompilerParams(vmem_limit_bytes=VMEM_LIMIT_BYTES),
        name="cmp_bias",
    )(pos, w1, b1)


def _compress_kernel(pt_ref, *refs, n_pages):
    del pt_ref
    page_refs = refs[:n_pages]
    w1_ref, bias_ref, w2_ref, o_ref, scr_ref, sum_ref = refs[n_pages:]
    sub_per_page = PAGE_ROWS // CMP_STRIDE
    n_sub = n_pages * sub_per_page
    sub_rows = CMP_STRIDE * ROW_SPLIT
    pages_per_chunk = 4 if n_pages % 4 == 0 else 1
    chunk_rows = pages_per_chunk * sub_per_page * ROW_SPLIT
    is_k = (lax.broadcasted_iota(jnp.int32, (chunk_rows, 1), 0) & (ROW_SPLIT - 1)) < N_KV_HEADS
    bias = jnp.where(is_k[0:ROW_SPLIT], bias_ref[0], bias_ref[1])
    bias = jnp.concatenate([bias] * (chunk_rows // ROW_SPLIT), axis=0)

    def by_side(x):
        return jnp.concatenate([jnp.where(is_k, x, 0.0), jnp.where(is_k, 0.0, x)], axis=1).astype(BF16)

    for ck in range(n_pages // pages_per_chunk):
        chunk_pages = page_refs[ck * pages_per_chunk:(ck + 1) * pages_per_chunk]
        acc = jnp.zeros((chunk_rows, 2 * HEAD_DIM), F32)
        for j in range(CMP_STRIDE):
            x = jnp.concatenate([r[n * sub_rows + j * ROW_SPLIT:n * sub_rows + (j + 1) * ROW_SPLIT, :]
                                 for r in chunk_pages for n in range(sub_per_page)], axis=0)
            acc = acc + _dot(by_side(x), w1_ref[j])
        scr_ref[ck * chunk_rows:(ck + 1) * chunk_rows, :] = acc
    total = n_sub * ROW_SPLIT
    for ck in range(n_pages // pages_per_chunk):
        rows = slice(ck * chunk_rows, (ck + 1) * chunk_rows)
        first = scr_ref[rows, 0:HEAD_DIM]
        nxt_rows = slice(ck * chunk_rows + ROW_SPLIT, min((ck + 1) * chunk_rows + ROW_SPLIT, total))
        nxt = scr_ref[nxt_rows, HEAD_DIM:2 * HEAD_DIM]
        if nxt.shape[0] < chunk_rows:
            nxt = jnp.concatenate([nxt, jnp.zeros((chunk_rows - nxt.shape[0], HEAD_DIM), F32)], axis=0)
        hid = _gelu_tanh(first + nxt + bias)
        out = _dot(by_side(hid), w2_ref[...])
        if (ck + 1) * chunk_rows == total:
            row = lax.broadcasted_iota(jnp.int32, (chunk_rows, 1), 0)
            out = jnp.where(row < chunk_rows - ROW_SPLIT, out, 0.0)
        sum_ref[rows, :] = out
    for ch in range(ROW_SPLIT):
        o_ref[0, :, ch * HEAD_DIM:(ch + 1) * HEAD_DIM] = sum_ref[pl.ds(ch, n_sub, stride=ROW_SPLIT), :]


def _compress(pages, table, w1cat, bias, w2cat):
    nb, n_pages = table.shape
    n_sub = n_pages * (PAGE_ROWS // CMP_STRIDE)
    page_rows2d = PAGE_ROWS * ROW_SPLIT
    page_specs = [pl.BlockSpec((page_rows2d, HEAD_DIM), functools.partial(lambda b, pt, p: (pt[b, p], 0), p=p))
                  for p in range(n_pages)]
    const = lambda shape: pl.BlockSpec(shape, lambda b, pt: (0,) * len(shape))
    grid_spec = pltpu.PrefetchScalarGridSpec(
        num_scalar_prefetch=1,
        grid=(nb,),
        in_specs=page_specs + [const(w1cat.shape), const(bias.shape), const(w2cat.shape)],
        out_specs=pl.BlockSpec((1, n_sub, KV_COLS), lambda b, pt: (b, 0, 0)),
        scratch_shapes=[pltpu.VMEM((n_sub * ROW_SPLIT, 2 * HEAD_DIM), F32),
                        pltpu.VMEM((n_sub * ROW_SPLIT, HEAD_DIM), F32)],
    )
    return pl.pallas_call(
        functools.partial(_compress_kernel, n_pages=n_pages),
        grid_spec=grid_spec,
        out_shape=jax.ShapeDtypeStruct((nb, n_sub, KV_COLS), F32),
        compiler_params=_params("parallel"),
        name="compress",
    )(table, *([pages] * n_pages), w1cat, bias, w2cat)


def _rel_bucket_np(dist):
    n = np.maximum(dist, 0)
    max_exact = N_BUCKETS // 2
    nf = np.maximum(n, 1).astype(np.float32)
    large = max_exact + (np.log(nf / np.float32(max_exact)) / np.float32(math.log(MAX_DISTANCE / max_exact))
                         * np.float32(N_BUCKETS - max_exact)).astype(np.int32)
    large = np.minimum(large, N_BUCKETS - 1)
    return np.where(n < max_exact, n, large).astype(np.int32)


def _bias_table_kernel(rb_ref, bk_ref, o_ref, *, minus_far, scale):
    bk = bk_ref[...]
    for h in range(N_HEADS):
        v = jnp.full(bk.shape, rb_ref[0, h], F32)
        for b in range(1, N_BUCKETS):
            v = jnp.where(bk == b, rb_ref[b, h], v)
        if minus_far:
            v = v - rb_ref[N_BUCKETS - 1, h]
        o_ref[h] = v if scale == 1.0 else v * scale


def _bias_table(rel_bias, bucket, minus_far=False, scale=1.0):
    R, C = bucket.shape
    tr = min(R, 128)
    assert R % tr == 0
    return pl.pallas_call(
        functools.partial(_bias_table_kernel, minus_far=minus_far, scale=scale),
        grid=(R // tr,),
        in_specs=[pl.BlockSpec(memory_space=pltpu.SMEM), pl.BlockSpec((tr, C), lambda r: (r, 0))],
        out_specs=pl.BlockSpec((N_HEADS, tr, C), lambda r: (0, r, 0)),
        out_shape=jax.ShapeDtypeStruct((N_HEADS, R, C), F32),
        compiler_params=_params("parallel"),
        name="bias_table",
    )(rel_bias, jnp.asarray(bucket))


def _overlap_np(n_sub, n_slc):
    n_cmp = n_sub - 1
    cs = np.arange(n_cmp) * CMP_STRIDE
    ce = cs + CMP_BLOCK - 1
    ss = np.arange(n_slc) * SLC_BLOCK
    se = ss + SLC_BLOCK - 1
    ov = np.minimum(ce[:, None], se[None, :]) - np.maximum(cs[:, None], ss[None, :]) + 1
    out = np.zeros((n_sub, n_slc), np.float32)
    out[:n_cmp] = np.maximum(ov, 0).astype(np.float32) / CMP_BLOCK
    return out


def _pad_to(a, shape):
    return np.pad(a, [(0, s - d) for d, s in zip(a.shape, shape)])


def _score(imp, j, qb, n_slc):
    prio = 2.0 * (j == qb).astype(F32) + (j == qb - 1).astype(F32) + (j == 0).astype(F32)
    score = jnp.where(j <= qb, imp + SEL_BIG * prio, -SEL_BIG)
    return jnp.where(j < n_slc, score, -3e38)


def _attn_prompt_kernel(qt_ref, gt_ref, ck_ref, cv_ref, sk_ref, svt_ref, wk_ref, wvt_ref,
                        toe_ref, cmpb_ref, ovt_ref, o_ref,
                        m_ref, l_ref, acc_ref, accw_ref, *, n_slc, n_sub, hps):
    tq = PAGE_ROWS
    i = pl.program_id(2)
    heads = range(hps)
    t_idx = lax.broadcasted_iota(jnp.int32, (1, tq), 1)
    qpos = i * tq + t_idx
    lane4 = lambda x: jnp.concatenate([x] * GQA_GROUP, axis=1)
    n_idx = lax.broadcasted_iota(jnp.int32, (n_sub, 1), 0)
    valid = lane4((n_idx * CMP_STRIDE + (CMP_BLOCK - 1) <= qpos) & (n_idx < n_sub - 1))
    nsp = ovt_ref.shape[0]
    j = lax.broadcasted_iota(jnp.int32, (nsp, 1), 0)
    qb = qpos >> int(math.log2(SLC_BLOCK))

    qts, qt_sels, o_cs = [], [], []
    for hh in heads:
        qt = qt_ref[hh, 0]
        cols = slice(hh * HEAD_DIM, (hh + 1) * HEAD_DIM)
        kc = ck_ref[0, :, cols].astype(BF16)
        vct = cv_ref[0, :, cols].T.astype(BF16)
        bias_c = jnp.concatenate([cmpb_ref[hh * GQA_GROUP + g, 0] for g in range(GQA_GROUP)], axis=1)
        logits = jnp.where(valid, _dot(kc, qt) + bias_c, NEG_INF)
        p = jnp.exp2(logits - jnp.max(logits, axis=0, keepdims=True))
        p = p * (1.0 / jnp.sum(p, axis=0, keepdims=True))
        p = jnp.where(valid, p, 0.0)
        o_cs.append(_dot(vct, p.astype(BF16)))
        psum = p[:, 0:tq]
        for g in range(1, GQA_GROUP):
            psum = psum + p[:, g * tq:(g + 1) * tq]
        imp = jnp.dot(ovt_ref[...], psum, precision=lax.Precision.HIGHEST, preferred_element_type=F32)
        score = _score(imp, j, qb, n_slc)
        rank = jnp.zeros((nsp, tq), jnp.int32)
        for jj in range(n_slc):
            row = score[jj:jj + 1, :]
            rank = rank + ((row > score) | ((row == score) & (jj < j))).astype(jnp.int32)
        madd = jnp.where(rank < min(N_SELECT, n_slc), 0.0, NEG_INF)
        qts.append(qt)
        qt_sels.append(jnp.concatenate([qt, lane4(madd).astype(BF16),
                                        jnp.zeros((HEAD_DIM - nsp, GQA_GROUP * tq), BF16)], axis=0))

    k_row = lax.broadcasted_iota(jnp.int32, (tq, 1), 0)

    def chunk(hh, k, kinds, k_ref, vt_ref, q_op, out_ref):
        nt = len(kinds)
        k0 = pl.multiple_of(k * tq, tq)
        s = _dot(k_ref[hh, pl.ds(k0, nt * tq), :], q_op)
        parts = []
        for jx, kind in enumerate(kinds):
            sj = s[jx * tq:(jx + 1) * tq]
            if kind in ("near", "diag"):
                d = 0 if kind == "diag" else 1
                sj = sj + jnp.concatenate([toe_ref[(hh * GQA_GROUP + g) * 2 + d] for g in range(GQA_GROUP)],
                                          axis=1)
            if kind == "diag":
                sj = jnp.where(lane4(k_row <= t_idx), sj, NEG_INF)
            elif kind == "edge":
                sj = jnp.where(lane4(k_row > t_idx), sj, NEG_INF)
            parts.append(sj)
        m_new = functools.reduce(jnp.maximum, parts)
        m_new = jnp.max(m_new, axis=0, keepdims=True)
        if out_ref is None:
            m_old = m_ref[hh]
            m_new = jnp.maximum(m_old, m_new)
            alpha = jnp.exp2(m_old - m_new)
        ps = [jnp.exp2(sj - m_new) for sj in parts]
        lsum = jnp.sum(functools.reduce(jnp.add, ps), axis=0, keepdims=True)
        vt = jnp.concatenate([vt_ref[hh, k + jx] for jx in range(nt)], axis=1)
        pv = _dot(vt, jnp.concatenate(ps, axis=0).astype(BF16))
        if out_ref is None:
            l_ref[hh] = alpha * l_ref[hh] + lsum
            acc_ref[hh] = alpha * acc_ref[hh] + pv
            m_ref[hh] = m_new
        else:
            out_ref[hh] = pv * (1.0 / lsum)

    def tail_kinds(n_far):
        return ("far",) * n_far + ("near", "diag")

    far_step = SEL_FAR_TILES
    wt = WINDOW // tq
    m_ref[...] = jnp.full(m_ref.shape, NEG_INF, F32)
    l_ref[...] = jnp.zeros(l_ref.shape, F32)
    acc_ref[...] = jnp.zeros(acc_ref.shape, F32)

    n_far = jnp.maximum(i - 1, 0) // far_step

    def far_body(c, carry):
        for hh in heads:
            chunk(hh, c * far_step, ("far",) * far_step, sk_ref, svt_ref, qt_sels[hh], None)
        return carry

    lax.fori_loop(0, n_far, far_body, 0)

    for c in range(wt):
        @pl.when(i == c)
        def _(c=c):
            kinds = tail_kinds(c - 1) if c >= 1 else ("diag",)
            for hh in heads:
                chunk(hh, 0, kinds, sk_ref, svt_ref, qt_sels[hh], None)
                chunk(hh, 0, kinds, wk_ref, wvt_ref, qts[hh], accw_ref)

    for r in range(far_step):
        @pl.when((i >= wt) & (lax.rem(i + far_step - 1, far_step) == r))
        def _(r=r):
            for hh in heads:
                chunk(hh, i - 1 - r, tail_kinds(r), sk_ref, svt_ref, qt_sels[hh], None)
                chunk(hh, i - wt, ("edge",) + tail_kinds(wt - 2), wk_ref, wvt_ref, qts[hh], accw_ref)

    for hh in heads:
        o_s = acc_ref[hh] * (1.0 / l_ref[hh])
        o_w = accw_ref[hh]
        o_c = o_cs[hh]
        gt = gt_ref[hh, 0]
        for g in range(GQA_GROUP):
            lanes = slice(g * tq, (g + 1) * tq)
            o = (gt[g:g + 1] * o_c[:, lanes] + gt[GQA_GROUP + g:GQA_GROUP + g + 1] * o_s[:, lanes]
                 + gt[2 * GQA_GROUP + g:2 * GQA_GROUP + g + 1] * o_w[:, lanes])
            col = (hh * GQA_GROUP + g) * HEAD_DIM
            o_ref[0, :, col:col + HEAD_DIM] = o.T.astype(o_ref.dtype)


def _attn_prompt(qt, gt, ckv, sk, svt, wk, wvt, toe, cmpb, B, S):
    tq = PAGE_ROWS
    nq = S // tq
    n_sub = ckv.shape[1]
    n_slc = -(-S // SLC_BLOCK)
    nsp = -(-n_slc // 16) * 16
    assert S % tq == 0 and nsp <= HEAD_DIM and nq > WINDOW // tq
    ovt = jnp.asarray(_pad_to(_overlap_np(n_sub, n_slc).T, (nsp, n_sub)))
    gw = GQA_GROUP * tq
    hps = PROMPT_HEADS_PER_STEP
    assert N_KV_HEADS % hps == 0
    return pl.pallas_call(
        functools.partial(_attn_prompt_kernel, n_slc=n_slc, n_sub=n_sub, hps=hps),
        grid=(B, N_KV_HEADS // hps, nq),
        in_specs=[pl.BlockSpec((hps, 1, HEAD_DIM, gw), lambda b, h, i: (h, b * nq + i, 0, 0)),
                  pl.BlockSpec((hps, 1, GATE_PAD, tq), lambda b, h, i: (h, b * nq + i, 0, 0)),
                  pl.BlockSpec((1, n_sub, hps * HEAD_DIM), lambda b, h, i: (b, 0, h)),
                  pl.BlockSpec((1, n_sub, hps * HEAD_DIM), lambda b, h, i: (b, 0, N_KV_HEADS // hps + h)),
                  pl.BlockSpec((hps, S, 2 * HEAD_DIM), lambda b, h, i: (h, b, 0)),
                  pl.BlockSpec((hps, nq, HEAD_DIM, tq), lambda b, h, i: (h, b, 0, 0)),
                  pl.BlockSpec((hps, S, HEAD_DIM), lambda b, h, i: (h, b, 0)),
                  pl.BlockSpec((hps, nq, HEAD_DIM, tq), lambda b, h, i: (h, b, 0, 0)),
                  pl.BlockSpec((hps * GQA_GROUP * 2, tq, tq), lambda b, h, i: (h, 0, 0)),
                  pl.BlockSpec((hps * GQA_GROUP, 1, n_sub, tq), lambda b, h, i: (h, i, 0, 0)),
                  pl.BlockSpec(ovt.shape, lambda b, h, i: (0, 0))],
        out_specs=pl.BlockSpec((1, tq, hps * GQA_GROUP * HEAD_DIM), lambda b, h, i: (b, i, h)),
        out_shape=jax.ShapeDtypeStruct((B, S, N_HEADS * HEAD_DIM), BF16),
        scratch_shapes=[pltpu.VMEM((hps, 1, gw), F32), pltpu.VMEM((hps, 1, gw), F32),
                        pltpu.VMEM((hps, HEAD_DIM, gw), F32), pltpu.VMEM((hps, HEAD_DIM, gw), F32)],
        compiler_params=_params("parallel", "parallel", "arbitrary"),
        name="attn_prompt",
    )(qt, gt, ckv, ckv, sk, svt, wk, wvt, toe, cmpb, ovt)


def _attn_sample_kernel(pt_ref, *refs, n_pages, n_slc, n_sub, past_len, win_buf, ds):
    del pt_ref
    page_refs = refs[:n_pages]
    (q_ref, gate_ref, ckv_ref, snew_ref, wst_ref, wnew_ref, bias_ref, ov_ref, e_ref,
     o_ref, wout_ref) = refs[n_pages:]
    nk_s = (n_pages + 1) * PAGE_ROWS
    nk_w = win_buf + PAGE_ROWS
    t_idx = lax.broadcasted_iota(jnp.int32, (ds, 1), 0)
    qpos = past_len + t_idx
    pad = jnp.zeros((PAGE_ROWS - ds, HEAD_DIM), F32)
    gates = gate_ref[0]
    row4 = lambda x: jnp.concatenate([x] * GQA_GROUP, axis=0)
    head_rows = lambda r, col, n: r[pl.ds(col, n, stride=ROW_SPLIT), :]

    shift = ds * ROW_SPLIT
    wout_ref[0:win_buf * ROW_SPLIT - shift, :] = wst_ref[shift:win_buf * ROW_SPLIT, :]
    wout_ref[win_buf * ROW_SPLIT - shift:win_buf * ROW_SPLIT, :] = wnew_ref[...]

    ks_pos = lax.broadcasted_iota(jnp.int32, (1, nk_s), 1)
    dist_w = win_buf + t_idx - lax.broadcasted_iota(jnp.int32, (1, nk_w), 1)
    ok_w4 = row4((dist_w >= 0) & (dist_w < WINDOW))
    n_idx = lax.broadcasted_iota(jnp.int32, (1, n_sub), 1)
    ok_c4 = row4((n_idx * CMP_STRIDE + (CMP_BLOCK - 1) <= qpos) & (n_idx < n_sub - 1))
    lanes = ov_ref.shape[1]
    j = lax.broadcasted_iota(jnp.int32, (1, lanes), 1)
    qb = qpos >> int(math.log2(SLC_BLOCK))

    def softmax(logits, ok4):
        logits = jnp.where(ok4, logits, NEG_INF)
        p = jnp.exp(logits - jnp.max(logits, axis=-1, keepdims=True))
        return p / jnp.sum(p, axis=-1, keepdims=True)

    for hk in range(N_KV_HEADS):
        kcol, vcol = hk, N_KV_HEADS + hk
        qcol = hk * GQA_GROUP * HEAD_DIM
        q4 = jnp.concatenate([q_ref[0, :, qcol + g * HEAD_DIM:qcol + (g + 1) * HEAD_DIM]
                              for g in range(GQA_GROUP)], axis=0)
        hb = bias_ref[hk * GQA_GROUP:(hk + 1) * GQA_GROUP].reshape(GQA_GROUP * ds, bias_ref.shape[2])

        kc = ckv_ref[0, :, kcol * HEAD_DIM:(kcol + 1) * HEAD_DIM].astype(BF16)
        vc = ckv_ref[0, :, vcol * HEAD_DIM:(vcol + 1) * HEAD_DIM].astype(BF16)
        p = jnp.where(ok_c4, softmax(_dot_nt(q4, kc) + hb[:, nk_s + nk_w:nk_s + nk_w + n_sub], ok_c4), 0.0)
        o_c = _dot(p.astype(BF16), vc)
        psum = p[0:ds]
        for g in range(1, GQA_GROUP):
            psum = psum + p[g * ds:(g + 1) * ds]
        imp = jnp.dot(psum, ov_ref[...], precision=lax.Precision.HIGHEST, preferred_element_type=F32)
        score = _score(imp, j, qb, n_slc)
        rank = jnp.zeros((ds, lanes), jnp.int32)
        for jj in range(n_slc):
            col = score[:, jj:jj + 1]
            rank = rank + ((col > score) | ((col == score) & (jj < j))).astype(jnp.int32)
        sel = ((rank < min(N_SELECT, n_slc)) & (j < n_slc)).astype(BF16)
        ok_s4 = row4((_dot(sel, e_ref[...]) > 0.5) & (ks_pos <= qpos))

        k_s = jnp.concatenate([head_rows(r, kcol, PAGE_ROWS) for r in page_refs]
                              + [head_rows(snew_ref, kcol, ds), pad], axis=0).astype(BF16)
        v_s = jnp.concatenate([head_rows(r, vcol, PAGE_ROWS) for r in page_refs]
                              + [head_rows(snew_ref, vcol, ds), pad], axis=0).astype(BF16)
        o_s = _dot(softmax(_dot_nt(q4, k_s) + hb[:, 0:nk_s], ok_s4).astype(BF16), v_s)

        k_w = jnp.concatenate([head_rows(wst_ref, kcol, win_buf), head_rows(wnew_ref, kcol, ds), pad],
                              axis=0).astype(BF16)
        v_w = jnp.concatenate([head_rows(wst_ref, vcol, win_buf), head_rows(wnew_ref, vcol, ds), pad],
                              axis=0).astype(BF16)
        o_w = _dot(softmax(_dot_nt(q4, k_w) + hb[:, nk_s:nk_s + nk_w], ok_w4).astype(BF16), v_w)

        gcol = hk * GATE_PAD
        for g in range(GQA_GROUP):
            rows = slice(g * ds, (g + 1) * ds)
            gate = lambda br, g=g: gates[:, gcol + br * GQA_GROUP + g:gcol + br * GQA_GROUP + g + 1]
            o = gate(0) * o_c[rows] + gate(1) * o_s[rows] + gate(2) * o_w[rows]
            o_ref[0, :, qcol + g * HEAD_DIM:qcol + (g + 1) * HEAD_DIM] = o.astype(o_ref.dtype)


def _attn_sample(q, gates, ckv, cache_slc, table, slc_new, win_state, win_new, bias, past_len, win_buf):
    DB, ds, _ = q.shape
    n_pages = table.shape[1]
    n_sub = ckv.shape[1]
    n_slc = -(-(past_len + ds) // SLC_BLOCK)
    lanes = PAGE_ROWS
    assert ds % 8 == 0 and ds <= PAGE_ROWS and n_slc <= lanes and n_sub <= lanes
    assert win_buf <= past_len and win_buf > ds
    nk_s = (n_pages + 1) * PAGE_ROWS
    ov = jnp.asarray(_pad_to(_overlap_np(n_sub, n_slc), (n_sub, lanes)))
    e = (np.arange(nk_s)[None, :] // SLC_BLOCK == np.arange(lanes)[:, None]) & (np.arange(lanes)[:, None] < n_slc)
    e = jnp.asarray(e.astype(np.float32), dtype=BF16)
    page_specs = [pl.BlockSpec((PAGE_ROWS * ROW_SPLIT, HEAD_DIM),
                               functools.partial(lambda b, pt, p: (pt[b, p], 0), p=p)) for p in range(n_pages)]
    per_b3 = lambda shape: pl.BlockSpec((1,) + shape[1:], lambda b, pt: (b, 0, 0))
    rows_b = lambda n: pl.BlockSpec((n * ROW_SPLIT, HEAD_DIM), lambda b, pt: (b, 0))
    const = lambda shape: pl.BlockSpec(shape, lambda b, pt: (0,) * len(shape))
    grid_spec = pltpu.PrefetchScalarGridSpec(
        num_scalar_prefetch=1,
        grid=(DB,),
        in_specs=page_specs + [per_b3(q.shape), per_b3(gates.shape), per_b3(ckv.shape), rows_b(ds),
                               rows_b(win_buf), rows_b(ds), const(bias.shape), const(ov.shape), const(e.shape)],
        out_specs=[per_b3(q.shape), rows_b(win_buf)],
    )
    return pl.pallas_call(
        functools.partial(_attn_sample_kernel, n_pages=n_pages, n_slc=n_slc, n_sub=n_sub,
                          past_len=past_len, win_buf=win_buf, ds=ds),
        grid_spec=grid_spec,
        out_shape=[jax.ShapeDtypeStruct(q.shape, BF16), jax.ShapeDtypeStruct(win_state.shape, F32)],
        compiler_params=_params("parallel"),
        name="attn_sample",
    )(table, *([cache_slc] * n_pages), q, gates, ckv, slc_new, win_state, win_new, bias, ov, e)


def _prompt_buckets(S, n_sub):
    tq = PAGE_ROWS
    k = np.arange(tq)[:, None]
    t = np.arange(tq)[None, :]
    toe = np.concatenate([_rel_bucket_np(t - k + d * tq) for d in range(2)], axis=0)
    kpos_c = np.arange(n_sub)[:, None] * CMP_STRIDE + CMP_BLOCK - 1
    cmp = np.concatenate([_rel_bucket_np(i * tq + t - kpos_c) for i in range(S // tq)], axis=0)
    return toe, cmp


def _sample_buckets(ds, past_len, n_pages, win_buf, n_sub):
    qpos = past_len + np.arange(ds)[:, None]
    ks = np.arange((n_pages + 1) * PAGE_ROWS)[None, :]
    kw = past_len - win_buf + np.arange(win_buf + PAGE_ROWS)[None, :]
    kc = np.arange(n_sub)[None, :] * CMP_STRIDE + CMP_BLOCK - 1
    return np.concatenate([_rel_bucket_np(qpos - ks), _rel_bucket_np(qpos - kw), _rel_bucket_np(qpos - kc)], axis=1)


def kernel(x_prompt, x_sample, state_pool, cache_cmp_kv, cache_slc_kv, state_win_kv, page_table, norms, w_pool,
           pool_scale, kv_norm, w_kv, cmp_pos, cmp_w1, cmp_b1, cmp_w2, w_q, w_o, rel_bias, w_ffn_in, w_ffn_out):
    B, S, D = x_prompt.shape
    DB, DS, _ = x_sample.shape
    n_pages = page_table.shape[1]
    past_len = n_pages * cache_slc_kv.shape[1]
    win_buf = state_win_kv.shape[1]
    assert norms.shape[0] == 2 and w_pool.shape[0] == 1 and w_q.shape[0] == 1, "one pooling + one attention layer"
    assert cache_slc_kv.shape[1] == PAGE_ROWS and S % PAGE_ROWS == 0
    assert past_len % CMP_STRIDE == 0 and DS < CMP_STRIDE and D == N_HEADS * HEAD_DIM
    assert S >= win_buf and N_BRANCH * GQA_GROUP <= GATE_PAD

    qdim = N_HEADS * HEAD_DIM
    w_pool_b = w_pool[0].astype(BF16)
    w_in_b = w_ffn_in.astype(BF16)
    w_out_b = w_ffn_out.astype(BF16)
    w_kv_b = [w_kv[:, k * KV_COLS:(k + 1) * KV_COLS].astype(BF16) for k in range(N_BRANCH)]
    w_qq = w_q[0, :, :qdim].astype(BF16)
    w_qg = w_q[0, :, qdim:].reshape(D, N_BRANCH, N_KV_HEADS, GQA_GROUP).transpose(0, 2, 1, 3)
    w_qg = jnp.pad(w_qg.reshape(D, N_KV_HEADS, N_BRANCH * GQA_GROUP),
                   ((0, 0), (0, 0), (0, GATE_PAD - N_BRANCH * GQA_GROUP)))
    w_qg = jnp.pad(w_qg.reshape(D, N_KV_HEADS * GATE_PAD), ((0, 0), (0, 128 - N_KV_HEADS * GATE_PAD))).astype(BF16)
    w_o_b = w_o.astype(BF16)
    w1_b = cmp_w1.astype(BF16)
    w1cat = w1_b.reshape(2, 2, CMP_STRIDE, HEAD_DIM, HEAD_DIM).transpose(2, 0, 3, 1, 4)
    w1cat = w1cat.reshape(CMP_STRIDE, 2 * HEAD_DIM, 2 * HEAD_DIM)
    w2cat = cmp_w2.astype(BF16).reshape(2 * HEAD_DIM, HEAD_DIM)
    cmp_bias = _cmp_bias(cmp_pos, w1_b, cmp_b1.reshape(2, 1, HEAD_DIM))

    def layer0(x, buf, pos0):
        Bx, Sx, _ = x.shape
        M = Bx * Sx
        mix, st = _pool_mix(x, buf, norms[0, 0], pos0)
        h = _proj_res(mix.reshape(M, D), w_pool_b, pool_scale[0], norms[0, 1], x.reshape(M, D), "pool_proj")
        return _ffn(h, norms[0, 2], norms[0, 3], w_in_b[0], w_out_b[0]), st

    def layer1_tail(h1, o):
        h2 = _proj_res(o, w_o_b, None, norms[1, 1], h1, "o_proj")
        return _ffn(h2, norms[1, 2], norms[1, 3], w_in_b[1], w_out_b[1])

    rows2d = lambda a: a.reshape(-1, HEAD_DIM)
    kv5 = lambda r, b, s: r.reshape(b, s, 2, N_KV_HEADS, HEAD_DIM)

    h1, st_p = layer0(x_prompt, jnp.zeros((B, POOL_BUF, D), F32), 0)
    cmp_p, slc_p, win_p, sk, svt, wk, wvt = _kv_proj(h1, kv_norm, w_kv_b, True, S)
    qt, gt = _q_proj(h1, norms[1, 0], w_qq, w_qg, True)
    pages_p = S // PAGE_ROWS
    ckv_p = _compress(cmp_p, jnp.arange(B * pages_p, dtype=jnp.int32).reshape(B, pages_p), w1cat, cmp_bias, w2cat)
    n_sub_p = ckv_p.shape[1]
    toe_bk, cmp_bk = _prompt_buckets(S, n_sub_p)
    toe = _bias_table(rel_bias, toe_bk, True, LOG2E).reshape(N_HEADS * 2, PAGE_ROWS, PAGE_ROWS)
    cmpb = _bias_table(rel_bias, cmp_bk, False, LOG2E).reshape(N_HEADS, pages_p, n_sub_p, PAGE_ROWS)
    o_p = _attn_prompt(qt, gt, ckv_p, sk, svt, wk, wvt, toe, cmpb, B, S)
    y_prompt = layer1_tail(h1, o_p.reshape(B * S, qdim)).reshape(B, S, D)
    win_state_p = win_p.reshape(B, S * ROW_SPLIT, HEAD_DIM)[:, (S - win_buf) * ROW_SPLIT:]

    h1s, st_s = layer0(x_sample, state_pool[0], past_len)
    cmp_s, slc_s, win_s = _kv_proj(h1s, kv_norm, w_kv_b, False, DS)
    q_s, gates_s = _q_proj(h1s, norms[1, 0], w_qq, w_qg, False)
    ckv_s = _compress(rows2d(cache_cmp_kv), page_table, w1cat, cmp_bias, w2cat)
    bias_s = _bias_table(rel_bias, _sample_buckets(DS, past_len, n_pages, win_buf, ckv_s.shape[1]))
    o_s, win_state_s = _attn_sample(
        q_s.reshape(DB, DS, qdim), gates_s.reshape(DB, DS, 128), ckv_s, rows2d(cache_slc_kv), page_table,
        slc_s, rows2d(state_win_kv), win_s, bias_s, past_len, win_buf)
    y_sample = layer1_tail(h1s, o_s.reshape(DB * DS, qdim)).reshape(DB, DS, D)

    return (y_prompt, y_sample, st_p[None], st_s[None],
            kv5(cmp_p, B, S), kv5(cmp_s, DB, DS), kv5(slc_p, B, S), kv5(slc_s, DB, DS),
            kv5(win_state_p, B, win_buf), kv5(win_state_s, DB, win_buf))
```

```python
import functools
import math

import numpy as np
import jax
import jax.numpy as jnp
from jax import lax
from jax.experimental import pallas as pl
from jax.experimental.pallas import tpu as pltpu

F32 = jnp.float32
BF16 = jnp.bfloat16

HEAD_DIM = 128
N_KV_HEADS = 4
GQA_GROUP = 4
N_HEADS = N_KV_HEADS * GQA_GROUP
N_BRANCH = 3
POOL_WINDOWS = (2, 4, 8, 16)
POOL_BUF = max(POOL_WINDOWS) - 1
POOL_HALO = POOL_BUF + 1
CMP_BLOCK = 32
CMP_STRIDE = 16
SLC_BLOCK = 64
N_SELECT = 16
WINDOW = 512
N_BUCKETS = 32
MAX_DISTANCE = 128
RMS_EPS = 1e-6
SEL_BIG = 1e3
NEG_INF = -1e30
LOG2E = math.log2(math.e)
PAGE_ROWS = 128
KV_COLS = 2 * N_KV_HEADS * HEAD_DIM
ROW_SPLIT = 2 * N_KV_HEADS
GATE_PAD = 16
SEL_FAR_TILES = 4
PROMPT_HEADS_PER_STEP = 4

VMEM_LIMIT_BYTES = 56 * 1024 * 1024

_NT = (((1,), (1,)), ((), ()))


def _params(*sem):
    return pltpu.CompilerParams(dimension_semantics=sem, vmem_limit_bytes=VMEM_LIMIT_BYTES)


def _rms(x):
    return x * lax.rsqrt(jnp.mean(x * x, axis=-1, keepdims=True) + RMS_EPS)


def _dot(a, b):
    return jnp.dot(a, b, preferred_element_type=F32)


def _dot_nt(a, b):
    return lax.dot_general(a, b, _NT, preferred_element_type=F32)


def _pool_mix_kernel(h_ref, buf_ref, g_ref, mix_ref, st_ref, ext_ref, *, ts, pos0, group_dim):
    i = pl.program_id(1)

    @pl.when(i == 0)
    def _():
        ext_ref[0:POOL_HALO, :] = buf_ref[0]

    @pl.when(i > 0)
    def _():
        ext_ref[0:POOL_HALO, :] = ext_ref[ts:ts + POOL_HALO, :]

    x = h_ref[0]
    ext_ref[POOL_HALO:POOL_HALO + ts, :] = _rms(x) * g_ref[...]
    pos = pos0 + i * ts + lax.broadcasted_iota(jnp.int32, (ts, 1), 0)
    for gi, w in enumerate(POOL_WINDOWS):
        cols = slice(gi * group_dim, (gi + 1) * group_dim)
        cur = ext_ref[POOL_HALO:POOL_HALO + ts, cols]
        s = cur
        for u in range(1, w):
            s = s + ext_ref[POOL_HALO - u:POOL_HALO - u + ts, cols]
        cnt = jnp.minimum(w, pos + 1).astype(F32)
        mix_ref[0, :, cols] = (s / cnt - cur).astype(mix_ref.dtype)
    st_ref[0] = ext_ref[ts:ts + POOL_HALO, :]


def _pool_mix(h, buf, g, pos0):
    B, S, D = h.shape
    ts = min(S, 256)
    assert S % ts == 0 and (ts >= POOL_HALO or S == ts) and ts % 8 == 0
    buf16 = jnp.pad(buf, ((0, 0), (POOL_HALO - POOL_BUF, 0), (0, 0)))
    mix, st = pl.pallas_call(
        functools.partial(_pool_mix_kernel, ts=ts, pos0=pos0, group_dim=D // len(POOL_WINDOWS)),
        grid=(B, S // ts),
        in_specs=[pl.BlockSpec((1, ts, D), lambda b, i: (b, i, 0)),
                  pl.BlockSpec((1, POOL_HALO, D), lambda b, i: (b, 0, 0)),
                  pl.BlockSpec((1, D), lambda b, i: (0, 0))],
        out_specs=[pl.BlockSpec((1, ts, D), lambda b, i: (b, i, 0)),
                   pl.BlockSpec((1, POOL_HALO, D), lambda b, i: (b, 0, 0))],
        out_shape=[jax.ShapeDtypeStruct((B, S, D), BF16),
                   jax.ShapeDtypeStruct((B, POOL_HALO, D), F32)],
        scratch_shapes=[pltpu.VMEM((POOL_HALO + ts, D), F32)],
        compiler_params=_params("parallel", "arbitrary"),
        name="pool_mix",
    )(h, buf16, g.reshape(1, D))
    return mix, st[:, POOL_HALO - POOL_BUF:]


def _proj_res_kernel(*refs, groups, has_scale):
    if has_scale:
        a_ref, w_ref, scale_ref, g_ref, res_ref, o_ref = refs
    else:
        a_ref, w_ref, g_ref, res_ref, o_ref = refs
    gd = a_ref.shape[1] // groups
    parts = [_dot(a_ref[:, gi * gd:(gi + 1) * gd], w_ref[gi]) for gi in range(groups)]
    y = parts[0] if groups == 1 else jnp.concatenate(parts, axis=-1)
    if has_scale:
        y = y * scale_ref[...]
    o_ref[...] = res_ref[...] + _rms(y) * g_ref[...]


def _proj_res(a, w, scale, g, res, name):
    M, D = a.shape
    groups = w.shape[0]
    tm = min(M, 256)
    assert M % tm == 0
    row = lambda i: (i, 0)
    fixed = lambda i: (0, 0)
    in_specs = [pl.BlockSpec((tm, D), row), pl.BlockSpec(w.shape, lambda i: (0, 0, 0))]
    args = [a, w]
    if scale is not None:
        in_specs.append(pl.BlockSpec((1, D), fixed))
        args.append(scale.reshape(1, D))
    in_specs += [pl.BlockSpec((1, D), fixed), pl.BlockSpec((tm, D), row)]
    args += [g.reshape(1, D), res]
    return pl.pallas_call(
        functools.partial(_proj_res_kernel, groups=groups, has_scale=scale is not None),
        grid=(M // tm,),
        in_specs=in_specs,
        out_specs=pl.BlockSpec((tm, D), row),
        out_shape=jax.ShapeDtypeStruct((M, D), F32),
        compiler_params=_params("parallel"),
        name=name,
    )(*args)


def _ffn_kernel(h_ref, gi_ref, go_ref, wg_ref, wu_ref, wo_ref, out_ref, xn_ref, acc_ref):
    f = pl.program_id(1)

    @pl.when(f == 0)
    def _():
        xn_ref[...] = (_rms(h_ref[...]) * gi_ref[...]).astype(xn_ref.dtype)
        acc_ref[...] = jnp.zeros_like(acc_ref)

    xn = xn_ref[...]
    gate = _dot(xn, wg_ref[...])
    up = _dot(xn, wu_ref[...])
    act = (gate * jax.nn.sigmoid(gate) * up).astype(BF16)
    acc_ref[...] += _dot(act, wo_ref[...])

    @pl.when(f == pl.num_programs(1) - 1)
    def _():
        out_ref[...] = h_ref[...] + _rms(acc_ref[...]) * go_ref[...]


def _ffn(h, g_in, g_out, w_in, w_out):
    M, D = h.shape
    F = w_out.shape[0]
    tm = min(M, 512)
    tf = 512
    assert M % tm == 0 and F % tf == 0
    nf = F // tf
    return pl.pallas_call(
        _ffn_kernel,
        grid=(M // tm, nf),
        in_specs=[pl.BlockSpec((tm, D), lambda i, f: (i, 0)),
                  pl.BlockSpec((1, D), lambda i, f: (0, 0)),
                  pl.BlockSpec((1, D), lambda i, f: (0, 0)),
                  pl.BlockSpec((D, tf), lambda i, f: (0, f)),
                  pl.BlockSpec((D, tf), lambda i, f: (0, nf + f)),
                  pl.BlockSpec((tf, D), lambda i, f: (f, 0))],
        out_specs=pl.BlockSpec((tm, D), lambda i, f: (i, 0)),
        out_shape=jax.ShapeDtypeStruct((M, D), F32),
        scratch_shapes=[pltpu.VMEM((tm, D), BF16), pltpu.VMEM((tm, D), F32)],
        compiler_params=_params("parallel", "arbitrary"),
        name="ffn",
    )(h, g_in.reshape(1, D), g_out.reshape(1, D), w_in, w_in, w_out)


def _kv_proj_kernel(x_ref, g_ref, *refs, emit_aux, seq):
    w_refs, out_refs = refs[:N_BRANCH], refs[N_BRANCH:]
    tm = x_ref.shape[0]
    xn = (_rms(x_ref[...]) * g_ref[...]).astype(BF16)
    for br in range(N_BRANCH):
        y = _dot(xn, w_refs[br][...])
        for j in range(ROW_SPLIT):
            out_refs[br][pl.ds(j, tm, stride=ROW_SPLIT), :] = y[:, j * HEAD_DIM:(j + 1) * HEAD_DIM]
        if emit_aux and br >= 1:
            kb_ref = out_refs[N_BRANCH + 2 * (br - 1)]
            vt_ref = out_refs[N_BRANCH + 2 * (br - 1) + 1]
            if br == 1:
                pos = lax.rem(pl.program_id(0) * tm, seq) + lax.broadcasted_iota(jnp.int32, (tm, 1), 0)
                col = lax.broadcasted_iota(jnp.int32, (1, HEAD_DIM), 1)
                onehot = (col == (pos >> int(math.log2(SLC_BLOCK)))).astype(BF16)
            for h in range(N_KV_HEADS):
                kh = y[:, h * HEAD_DIM:(h + 1) * HEAD_DIM].astype(BF16)
                kb_ref[h] = jnp.concatenate([kh, onehot], axis=1) if br == 1 else kh
                vcol = (N_KV_HEADS + h) * HEAD_DIM
                for r in range(tm // PAGE_ROWS):
                    v = y[r * PAGE_ROWS:(r + 1) * PAGE_ROWS, vcol:vcol + HEAD_DIM]
                    vt_ref[h, r] = v.T.astype(BF16)


def _kv_proj(x, g, weights, emit_aux, seq):
    M, D = x.shape
    tm = min(M, 256)
    assert M % tm == 0 and (not emit_aux or (tm % PAGE_ROWS == 0 and seq % tm == 0
                                             and -(-seq // SLC_BLOCK) <= HEAD_DIM))
    out_specs = [pl.BlockSpec((tm * ROW_SPLIT, HEAD_DIM), lambda i: (i, 0))] * N_BRANCH
    out_shape = [jax.ShapeDtypeStruct((M * ROW_SPLIT, HEAD_DIM), F32)] * N_BRANCH
    if emit_aux:
        tiles = tm // PAGE_ROWS
        for kw in (2 * HEAD_DIM, HEAD_DIM):
            out_specs += [pl.BlockSpec((N_KV_HEADS, tm, kw), lambda i: (0, i, 0)),
                          pl.BlockSpec((N_KV_HEADS, tiles, HEAD_DIM, PAGE_ROWS), lambda i: (0, i, 0, 0))]
            out_shape += [jax.ShapeDtypeStruct((N_KV_HEADS, M, kw), BF16),
                          jax.ShapeDtypeStruct((N_KV_HEADS, M // PAGE_ROWS, HEAD_DIM, PAGE_ROWS), BF16)]
    return pl.pallas_call(
        functools.partial(_kv_proj_kernel, emit_aux=emit_aux, seq=seq),
        grid=(M // tm,),
        in_specs=[pl.BlockSpec((tm, D), lambda i: (i, 0)), pl.BlockSpec((1, D), lambda i: (0, 0))]
        + [pl.BlockSpec(w.shape, lambda i: (0, 0)) for w in weights],
        out_specs=out_specs,
        out_shape=out_shape,
        compiler_params=_params("parallel"),
        name="kv_proj",
    )(x, g.reshape(1, D), *weights)


def _q_proj_kernel(x_ref, g_ref, wq_ref, wg_ref, q_ref, gate_ref, *, transposed):
    tm = x_ref.shape[0]
    xn = (_rms(x_ref[...]) * g_ref[...]).astype(BF16)
    q = _dot(xn, wq_ref[...]) * (HEAD_DIM ** -0.5 * (LOG2E if transposed else 1.0))
    gates = jax.nn.sigmoid(_dot(xn, wg_ref[...]))
    if not transposed:
        q_ref[...] = q.astype(BF16)
        gate_ref[...] = gates
        return
    for r in range(tm // PAGE_ROWS):
        rows = slice(r * PAGE_ROWS, (r + 1) * PAGE_ROWS)
        gt = gates[rows].T
        for hk in range(N_KV_HEADS):
            gate_ref[hk, r] = gt[hk * GATE_PAD:(hk + 1) * GATE_PAD]
            for g in range(GQA_GROUP):
                col = (hk * GQA_GROUP + g) * HEAD_DIM
                q_ref[hk, r, :, g * PAGE_ROWS:(g + 1) * PAGE_ROWS] = q[rows, col:col + HEAD_DIM].T.astype(BF16)


def _q_proj(x, g, wq, wg, transposed):
    M, D = x.shape
    tm = min(M, 256)
    assert M % tm == 0 and (tm % PAGE_ROWS == 0 or not transposed)
    tiles = tm // PAGE_ROWS
    gw = GQA_GROUP * PAGE_ROWS
    if transposed:
        out_specs = [pl.BlockSpec((N_KV_HEADS, tiles, HEAD_DIM, gw), lambda i: (0, i, 0, 0)),
                     pl.BlockSpec((N_KV_HEADS, tiles, GATE_PAD, PAGE_ROWS), lambda i: (0, i, 0, 0))]
        out_shape = [jax.ShapeDtypeStruct((N_KV_HEADS, M // PAGE_ROWS, HEAD_DIM, gw), BF16),
                     jax.ShapeDtypeStruct((N_KV_HEADS, M // PAGE_ROWS, GATE_PAD, PAGE_ROWS), F32)]
    else:
        out_specs = [pl.BlockSpec((tm, wq.shape[1]), lambda i: (i, 0)),
                     pl.BlockSpec((tm, wg.shape[1]), lambda i: (i, 0))]
        out_shape = [jax.ShapeDtypeStruct((M, wq.shape[1]), BF16), jax.ShapeDtypeStruct((M, wg.shape[1]), F32)]
    return pl.pallas_call(
        functools.partial(_q_proj_kernel, transposed=transposed),
        grid=(M // tm,),
        in_specs=[pl.BlockSpec((tm, D), lambda i: (i, 0)), pl.BlockSpec((1, D), lambda i: (0, 0)),
                  pl.BlockSpec(wq.shape, lambda i: (0, 0)), pl.BlockSpec(wg.shape, lambda i: (0, 0))],
        out_specs=out_specs,
        out_shape=out_shape,
        compiler_params=_params("parallel"),
        name="q_proj",
    )(x, g.reshape(1, D), wq, wg)


def _gelu_tanh(x):
    return 0.5 * x * (1.0 + jnp.tanh(math.sqrt(2.0 / math.pi) * (x + 0.044715 * (x * x * x))))


def _cmp_bias_kernel(pos_ref, w1_ref, b1_ref, o_ref):
    for c in range(2):
        acc = jnp.zeros((8, HEAD_DIM), F32)
        for j in range(CMP_BLOCK):
            pj = jnp.broadcast_to(pos_ref[c, j:j + 1, :], (8, HEAD_DIM)).astype(BF16)
            acc = acc + _dot(pj, w1_ref[c, j])
        o_ref[c] = acc + b1_ref[c]


def _cmp_bias(pos, w1, b1):
    return pl.pallas_call(
        _cmp_bias_kernel,
        out_shape=jax.ShapeDtypeStruct((2, 8, HEAD_DIM), F32),
        compiler_params=pltpu.CompilerParams(vmem_limit_bytes=VMEM_LIMIT_BYTES),
        name="cmp_bias",
    )(pos, w1, b1)


def _compress_kernel(pt_ref, *refs, n_pages):
    del pt_ref
    page_refs = refs[:n_pages]
    w1_ref, bias_ref, w2_ref, o_ref, scr_ref, sum_ref = refs[n_pages:]
    sub_per_page = PAGE_ROWS // CMP_STRIDE
    n_sub = n_pages * sub_per_page
    sub_rows = CMP_STRIDE * ROW_SPLIT
    pages_per_chunk = 4 if n_pages % 4 == 0 else 1
    chunk_rows = pages_per_chunk * sub_per_page * ROW_SPLIT
    is_k = (lax.broadcasted_iota(jnp.int32, (chunk_rows, 1), 0) & (ROW_SPLIT - 1)) < N_KV_HEADS
    bias = jnp.where(is_k[0:ROW_SPLIT], bias_ref[0], bias_ref[1])
    bias = jnp.concatenate([bias] * (chunk_rows // ROW_SPLIT), axis=0)

    def by_side(x):
        return jnp.concatenate([jnp.where(is_k, x, 0.0), jnp.where(is_k, 0.0, x)], axis=1).astype(BF16)

    for ck in range(n_pages // pages_per_chunk):
        chunk_pages = page_refs[ck * pages_per_chunk:(ck + 1) * pages_per_chunk]
        acc = jnp.zeros((chunk_rows, 2 * HEAD_DIM), F32)
        for j in range(CMP_STRIDE):
            x = jnp.concatenate([r[n * sub_rows + j * ROW_SPLIT:n * sub_rows + (j + 1) * ROW_SPLIT, :]
                                 for r in chunk_pages for n in range(sub_per_page)], axis=0)
            acc = acc + _dot(by_side(x), w1_ref[j])
        scr_ref[ck * chunk_rows:(ck + 1) * chunk_rows, :] = acc
    total = n_sub * ROW_SPLIT
    for ck in range(n_pages // pages_per_chunk):
        rows = slice(ck * chunk_rows, (ck + 1) * chunk_rows)
        first = scr_ref[rows, 0:HEAD_DIM]
        nxt_rows = slice(ck * chunk_rows + ROW_SPLIT, min((ck + 1) * chunk_rows + ROW_SPLIT, total))
        nxt = scr_ref[nxt_rows, HEAD_DIM:2 * HEAD_DIM]
        if nxt.shape[0] < chunk_rows:
            nxt = jnp.concatenate([nxt, jnp.zeros((chunk_rows - nxt.shape[0], HEAD_DIM), F32)], axis=0)
        hid = _gelu_tanh(first + nxt + bias)
        out = _dot(by_side(hid), w2_ref[...])
        if (ck + 1) * chunk_rows == total:
            row = lax.broadcasted_iota(jnp.int32, (chunk_rows, 1), 0)
            out = jnp.where(row < chunk_rows - ROW_SPLIT, out, 0.0)
        sum_ref[rows, :] = out
    for ch in range(ROW_SPLIT):
        o_ref[0, :, ch * HEAD_DIM:(ch + 1) * HEAD_DIM] = sum_ref[pl.ds(ch, n_sub, stride=ROW_SPLIT), :]


def _compress(pages, table, w1cat, bias, w2cat):
    nb, n_pages = table.shape
    n_sub = n_pages * (PAGE_ROWS // CMP_STRIDE)
    page_rows2d = PAGE_ROWS * ROW_SPLIT
    page_specs = [pl.BlockSpec((page_rows2d, HEAD_DIM), functools.partial(lambda b, pt, p: (pt[b, p], 0), p=p))
                  for p in range(n_pages)]
    const = lambda shape: pl.BlockSpec(shape, lambda b, pt: (0,) * len(shape))
    grid_spec = pltpu.PrefetchScalarGridSpec(
        num_scalar_prefetch=1,
        grid=(nb,),
        in_specs=page_specs + [const(w1cat.shape), const(bias.shape), const(w2cat.shape)],
        out_specs=pl.BlockSpec((1, n_sub, KV_COLS), lambda b, pt: (b, 0, 0)),
        scratch_shapes=[pltpu.VMEM((n_sub * ROW_SPLIT, 2 * HEAD_DIM), F32),
                        pltpu.VMEM((n_sub * ROW_SPLIT, HEAD_DIM), F32)],
    )
    return pl.pallas_call(
        functools.partial(_compress_kernel, n_pages=n_pages),
        grid_spec=grid_spec,
        out_shape=jax.ShapeDtypeStruct((nb, n_sub, KV_COLS), F32),
        compiler_params=_params("parallel"),
        name="compress",
    )(table, *([pages] * n_pages), w1cat, bias, w2cat)


def _rel_bucket_np(dist):
    n = np.maximum(dist, 0)
    max_exact = N_BUCKETS // 2
    nf = np.maximum(n, 1).astype(np.float32)
    large = max_exact + (np.log(nf / np.float32(max_exact)) / np.float32(math.log(MAX_DISTANCE / max_exact))
                         * np.float32(N_BUCKETS - max_exact)).astype(np.int32)
    large = np.minimum(large, N_BUCKETS - 1)
    return np.where(n < max_exact, n, large).astype(np.int32)


def _bias_table_kernel(rb_ref, bk_ref, o_ref, *, minus_far, scale):
    bk = bk_ref[...]
    for h in range(N_HEADS):
        v = jnp.full(bk.shape, rb_ref[0, h], F32)
        for b in range(1, N_BUCKETS):
            v = jnp.where(bk == b, rb_ref[b, h], v)
        if minus_far:
            v = v - rb_ref[N_BUCKETS - 1, h]
        o_ref[h] = v if scale == 1.0 else v * scale


def _bias_table(rel_bias, bucket, minus_far=False, scale=1.0):
    R, C = bucket.shape
    tr = min(R, 128)
    assert R % tr == 0
    return pl.pallas_call(
        functools.partial(_bias_table_kernel, minus_far=minus_far, scale=scale),
        grid=(R // tr,),
        in_specs=[pl.BlockSpec(memory_space=pltpu.SMEM), pl.BlockSpec((tr, C), lambda r: (r, 0))],
        out_specs=pl.BlockSpec((N_HEADS, tr, C), lambda r: (0, r, 0)),
        out_shape=jax.ShapeDtypeStruct((N_HEADS, R, C), F32),
        compiler_params=_params("parallel"),
        name="bias_table",
    )(rel_bias, jnp.asarray(bucket))


def _overlap_np(n_sub, n_slc):
    n_cmp = n_sub - 1
    cs = np.arange(n_cmp) * CMP_STRIDE
    ce = cs + CMP_BLOCK - 1
    ss = np.arange(n_slc) * SLC_BLOCK
    se = ss + SLC_BLOCK - 1
    ov = np.minimum(ce[:, None], se[None, :]) - np.maximum(cs[:, None], ss[None, :]) + 1
    out = np.zeros((n_sub, n_slc), np.float32)
    out[:n_cmp] = np.maximum(ov, 0).astype(np.float32) / CMP_BLOCK
    return out


def _pad_to(a, shape):
    return np.pad(a, [(0, s - d) for d, s in zip(a.shape, shape)])


def _score(imp, j, qb, n_slc):
    prio = 2.0 * (j == qb).astype(F32) + (j == qb - 1).astype(F32) + (j == 0).astype(F32)
    score = jnp.where(j <= qb, imp + SEL_BIG * prio, -SEL_BIG)
    return jnp.where(j < n_slc, score, -3e38)


def _attn_prompt_kernel(qt_ref, gt_ref, ck_ref, cv_ref, sk_ref, svt_ref, wk_ref, wvt_ref,
                        toe_ref, cmpb_ref, ovt_ref, o_ref,
                        m_ref, l_ref, acc_ref, accw_ref, *, n_slc, n_sub, hps):
    tq = PAGE_ROWS
    i = pl.program_id(2)
    heads = range(hps)
    t_idx = lax.broadcasted_iota(jnp.int32, (1, tq), 1)
    qpos = i * tq + t_idx
    lane4 = lambda x: jnp.concatenate([x] * GQA_GROUP, axis=1)
    n_idx = lax.broadcasted_iota(jnp.int32, (n_sub, 1), 0)
    valid = lane4((n_idx * CMP_STRIDE + (CMP_BLOCK - 1) <= qpos) & (n_idx < n_sub - 1))
    nsp = ovt_ref.shape[0]
    j = lax.broadcasted_iota(jnp.int32, (nsp, 1), 0)
    qb = qpos >> int(math.log2(SLC_BLOCK))

    qts, qt_sels, o_cs = [], [], []
    for hh in heads:
        qt = qt_ref[hh, 0]
        cols = slice(hh * HEAD_DIM, (hh + 1) * HEAD_DIM)
        kc = ck_ref[0, :, cols].astype(BF16)
        vct = cv_ref[0, :, cols].T.astype(BF16)
        bias_c = jnp.concatenate([cmpb_ref[hh * GQA_GROUP + g, 0] for g in range(GQA_GROUP)], axis=1)
        logits = jnp.where(valid, _dot(kc, qt) + bias_c, NEG_INF)
        p = jnp.exp2(logits - jnp.max(logits, axis=0, keepdims=True))
        p = p * (1.0 / jnp.sum(p, axis=0, keepdims=True))
        p = jnp.where(valid, p, 0.0)
        o_cs.append(_dot(vct, p.astype(BF16)))
        psum = p[:, 0:tq]
        for g in range(1, GQA_GROUP):
            psum = psum + p[:, g * tq:(g + 1) * tq]
        imp = jnp.dot(ovt_ref[...], psum, precision=lax.Precision.HIGHEST, preferred_element_type=F32)
        score = _score(imp, j, qb, n_slc)
        rank = jnp.zeros((nsp, tq), jnp.int32)
        for jj in range(n_slc):
            row = score[jj:jj + 1, :]
            rank = rank + ((row > score) | ((row == score) & (jj < j))).astype(jnp.int32)
        madd = jnp.where(rank < min(N_SELECT, n_slc), 0.0, NEG_INF)
        qts.append(qt)
        qt_sels.append(jnp.concatenate([qt, lane4(madd).astype(BF16),
                                        jnp.zeros((HEAD_DIM - nsp, GQA_GROUP * tq), BF16)], axis=0))

    k_row = lax.broadcasted_iota(jnp.int32, (tq, 1), 0)

    def chunk(hh, k, kinds, k_ref, vt_ref, q_op, out_ref):
        nt = len(kinds)
        k0 = pl.multiple_of(k * tq, tq)
        s = _dot(k_ref[hh, pl.ds(k0, nt * tq), :], q_op)
        parts = []
        for jx, kind in enumerate(kinds):
            sj = s[jx * tq:(jx + 1) * tq]
            if kind in ("near", "diag"):
                d = 0 if kind == "diag" else 1
                sj = sj + jnp.concatenate([toe_ref[(hh * GQA_GROUP + g) * 2 + d] for g in range(GQA_GROUP)],
                                          axis=1)
            if kind == "diag":
                sj = jnp.where(lane4(k_row <= t_idx), sj, NEG_INF)
            elif kind == "edge":
                sj = jnp.where(lane4(k_row > t_idx), sj, NEG_INF)
            parts.append(sj)
        m_new = functools.reduce(jnp.maximum, parts)
        m_new = jnp.max(m_new, axis=0, keepdims=True)
        if out_ref is None:
            m_old = m_ref[hh]
            m_new = jnp.maximum(m_old, m_new)
            alpha = jnp.exp2(m_old - m_new)
        ps = [jnp.exp2(sj - m_new) for sj in parts]
        lsum = jnp.sum(functools.reduce(jnp.add, ps), axis=0, keepdims=True)
        vt = jnp.concatenate([vt_ref[hh, k + jx] for jx in range(nt)], axis=1)
        pv = _dot(vt, jnp.concatenate(ps, axis=0).astype(BF16))
        if out_ref is None:
            l_ref[hh] = alpha * l_ref[hh] + lsum
            acc_ref[hh] = alpha * acc_ref[hh] + pv
            m_ref[hh] = m_new
        else:
            out_ref[hh] = pv * (1.0 / lsum)

    def tail_kinds(n_far):
        return ("far",) * n_far + ("near", "diag")

    far_step = SEL_FAR_TILES
    wt = WINDOW // tq
    m_ref[...] = jnp.full(m_ref.shape, NEG_INF, F32)
    l_ref[...] = jnp.zeros(l_ref.shape, F32)
    acc_ref[...] = jnp.zeros(acc_ref.shape, F32)

    n_far = jnp.maximum(i - 1, 0) // far_step

    def far_body(c, carry):
        for hh in heads:
            chunk(hh, c * far_step, ("far",) * far_step, sk_ref, svt_ref, qt_sels[hh], None)
        return carry

    lax.fori_loop(0, n_far, far_body, 0)

    for c in range(wt):
        @pl.when(i == c)
        def _(c=c):
            kinds = tail_kinds(c - 1) if c >= 1 else ("diag",)
            for hh in heads:
                chunk(hh, 0, kinds, sk_ref, svt_ref, qt_sels[hh], None)
                chunk(hh, 0, kinds, wk_ref, wvt_ref, qts[hh], accw_ref)

    for r in range(far_step):
        @pl.when((i >= wt) & (lax.rem(i + far_step - 1, far_step) == r))
        def _(r=r):
            for hh in heads:
                chunk(hh, i - 1 - r, tail_kinds(r), sk_ref, svt_ref, qt_sels[hh], None)
                chunk(hh, i - wt, ("edge",) + tail_kinds(wt - 2), wk_ref, wvt_ref, qts[hh], accw_ref)

    for hh in heads:
        o_s = acc_ref[hh] * (1.0 / l_ref[hh])
        o_w = accw_ref[hh]
        o_c = o_cs[hh]
        gt = gt_ref[hh, 0]
        for g in range(GQA_GROUP):
            lanes = slice(g * tq, (g + 1) * tq)
            o = (gt[g:g + 1] * o_c[:, lanes] + gt[GQA_GROUP + g:GQA_GROUP + g + 1] * o_s[:, lanes]
                 + gt[2 * GQA_GROUP + g:2 * GQA_GROUP + g + 1] * o_w[:, lanes])
            col = (hh * GQA_GROUP + g) * HEAD_DIM
            o_ref[0, :, col:col + HEAD_DIM] = o.T.astype(o_ref.dtype)


def _attn_prompt(qt, gt, ckv, sk, svt, wk, wvt, toe, cmpb, B, S):
    tq = PAGE_ROWS
    nq = S // tq
    n_sub = ckv.shape[1]
    n_slc = -(-S // SLC_BLOCK)
    nsp = -(-n_slc // 16) * 16
    assert S % tq == 0 and nsp <= HEAD_DIM and nq > WINDOW // tq
    ovt = jnp.asarray(_pad_to(_overlap_np(n_sub, n_slc).T, (nsp, n_sub)))
    gw = GQA_GROUP * tq
    hps = PROMPT_HEADS_PER_STEP
    assert N_KV_HEADS % hps == 0
    return pl.pallas_call(
        functools.partial(_attn_prompt_kernel, n_slc=n_slc, n_sub=n_sub, hps=hps),
        grid=(B, N_KV_HEADS // hps, nq),
        in_specs=[pl.BlockSpec((hps, 1, HEAD_DIM, gw), lambda b, h, i: (h, b * nq + i, 0, 0)),
                  pl.BlockSpec((hps, 1, GATE_PAD, tq), lambda b, h, i: (h, b * nq + i, 0, 0)),
                  pl.BlockSpec((1, n_sub, hps * HEAD_DIM), lambda b, h, i: (b, 0, h)),
                  pl.BlockSpec((1, n_sub, hps * HEAD_DIM), lambda b, h, i: (b, 0, N_KV_HEADS // hps + h)),
                  pl.BlockSpec((hps, S, 2 * HEAD_DIM), lambda b, h, i: (h, b, 0)),
                  pl.BlockSpec((hps, nq, HEAD_DIM, tq), lambda b, h, i: (h, b, 0, 0)),
                  pl.BlockSpec((hps, S, HEAD_DIM), lambda b, h, i: (h, b, 0)),
                  pl.BlockSpec((hps, nq, HEAD_DIM, tq), lambda b, h, i: (h, b, 0, 0)),
                  pl.BlockSpec((hps * GQA_GROUP * 2, tq, tq), lambda b, h, i: (h, 0, 0)),
                  pl.BlockSpec((hps * GQA_GROUP, 1, n_sub, tq), lambda b, h, i: (h, i, 0, 0)),
                  pl.BlockSpec(ovt.shape, lambda b, h, i: (0, 0))],
        out_specs=pl.BlockSpec((1, tq, hps * GQA_GROUP * HEAD_DIM), lambda b, h, i: (b, i, h)),
        out_shape=jax.ShapeDtypeStruct((B, S, N_HEADS * HEAD_DIM), BF16),
        scratch_shapes=[pltpu.VMEM((hps, 1, gw), F32), pltpu.VMEM((hps, 1, gw), F32),
                        pltpu.VMEM((hps, HEAD_DIM, gw), F32), pltpu.VMEM((hps, HEAD_DIM, gw), F32)],
        compiler_params=_params("parallel", "parallel", "arbitrary"),
        name="attn_prompt",
    )(qt, gt, ckv, ckv, sk, svt, wk, wvt, toe, cmpb, ovt)


def _attn_sample_kernel(pt_ref, cache_ref, wst_ref, wst2d_ref, q_ref, gate_ref, ckv_ref, snew_ref, wnew_ref,
                        bias_ref, ov_ref, e_ref, o_ref, wout_ref, kvs_ref, kvw_ref, sem_s, sem_w, sem_o,
                        *, n_pages, n_slc, n_sub, past_len, win_buf, ds):
    b = pl.program_id(0)
    slot = lax.rem(b, 2)
    n_past = n_pages * PAGE_ROWS
    nk_s = n_past + PAGE_ROWS
    nk_w = win_buf + PAGE_ROWS

    def fetches(bb, sl):
        cps = []
        for p in range(n_pages):
            row0 = pt_ref[bb, p] * PAGE_ROWS
            for ch in range(ROW_SPLIT):
                cps.append(pltpu.make_async_copy(cache_ref.at[pl.ds(row0, PAGE_ROWS), ch, 0],
                                                 kvs_ref.at[sl, ch, pl.ds(p * PAGE_ROWS, PAGE_ROWS)], sem_s.at[sl]))
        for ch in range(ROW_SPLIT):
            cps.append(pltpu.make_async_copy(wst_ref.at[pl.ds(bb * win_buf, win_buf), ch, 0],
                                             kvw_ref.at[sl, ch], sem_w.at[sl]))
        return cps

    @pl.when(b == 0)
    def _():
        for cp in fetches(0, 0):
            cp.start()

    @pl.when(b + 1 < pl.num_programs(0))
    def _():
        for cp in fetches(b + 1, 1 - slot):
            cp.start()

    state_rows = win_buf * ROW_SPLIT
    shift = ds * ROW_SPLIT
    keep = pltpu.make_async_copy(wst2d_ref.at[pl.ds(b * state_rows + shift, state_rows - shift)],
                                 wout_ref.at[pl.ds(b * state_rows, state_rows - shift)], sem_o.at[0])
    append = pltpu.make_async_copy(wnew_ref, wout_ref.at[pl.ds((b + 1) * state_rows - shift, shift)], sem_o.at[1])
    keep.start()
    append.start()

    for cp in fetches(b, slot):
        cp.wait()

    t_idx = lax.broadcasted_iota(jnp.int32, (ds, 1), 0)
    qpos = past_len + t_idx
    pad = jnp.zeros((PAGE_ROWS - ds, HEAD_DIM), F32)
    gates = gate_ref[0]
    kv_heads = range(N_KV_HEADS)
    head_rows = GQA_GROUP * ds
    per_head = lambda x, reps: jnp.concatenate([x] * reps, axis=0)
    new_rows = lambda r, col: jnp.concatenate([r[pl.ds(col, ds, stride=ROW_SPLIT), :], pad], axis=0).astype(BF16)

    ks_pos = lax.broadcasted_iota(jnp.int32, (1, nk_s), 1)
    dist_w = win_buf + t_idx - lax.broadcasted_iota(jnp.int32, (1, nk_w), 1)
    ok_w = per_head((dist_w >= 0) & (dist_w < WINDOW), N_HEADS)
    n_idx = lax.broadcasted_iota(jnp.int32, (1, n_sub), 1)
    ok_c = per_head((n_idx * CMP_STRIDE + (CMP_BLOCK - 1) <= qpos) & (n_idx < n_sub - 1), N_HEADS)
    lanes = ov_ref.shape[1]
    j = lax.broadcasted_iota(jnp.int32, (1, lanes), 1)
    qb = per_head(qpos >> int(math.log2(SLC_BLOCK)), N_KV_HEADS)

    def softmax(logits, ok):
        logits = jnp.where(ok, logits, NEG_INF)
        p = jnp.exp(logits - jnp.max(logits, axis=-1, keepdims=True))
        return p / jnp.sum(p, axis=-1, keepdims=True)

    bias = bias_ref[...].reshape(N_HEADS * ds, bias_ref.shape[2])
    qs = [jnp.concatenate([q_ref[0, :, (hk * GQA_GROUP + g) * HEAD_DIM:(hk * GQA_GROUP + g + 1) * HEAD_DIM]
                           for g in range(GQA_GROUP)], axis=0) for hk in kv_heads]
    kcols = [slice(hk * HEAD_DIM, (hk + 1) * HEAD_DIM) for hk in kv_heads]
    vcols = [slice((N_KV_HEADS + hk) * HEAD_DIM, (N_KV_HEADS + hk + 1) * HEAD_DIM) for hk in kv_heads]

    def attend(k_past, v_past, k_new, v_new, bias_cols, ok):
        n = k_past(0).shape[0]
        logits = jnp.concatenate(
            [jnp.concatenate([_dot_nt(qs[hk], k_past(hk).astype(BF16)), _dot_nt(qs[hk], k_new(hk))], axis=1)
             for hk in kv_heads], axis=0) + bias_cols
        p = softmax(logits, ok).astype(BF16)
        return [_dot(p[hk * head_rows:(hk + 1) * head_rows, 0:n], v_past(hk).astype(BF16))
                + _dot(p[hk * head_rows:(hk + 1) * head_rows, n:], v_new(hk)) for hk in kv_heads]

    logits_c = jnp.concatenate([_dot_nt(qs[hk], ckv_ref[0, :, kcols[hk]].astype(BF16)) for hk in kv_heads], axis=0)
    p_c = jnp.where(ok_c, softmax(logits_c + bias[:, nk_s + nk_w:nk_s + nk_w + n_sub], ok_c), 0.0)
    o_c = [_dot(p_c[hk * head_rows:(hk + 1) * head_rows].astype(BF16), ckv_ref[0, :, vcols[hk]].astype(BF16))
           for hk in kv_heads]
    psum = jnp.concatenate(
        [functools.reduce(jnp.add, [p_c[(hk * GQA_GROUP + g) * ds:(hk * GQA_GROUP + g + 1) * ds]
                                    for g in range(GQA_GROUP)]) for hk in kv_heads], axis=0)
    imp = jnp.dot(psum, ov_ref[...], precision=lax.Precision.HIGHEST, preferred_element_type=F32)
    score = _score(imp, j, qb, n_slc)
    rank = jnp.zeros(score.shape, jnp.int32)
    for jj in range(n_slc):
        col = score[:, jj:jj + 1]
        rank = rank + ((col > score) | ((col == score) & (jj < j))).astype(jnp.int32)
    sel = ((rank < min(N_SELECT, n_slc)) & (j < n_slc)).astype(BF16)
    allowed = (_dot(sel, e_ref[...]) > 0.5) & (ks_pos <= per_head(qpos, N_KV_HEADS))
    ok_s = jnp.concatenate([per_head(allowed[hk * ds:(hk + 1) * ds], GQA_GROUP) for hk in kv_heads], axis=0)

    o_s = attend(lambda hk: kvs_ref[slot, hk], lambda hk: kvs_ref[slot, N_KV_HEADS + hk],
                 lambda hk: new_rows(snew_ref, hk), lambda hk: new_rows(snew_ref, N_KV_HEADS + hk),
                 bias[:, 0:nk_s], ok_s)
    o_w = attend(lambda hk: kvw_ref[slot, hk], lambda hk: kvw_ref[slot, N_KV_HEADS + hk],
                 lambda hk: new_rows(wnew_ref, hk), lambda hk: new_rows(wnew_ref, N_KV_HEADS + hk),
                 bias[:, nk_s:nk_s + nk_w], ok_w)

    for hk in kv_heads:
        gcol = hk * GATE_PAD
        for g in range(GQA_GROUP):
            rows = slice(g * ds, (g + 1) * ds)
            gate = lambda br, g=g: gates[:, gcol + br * GQA_GROUP + g:gcol + br * GQA_GROUP + g + 1]
            o = gate(0) * o_c[hk][rows] + gate(1) * o_s[hk][rows] + gate(2) * o_w[hk][rows]
            col = (hk * GQA_GROUP + g) * HEAD_DIM
            o_ref[0, :, col:col + HEAD_DIM] = o.astype(o_ref.dtype)

    keep.wait()
    append.wait()


def _attn_sample(q, gates, ckv, cache_slc, table, slc_new, win_state, win_new, bias, past_len, win_buf):
    DB, ds, _ = q.shape
    n_pages = table.shape[1]
    n_sub = ckv.shape[1]
    n_slc = -(-(past_len + ds) // SLC_BLOCK)
    lanes = PAGE_ROWS
    assert ds % 8 == 0 and ds <= PAGE_ROWS and n_slc <= lanes and n_sub <= lanes
    assert win_buf <= past_len and win_buf > ds
    nk_s = (n_pages + 1) * PAGE_ROWS
    ov = jnp.asarray(_pad_to(_overlap_np(n_sub, n_slc), (n_sub, lanes)))
    e = (np.arange(nk_s)[None, :] // SLC_BLOCK == np.arange(lanes)[:, None]) & (np.arange(lanes)[:, None] < n_slc)
    e = jnp.asarray(e.astype(np.float32), dtype=BF16)
    by_col = lambda a: a.reshape(-1, ROW_SPLIT, 1, HEAD_DIM)
    rows2d = lambda a: a.reshape(-1, HEAD_DIM)
    per_b3 = lambda shape: pl.BlockSpec((1,) + shape[1:], lambda b, pt: (b, 0, 0))
    rows_b = lambda n: pl.BlockSpec((n * ROW_SPLIT, HEAD_DIM), lambda b, pt: (b, 0))
    const = lambda shape: pl.BlockSpec(shape, lambda b, pt: (0,) * len(shape))
    hbm = pl.BlockSpec(memory_space=pl.ANY)
    grid_spec = pltpu.PrefetchScalarGridSpec(
        num_scalar_prefetch=1,
        grid=(DB,),
        in_specs=[hbm, hbm, hbm, per_b3(q.shape), per_b3(gates.shape), per_b3(ckv.shape), rows_b(ds), rows_b(ds),
                  const(bias.shape), const(ov.shape), const(e.shape)],
        out_specs=[per_b3(q.shape), hbm],
        scratch_shapes=[pltpu.VMEM((2, ROW_SPLIT, n_pages * PAGE_ROWS, HEAD_DIM), F32),
                        pltpu.VMEM((2, ROW_SPLIT, win_buf, HEAD_DIM), F32),
                        pltpu.SemaphoreType.DMA((2,)), pltpu.SemaphoreType.DMA((2,)),
                        pltpu.SemaphoreType.DMA((2,))],
    )
    return pl.pallas_call(
        functools.partial(_attn_sample_kernel, n_pages=n_pages, n_slc=n_slc, n_sub=n_sub,
                          past_len=past_len, win_buf=win_buf, ds=ds),
        grid_spec=grid_spec,
        out_shape=[jax.ShapeDtypeStruct(q.shape, BF16),
                   jax.ShapeDtypeStruct((DB * win_buf * ROW_SPLIT, HEAD_DIM), F32)],
        compiler_params=_params("arbitrary"),
        name="attn_sample",
    )(table, by_col(cache_slc), by_col(win_state), rows2d(win_state), q, gates, ckv, slc_new, win_new, bias, ov, e)


def _prompt_buckets(S, n_sub):
    tq = PAGE_ROWS
    k = np.arange(tq)[:, None]
    t = np.arange(tq)[None, :]
    toe = np.concatenate([_rel_bucket_np(t - k + d * tq) for d in range(2)], axis=0)
    kpos_c = np.arange(n_sub)[:, None] * CMP_STRIDE + CMP_BLOCK - 1
    cmp = np.concatenate([_rel_bucket_np(i * tq + t - kpos_c) for i in range(S // tq)], axis=0)
    return toe, cmp


def _sample_buckets(ds, past_len, n_pages, win_buf, n_sub):
    qpos = past_len + np.arange(ds)[:, None]
    ks = np.arange((n_pages + 1) * PAGE_ROWS)[None, :]
    kw = past_len - win_buf + np.arange(win_buf + PAGE_ROWS)[None, :]
    kc = np.arange(n_sub)[None, :] * CMP_STRIDE + CMP_BLOCK - 1
    return np.concatenate([_rel_bucket_np(qpos - ks), _rel_bucket_np(qpos - kw), _rel_bucket_np(qpos - kc)], axis=1)


def kernel(x_prompt, x_sample, state_pool, cache_cmp_kv, cache_slc_kv, state_win_kv, page_table, norms, w_pool,
           pool_scale, kv_norm, w_kv, cmp_pos, cmp_w1, cmp_b1, cmp_w2, w_q, w_o, rel_bias, w_ffn_in, w_ffn_out):
    B, S, D = x_prompt.shape
    DB, DS, _ = x_sample.shape
    n_pages = page_table.shape[1]
    past_len = n_pages * cache_slc_kv.shape[1]
    win_buf = state_win_kv.shape[1]
    assert norms.shape[0] == 2 and w_pool.shape[0] == 1 and w_q.shape[0] == 1, "one pooling + one attention layer"
    assert cache_slc_kv.shape[1] == PAGE_ROWS and S % PAGE_ROWS == 0
    assert past_len % CMP_STRIDE == 0 and DS < CMP_STRIDE and D == N_HEADS * HEAD_DIM
    assert S >= win_buf and N_BRANCH * GQA_GROUP <= GATE_PAD

    qdim = N_HEADS * HEAD_DIM
    w_pool_b = w_pool[0].astype(BF16)
    w_in_b = w_ffn_in.astype(BF16)
    w_out_b = w_ffn_out.astype(BF16)
    w_kv_b = [w_kv[:, k * KV_COLS:(k + 1) * KV_COLS].astype(BF16) for k in range(N_BRANCH)]
    w_qq = w_q[0, :, :qdim].astype(BF16)
    w_qg = w_q[0, :, qdim:].reshape(D, N_BRANCH, N_KV_HEADS, GQA_GROUP).transpose(0, 2, 1, 3)
    w_qg = jnp.pad(w_qg.reshape(D, N_KV_HEADS, N_BRANCH * GQA_GROUP),
                   ((0, 0), (0, 0), (0, GATE_PAD - N_BRANCH * GQA_GROUP)))
    w_qg = jnp.pad(w_qg.reshape(D, N_KV_HEADS * GATE_PAD), ((0, 0), (0, 128 - N_KV_HEADS * GATE_PAD))).astype(BF16)
    w_o_b = w_o.astype(BF16)
    w1_b = cmp_w1.astype(BF16)
    w1cat = w1_b.reshape(2, 2, CMP_STRIDE, HEAD_DIM, HEAD_DIM).transpose(2, 0, 3, 1, 4)
    w1cat = w1cat.reshape(CMP_STRIDE, 2 * HEAD_DIM, 2 * HEAD_DIM)
    w2cat = cmp_w2.astype(BF16).reshape(2 * HEAD_DIM, HEAD_DIM)
    cmp_bias = _cmp_bias(cmp_pos, w1_b, cmp_b1.reshape(2, 1, HEAD_DIM))

    def layer0(x, buf, pos0):
        Bx, Sx, _ = x.shape
        M = Bx * Sx
        mix, st = _pool_mix(x, buf, norms[0, 0], pos0)
        h = _proj_res(mix.reshape(M, D), w_pool_b, pool_scale[0], norms[0, 1], x.reshape(M, D), "pool_proj")
        return _ffn(h, norms[0, 2], norms[0, 3], w_in_b[0], w_out_b[0]), st

    def layer1_tail(h1, o):
        h2 = _proj_res(o, w_o_b, None, norms[1, 1], h1, "o_proj")
        return _ffn(h2, norms[1, 2], norms[1, 3], w_in_b[1], w_out_b[1])

    rows2d = lambda a: a.reshape(-1, HEAD_DIM)
    kv5 = lambda r, b, s: r.reshape(b, s, 2, N_KV_HEADS, HEAD_DIM)

    h1, st_p = layer0(x_prompt, jnp.zeros((B, POOL_BUF, D), F32), 0)
    cmp_p, slc_p, win_p, sk, svt, wk, wvt = _kv_proj(h1, kv_norm, w_kv_b, True, S)
    qt, gt = _q_proj(h1, norms[1, 0], w_qq, w_qg, True)
    pages_p = S // PAGE_ROWS
    ckv_p = _compress(cmp_p, jnp.arange(B * pages_p, dtype=jnp.int32).reshape(B, pages_p), w1cat, cmp_bias, w2cat)
    n_sub_p = ckv_p.shape[1]
    toe_bk, cmp_bk = _prompt_buckets(S, n_sub_p)
    toe = _bias_table(rel_bias, toe_bk, True, LOG2E).reshape(N_HEADS * 2, PAGE_ROWS, PAGE_ROWS)
    cmpb = _bias_table(rel_bias, cmp_bk, False, LOG2E).reshape(N_HEADS, pages_p, n_sub_p, PAGE_ROWS)
    o_p = _attn_prompt(qt, gt, ckv_p, sk, svt, wk, wvt, toe, cmpb, B, S)
    y_prompt = layer1_tail(h1, o_p.reshape(B * S, qdim)).reshape(B, S, D)
    win_state_p = win_p.reshape(B, S * ROW_SPLIT, HEAD_DIM)[:, (S - win_buf) * ROW_SPLIT:]

    h1s, st_s = layer0(x_sample, state_pool[0], past_len)
    cmp_s, slc_s, win_s = _kv_proj(h1s, kv_norm, w_kv_b, False, DS)
    q_s, gates_s = _q_proj(h1s, norms[1, 0], w_qq, w_qg, False)
    ckv_s = _compress(rows2d(cache_cmp_kv), page_table, w1cat, cmp_bias, w2cat)
    bias_s = _bias_table(rel_bias, _sample_buckets(DS, past_len, n_pages, win_buf, ckv_s.shape[1]))
    o_s, win_state_s = _attn_sample(
        q_s.reshape(DB, DS, qdim), gates_s.reshape(DB, DS, 128), ckv_s, cache_slc_kv, page_table,
        slc_s, state_win_kv, win_s, bias_s, past_len, win_buf)
    y_sample = layer1_tail(h1s, o_s.reshape(DB * DS, qdim)).reshape(DB, DS, D)

    return (y_prompt, y_sample, st_p[None], st_s[None],
            kv5(cmp_p, B, S), kv5(cmp_s, DB, DS), kv5(slc_p, B, S), kv5(slc_s, DB, DS),
            kv5(win_state_p, B, win_buf), kv5(win_state_s, DB, win_buf))
```

```python
import functools
import math

import numpy as np
import jax
import jax.numpy as jnp
from jax import lax
from jax.experimental import pallas as pl
from jax.experimental.pallas import tpu as pltpu

F32 = jnp.float32
BF16 = jnp.bfloat16

HEAD_DIM = 128
N_KV_HEADS = 4
GQA_GROUP = 4
N_HEADS = N_KV_HEADS * GQA_GROUP
N_BRANCH = 3
POOL_WINDOWS = (2, 4, 8, 16)
POOL_BUF = max(POOL_WINDOWS) - 1
POOL_HALO = POOL_BUF + 1
CMP_BLOCK = 32
CMP_STRIDE = 16
SLC_BLOCK = 64
N_SELECT = 16
WINDOW = 512
N_BUCKETS = 32
MAX_DISTANCE = 128
RMS_EPS = 1e-6
SEL_BIG = 1e3
NEG_INF = -1e30
LOG2E = math.log2(math.e)
PAGE_ROWS = 128
KV_COLS = 2 * N_KV_HEADS * HEAD_DIM
ROW_SPLIT = 2 * N_KV_HEADS
GATE_PAD = 16
SEL_FAR_TILES = 4
PROMPT_HEADS_PER_STEP = 4

VMEM_LIMIT_BYTES = 56 * 1024 * 1024

_NT = (((1,), (1,)), ((), ()))


def _params(*sem):
    return pltpu.CompilerParams(dimension_semantics=sem, vmem_limit_bytes=VMEM_LIMIT_BYTES)


def _rms(x):
    return x * lax.rsqrt(jnp.mean(x * x, axis=-1, keepdims=True) + RMS_EPS)


def _dot(a, b):
    return jnp.dot(a, b, preferred_element_type=F32)


def _dot_nt(a, b):
    return lax.dot_general(a, b, _NT, preferred_element_type=F32)


def _pool_mix_kernel(h_ref, buf_ref, g_ref, mix_ref, st_ref, ext_ref, *, ts, pos0, group_dim):
    i = pl.program_id(1)

    @pl.when(i == 0)
    def _():
        ext_ref[0:POOL_HALO, :] = buf_ref[0]

    @pl.when(i > 0)
    def _():
        ext_ref[0:POOL_HALO, :] = ext_ref[ts:ts + POOL_HALO, :]

    x = h_ref[0]
    ext_ref[POOL_HALO:POOL_HALO + ts, :] = _rms(x) * g_ref[...]
    pos = pos0 + i * ts + lax.broadcasted_iota(jnp.int32, (ts, 1), 0)
    for gi, w in enumerate(POOL_WINDOWS):
        cols = slice(gi * group_dim, (gi + 1) * group_dim)
        cur = ext_ref[POOL_HALO:POOL_HALO + ts, cols]
        s = cur
        for u in range(1, w):
            s = s + ext_ref[POOL_HALO - u:POOL_HALO - u + ts, cols]
        cnt = jnp.minimum(w, pos + 1).astype(F32)
        mix_ref[0, :, cols] = (s / cnt - cur).astype(mix_ref.dtype)
    st_ref[0] = ext_ref[ts:ts + POOL_HALO, :]


def _pool_mix(h, buf, g, pos0):
    B, S, D = h.shape
    ts = min(S, 256)
    assert S % ts == 0 and (ts >= POOL_HALO or S == ts) and ts % 8 == 0
    buf16 = jnp.pad(buf, ((0, 0), (POOL_HALO - POOL_BUF, 0), (0, 0)))
    mix, st = pl.pallas_call(
        functools.partial(_pool_mix_kernel, ts=ts, pos0=pos0, group_dim=D // len(POOL_WINDOWS)),
        grid=(B, S // ts),
        in_specs=[pl.BlockSpec((1, ts, D), lambda b, i: (b, i, 0)),
                  pl.BlockSpec((1, POOL_HALO, D), lambda b, i: (b, 0, 0)),
                  pl.BlockSpec((1, D), lambda b, i: (0, 0))],
        out_specs=[pl.BlockSpec((1, ts, D), lambda b, i: (b, i, 0)),
                   pl.BlockSpec((1, POOL_HALO, D), lambda b, i: (b, 0, 0))],
        out_shape=[jax.ShapeDtypeStruct((B, S, D), BF16),
                   jax.ShapeDtypeStruct((B, POOL_HALO, D), F32)],
        scratch_shapes=[pltpu.VMEM((POOL_HALO + ts, D), F32)],
        compiler_params=_params("parallel", "arbitrary"),
        name="pool_mix",
    )(h, buf16, g.reshape(1, D))
    return mix, st[:, POOL_HALO - POOL_BUF:]


def _proj_res_kernel(*refs, groups, has_scale):
    if has_scale:
        a_ref, w_ref, scale_ref, g_ref, res_ref, o_ref = refs
    else:
        a_ref, w_ref, g_ref, res_ref, o_ref = refs
    gd = a_ref.shape[1] // groups
    parts = [_dot(a_ref[:, gi * gd:(gi + 1) * gd], w_ref[gi]) for gi in range(groups)]
    y = parts[0] if groups == 1 else jnp.concatenate(parts, axis=-1)
    if has_scale:
        y = y * scale_ref[...]
    o_ref[...] = res_ref[...] + _rms(y) * g_ref[...]


def _proj_res(a, w, scale, g, res, name):
    M, D = a.shape
    groups = w.shape[0]
    tm = min(M, 256)
    assert M % tm == 0
    row = lambda i: (i, 0)
    fixed = lambda i: (0, 0)
    in_specs = [pl.BlockSpec((tm, D), row), pl.BlockSpec(w.shape, lambda i: (0, 0, 0))]
    args = [a, w]
    if scale is not None:
        in_specs.append(pl.BlockSpec((1, D), fixed))
        args.append(scale.reshape(1, D))
    in_specs += [pl.BlockSpec((1, D), fixed), pl.BlockSpec((tm, D), row)]
    args += [g.reshape(1, D), res]
    return pl.pallas_call(
        functools.partial(_proj_res_kernel, groups=groups, has_scale=scale is not None),
        grid=(M // tm,),
        in_specs=in_specs,
        out_specs=pl.BlockSpec((tm, D), row),
        out_shape=jax.ShapeDtypeStruct((M, D), F32),
        compiler_params=_params("parallel"),
        name=name,
    )(*args)


def _ffn_kernel(h_ref, gi_ref, go_ref, wg_ref, wu_ref, wo_ref, out_ref, xn_ref, acc_ref):
    f = pl.program_id(1)

    @pl.when(f == 0)
    def _():
        xn_ref[...] = (_rms(h_ref[...]) * gi_ref[...]).astype(xn_ref.dtype)
        acc_ref[...] = jnp.zeros_like(acc_ref)

    xn = xn_ref[...]
    gate = _dot(xn, wg_ref[...])
    up = _dot(xn, wu_ref[...])
    act = (gate * jax.nn.sigmoid(gate) * up).astype(BF16)
    acc_ref[...] += _dot(act, wo_ref[...])

    @pl.when(f == pl.num_programs(1) - 1)
    def _():
        out_ref[...] = h_ref[...] + _rms(acc_ref[...]) * go_ref[...]


def _ffn(h, g_in, g_out, w_in, w_out):
    M, D = h.shape
    F = w_out.shape[0]
    tm = min(M, 512)
    tf = 512
    assert M % tm == 0 and F % tf == 0
    nf = F // tf
    return pl.pallas_call(
        _ffn_kernel,
        grid=(M // tm, nf),
        in_specs=[pl.BlockSpec((tm, D), lambda i, f: (i, 0)),
                  pl.BlockSpec((1, D), lambda i, f: (0, 0)),
                  pl.BlockSpec((1, D), lambda i, f: (0, 0)),
                  pl.BlockSpec((D, tf), lambda i, f: (0, f)),
                  pl.BlockSpec((D, tf), lambda i, f: (0, nf + f)),
                  pl.BlockSpec((tf, D), lambda i, f: (f, 0))],
        out_specs=pl.BlockSpec((tm, D), lambda i, f: (i, 0)),
        out_shape=jax.ShapeDtypeStruct((M, D), F32),
        scratch_shapes=[pltpu.VMEM((tm, D), BF16), pltpu.VMEM((tm, D), F32)],
        compiler_params=_params("parallel", "arbitrary"),
        name="ffn",
    )(h, g_in.reshape(1, D), g_out.reshape(1, D), w_in, w_in, w_out)


def _kv_proj_kernel(x_ref, g_ref, *refs, emit_aux, seq):
    w_refs, out_refs = refs[:N_BRANCH], refs[N_BRANCH:]
    tm = x_ref.shape[0]
    xn = (_rms(x_ref[...]) * g_ref[...]).astype(BF16)
    for br in range(N_BRANCH):
        y = _dot(xn, w_refs[br][...])
        for j in range(ROW_SPLIT):
            out_refs[br][pl.ds(j, tm, stride=ROW_SPLIT), :] = y[:, j * HEAD_DIM:(j + 1) * HEAD_DIM]
        if emit_aux and br >= 1:
            kb_ref = out_refs[N_BRANCH + 2 * (br - 1)]
            vt_ref = out_refs[N_BRANCH + 2 * (br - 1) + 1]
            if br == 1:
                pos = lax.rem(pl.program_id(0) * tm, seq) + lax.broadcasted_iota(jnp.int32, (tm, 1), 0)
                col = lax.broadcasted_iota(jnp.int32, (1, HEAD_DIM), 1)
                onehot = (col == (pos >> int(math.log2(SLC_BLOCK)))).astype(BF16)
            for h in range(N_KV_HEADS):
                kh = y[:, h * HEAD_DIM:(h + 1) * HEAD_DIM].astype(BF16)
                kb_ref[h] = jnp.concatenate([kh, onehot], axis=1) if br == 1 else kh
                vcol = (N_KV_HEADS + h) * HEAD_DIM
                for r in range(tm // PAGE_ROWS):
                    v = y[r * PAGE_ROWS:(r + 1) * PAGE_ROWS, vcol:vcol + HEAD_DIM]
                    vt_ref[h, r] = v.T.astype(BF16)


def _kv_proj(x, g, weights, emit_aux, seq):
    M, D = x.shape
    tm = min(M, 256)
    assert M % tm == 0 and (not emit_aux or (tm % PAGE_ROWS == 0 and seq % tm == 0
                                             and -(-seq // SLC_BLOCK) <= HEAD_DIM))
    out_specs = [pl.BlockSpec((tm * ROW_SPLIT, HEAD_DIM), lambda i: (i, 0))] * N_BRANCH
    out_shape = [jax.ShapeDtypeStruct((M * ROW_SPLIT, HEAD_DIM), F32)] * N_BRANCH
    if emit_aux:
        tiles = tm // PAGE_ROWS
        for kw in (2 * HEAD_DIM, HEAD_DIM):
            out_specs += [pl.BlockSpec((N_KV_HEADS, tm, kw), lambda i: (0, i, 0)),
                          pl.BlockSpec((N_KV_HEADS, tiles, HEAD_DIM, PAGE_ROWS), lambda i: (0, i, 0, 0))]
            out_shape += [jax.ShapeDtypeStruct((N_KV_HEADS, M, kw), BF16),
                          jax.ShapeDtypeStruct((N_KV_HEADS, M // PAGE_ROWS, HEAD_DIM, PAGE_ROWS), BF16)]
    return pl.pallas_call(
        functools.partial(_kv_proj_kernel, emit_aux=emit_aux, seq=seq),
        grid=(M // tm,),
        in_specs=[pl.BlockSpec((tm, D), lambda i: (i, 0)), pl.BlockSpec((1, D), lambda i: (0, 0))]
        + [pl.BlockSpec(w.shape, lambda i: (0, 0)) for w in weights],
        out_specs=out_specs,
        out_shape=out_shape,
        compiler_params=_params("parallel"),
        name="kv_proj",
    )(x, g.reshape(1, D), *weights)


def _q_proj_kernel(x_ref, g_ref, wq_ref, wg_ref, q_ref, gate_ref, *, transposed):
    tm = x_ref.shape[0]
    xn = (_rms(x_ref[...]) * g_ref[...]).astype(BF16)
    q = _dot(xn, wq_ref[...]) * (HEAD_DIM ** -0.5 * (LOG2E if transposed else 1.0))
    gates = jax.nn.sigmoid(_dot(xn, wg_ref[...]))
    if not transposed:
        q_ref[...] = q.astype(BF16)
        gate_ref[...] = gates
        return
    for r in range(tm // PAGE_ROWS):
        rows = slice(r * PAGE_ROWS, (r + 1) * PAGE_ROWS)
        gt = gates[rows].T
        for hk in range(N_KV_HEADS):
            gate_ref[hk, r] = gt[hk * GATE_PAD:(hk + 1) * GATE_PAD]
            for g in range(GQA_GROUP):
                col = (hk * GQA_GROUP + g) * HEAD_DIM
                q_ref[hk, r, :, g * PAGE_ROWS:(g + 1) * PAGE_ROWS] = q[rows, col:col + HEAD_DIM].T.astype(BF16)


def _q_proj(x, g, wq, wg, transposed):
    M, D = x.shape
    tm = min(M, 256)
    assert M % tm == 0 and (tm % PAGE_ROWS == 0 or not transposed)
    tiles = tm // PAGE_ROWS
    gw = GQA_GROUP * PAGE_ROWS
    if transposed:
        out_specs = [pl.BlockSpec((N_KV_HEADS, tiles, HEAD_DIM, gw), lambda i: (0, i, 0, 0)),
                     pl.BlockSpec((N_KV_HEADS, tiles, GATE_PAD, PAGE_ROWS), lambda i: (0, i, 0, 0))]
        out_shape = [jax.ShapeDtypeStruct((N_KV_HEADS, M // PAGE_ROWS, HEAD_DIM, gw), BF16),
                     jax.ShapeDtypeStruct((N_KV_HEADS, M // PAGE_ROWS, GATE_PAD, PAGE_ROWS), F32)]
    else:
        out_specs = [pl.BlockSpec((tm, wq.shape[1]), lambda i: (i, 0)),
                     pl.BlockSpec((tm, wg.shape[1]), lambda i: (i, 0))]
        out_shape = [jax.ShapeDtypeStruct((M, wq.shape[1]), BF16), jax.ShapeDtypeStruct((M, wg.shape[1]), F32)]
    return pl.pallas_call(
        functools.partial(_q_proj_kernel, transposed=transposed),
        grid=(M // tm,),
        in_specs=[pl.BlockSpec((tm, D), lambda i: (i, 0)), pl.BlockSpec((1, D), lambda i: (0, 0)),
                  pl.BlockSpec(wq.shape, lambda i: (0, 0)), pl.BlockSpec(wg.shape, lambda i: (0, 0))],
        out_specs=out_specs,
        out_shape=out_shape,
        compiler_params=_params("parallel"),
        name="q_proj",
    )(x, g.reshape(1, D), wq, wg)


def _gelu_tanh(x):
    return 0.5 * x * (1.0 + jnp.tanh(math.sqrt(2.0 / math.pi) * (x + 0.044715 * (x * x * x))))


def _cmp_bias_kernel(pos_ref, w1_ref, b1_ref, o_ref):
    for c in range(2):
        acc = jnp.zeros((8, HEAD_DIM), F32)
        for j in range(CMP_BLOCK):
            pj = jnp.broadcast_to(pos_ref[c, j:j + 1, :], (8, HEAD_DIM)).astype(BF16)
            acc = acc + _dot(pj, w1_ref[c, j])
        o_ref[c] = acc + b1_ref[c]


def _cmp_bias(pos, w1, b1):
    return pl.pallas_call(
        _cmp_bias_kernel,
        out_shape=jax.ShapeDtypeStruct((2, 8, HEAD_DIM), F32),
        compiler_params=pltpu.CompilerParams(vmem_limit_bytes=VMEM_LIMIT_BYTES),
        name="cmp_bias",
    )(pos, w1, b1)


def _compress_kernel(pt_ref, *refs, n_pages):
    del pt_ref
    page_refs = refs[:n_pages]
    w1_ref, bias_ref, w2_ref, o_ref, scr_ref, sum_ref = refs[n_pages:]
    sub_per_page = PAGE_ROWS // CMP_STRIDE
    n_sub = n_pages * sub_per_page
    sub_rows = CMP_STRIDE * ROW_SPLIT
    pages_per_chunk = 4 if n_pages % 4 == 0 else 1
    chunk_rows = pages_per_chunk * sub_per_page * ROW_SPLIT
    is_k = (lax.broadcasted_iota(jnp.int32, (chunk_rows, 1), 0) & (ROW_SPLIT - 1)) < N_KV_HEADS
    bias = jnp.where(is_k[0:ROW_SPLIT], bias_ref[0], bias_ref[1])
    bias = jnp.concatenate([bias] * (chunk_rows // ROW_SPLIT), axis=0)

    def by_side(x):
        return jnp.concatenate([jnp.where(is_k, x, 0.0), jnp.where(is_k, 0.0, x)], axis=1).astype(BF16)

    for ck in range(n_pages // pages_per_chunk):
        chunk_pages = page_refs[ck * pages_per_chunk:(ck + 1) * pages_per_chunk]
        acc = jnp.zeros((chunk_rows, 2 * HEAD_DIM), F32)
        for j in range(CMP_STRIDE):
            x = jnp.concatenate([r[n * sub_rows + j * ROW_SPLIT:n * sub_rows + (j + 1) * ROW_SPLIT, :]
                                 for r in chunk_pages for n in range(sub_per_page)], axis=0)
            acc = acc + _dot(by_side(x), w1_ref[j])
        scr_ref[ck * chunk_rows:(ck + 1) * chunk_rows, :] = acc
    total = n_sub * ROW_SPLIT
    for ck in range(n_pages // pages_per_chunk):
        rows = slice(ck * chunk_rows, (ck + 1) * chunk_rows)
        first = scr_ref[rows, 0:HEAD_DIM]
        nxt_rows = slice(ck * chunk_rows + ROW_SPLIT, min((ck + 1) * chunk_rows + ROW_SPLIT, total))
        nxt = scr_ref[nxt_rows, HEAD_DIM:2 * HEAD_DIM]
        if nxt.shape[0] < chunk_rows:
            nxt = jnp.concatenate([nxt, jnp.zeros((chunk_rows - nxt.shape[0], HEAD_DIM), F32)], axis=0)
        hid = _gelu_tanh(first + nxt + bias)
        out = _dot(by_side(hid), w2_ref[...])
        if (ck + 1) * chunk_rows == total:
            row = lax.broadcasted_iota(jnp.int32, (chunk_rows, 1), 0)
            out = jnp.where(row < chunk_rows - ROW_SPLIT, out, 0.0)
        sum_ref[rows, :] = out
    for ch in range(ROW_SPLIT):
        o_ref[0, :, ch * HEAD_DIM:(ch + 1) * HEAD_DIM] = sum_ref[pl.ds(ch, n_sub, stride=ROW_SPLIT), :]


def _compress(pages, table, w1cat, bias, w2cat):
    nb, n_pages = table.shape
    n_sub = n_pages * (PAGE_ROWS // CMP_STRIDE)
    page_rows2d = PAGE_ROWS * ROW_SPLIT
    page_specs = [pl.BlockSpec((page_rows2d, HEAD_DIM), functools.partial(lambda b, pt, p: (pt[b, p], 0), p=p))
                  for p in range(n_pages)]
    const = lambda shape: pl.BlockSpec(shape, lambda b, pt: (0,) * len(shape))
    grid_spec = pltpu.PrefetchScalarGridSpec(
        num_scalar_prefetch=1,
        grid=(nb,),
        in_specs=page_specs + [const(w1cat.shape), const(bias.shape), const(w2cat.shape)],
        out_specs=pl.BlockSpec((1, n_sub, KV_COLS), lambda b, pt: (b, 0, 0)),
        scratch_shapes=[pltpu.VMEM((n_sub * ROW_SPLIT, 2 * HEAD_DIM), F32),
                        pltpu.VMEM((n_sub * ROW_SPLIT, HEAD_DIM), F32)],
    )
    return pl.pallas_call(
        functools.partial(_compress_kernel, n_pages=n_pages),
        grid_spec=grid_spec,
        out_shape=jax.ShapeDtypeStruct((nb, n_sub, KV_COLS), F32),
        compiler_params=_params("parallel"),
        name="compress",
    )(table, *([pages] * n_pages), w1cat, bias, w2cat)


def _rel_bucket_np(dist):
    n = np.maximum(dist, 0)
    max_exact = N_BUCKETS // 2
    nf = np.maximum(n, 1).astype(np.float32)
    large = max_exact + (np.log(nf / np.float32(max_exact)) / np.float32(math.log(MAX_DISTANCE / max_exact))
                         * np.float32(N_BUCKETS - max_exact)).astype(np.int32)
    large = np.minimum(large, N_BUCKETS - 1)
    return np.where(n < max_exact, n, large).astype(np.int32)


def _bias_table_kernel(rb_ref, bk_ref, o_ref, *, minus_far, scale):
    bk = bk_ref[...]
    for h in range(N_HEADS):
        v = jnp.full(bk.shape, rb_ref[0, h], F32)
        for b in range(1, N_BUCKETS):
            v = jnp.where(bk == b, rb_ref[b, h], v)
        if minus_far:
            v = v - rb_ref[N_BUCKETS - 1, h]
        o_ref[h] = v if scale == 1.0 else v * scale


def _bias_table(rel_bias, bucket, minus_far=False, scale=1.0):
    R, C = bucket.shape
    tr = min(R, 128)
    assert R % tr == 0
    return pl.pallas_call(
        functools.partial(_bias_table_kernel, minus_far=minus_far, scale=scale),
        grid=(R // tr,),
        in_specs=[pl.BlockSpec(memory_space=pltpu.SMEM), pl.BlockSpec((tr, C), lambda r: (r, 0))],
        out_specs=pl.BlockSpec((N_HEADS, tr, C), lambda r: (0, r, 0)),
        out_shape=jax.ShapeDtypeStruct((N_HEADS, R, C), F32),
        compiler_params=_params("parallel"),
        name="bias_table",
    )(rel_bias, jnp.asarray(bucket))


def _overlap_np(n_sub, n_slc):
    n_cmp = n_sub - 1
    cs = np.arange(n_cmp) * CMP_STRIDE
    ce = cs + CMP_BLOCK - 1
    ss = np.arange(n_slc) * SLC_BLOCK
    se = ss + SLC_BLOCK - 1
    ov = np.minimum(ce[:, None], se[None, :]) - np.maximum(cs[:, None], ss[None, :]) + 1
    out = np.zeros((n_sub, n_slc), np.float32)
    out[:n_cmp] = np.maximum(ov, 0).astype(np.float32) / CMP_BLOCK
    return out


def _pad_to(a, shape):
    return np.pad(a, [(0, s - d) for d, s in zip(a.shape, shape)])


def _score(imp, j, qb, n_slc):
    prio = 2.0 * (j == qb).astype(F32) + (j == qb - 1).astype(F32) + (j == 0).astype(F32)
    score = jnp.where(j <= qb, imp + SEL_BIG * prio, -SEL_BIG)
    return jnp.where(j < n_slc, score, -3e38)


def _attn_prompt_kernel(qt_ref, gt_ref, ck_ref, cv_ref, sk_ref, svt_ref, wk_ref, wvt_ref,
                        toe_ref, cmpb_ref, ovt_ref, o_ref,
                        m_ref, l_ref, acc_ref, accw_ref, *, n_slc, n_sub, hps):
    tq = PAGE_ROWS
    i = pl.program_id(2)
    heads = range(hps)
    t_idx = lax.broadcasted_iota(jnp.int32, (1, tq), 1)
    qpos = i * tq + t_idx
    lane4 = lambda x: jnp.concatenate([x] * GQA_GROUP, axis=1)
    n_idx = lax.broadcasted_iota(jnp.int32, (n_sub, 1), 0)
    valid = lane4((n_idx * CMP_STRIDE + (CMP_BLOCK - 1) <= qpos) & (n_idx < n_sub - 1))
    nsp = ovt_ref.shape[0]
    j = lax.broadcasted_iota(jnp.int32, (nsp, 1), 0)
    qb = qpos >> int(math.log2(SLC_BLOCK))

    qts, qt_sels, o_cs = [], [], []
    for hh in heads:
        qt = qt_ref[hh, 0]
        cols = slice(hh * HEAD_DIM, (hh + 1) * HEAD_DIM)
        kc = ck_ref[0, :, cols].astype(BF16)
        vct = cv_ref[0, :, cols].T.astype(BF16)
        bias_c = jnp.concatenate([cmpb_ref[hh * GQA_GROUP + g, 0] for g in range(GQA_GROUP)], axis=1)
        logits = jnp.where(valid, _dot(kc, qt) + bias_c, NEG_INF)
        p = jnp.exp2(logits - jnp.max(logits, axis=0, keepdims=True))
        p = p * (1.0 / jnp.sum(p, axis=0, keepdims=True))
        p = jnp.where(valid, p, 0.0)
        o_cs.append(_dot(vct, p.astype(BF16)))
        psum = p[:, 0:tq]
        for g in range(1, GQA_GROUP):
            psum = psum + p[:, g * tq:(g + 1) * tq]
        imp = jnp.dot(ovt_ref[...], psum, precision=lax.Precision.HIGHEST, preferred_element_type=F32)
        score = _score(imp, j, qb, n_slc)
        rank = jnp.zeros((nsp, tq), jnp.int32)
        for jj in range(n_slc):
            row = score[jj:jj + 1, :]
            rank = rank + ((row > score) | ((row == score) & (jj < j))).astype(jnp.int32)
        madd = jnp.where(rank < min(N_SELECT, n_slc), 0.0, NEG_INF)
        qts.append(qt)
        qt_sels.append(jnp.concatenate([qt, lane4(madd).astype(BF16),
                                        jnp.zeros((HEAD_DIM - nsp, GQA_GROUP * tq), BF16)], axis=0))

    k_row = lax.broadcasted_iota(jnp.int32, (tq, 1), 0)

    def chunk(hh, k, kinds, k_ref, vt_ref, q_op, out_ref):
        nt = len(kinds)
        k0 = pl.multiple_of(k * tq, tq)
        s = _dot(k_ref[hh, pl.ds(k0, nt * tq), :], q_op)
        parts = []
        for jx, kind in enumerate(kinds):
            sj = s[jx * tq:(jx + 1) * tq]
            if kind in ("near", "diag"):
                d = 0 if kind == "diag" else 1
                sj = sj + jnp.concatenate([toe_ref[(hh * GQA_GROUP + g) * 2 + d] for g in range(GQA_GROUP)],
                                          axis=1)
            if kind == "diag":
                sj = jnp.where(lane4(k_row <= t_idx), sj, NEG_INF)
            elif kind == "edge":
                sj = jnp.where(lane4(k_row > t_idx), sj, NEG_INF)
            parts.append(sj)
        m_new = functools.reduce(jnp.maximum, parts)
        m_new = jnp.max(m_new, axis=0, keepdims=True)
        if out_ref is None:
            m_old = m_ref[hh]
            m_new = jnp.maximum(m_old, m_new)
            alpha = jnp.exp2(m_old - m_new)
        ps = [jnp.exp2(sj - m_new) for sj in parts]
        lsum = jnp.sum(functools.reduce(jnp.add, ps), axis=0, keepdims=True)
        vt = jnp.concatenate([vt_ref[hh, k + jx] for jx in range(nt)], axis=1)
        pv = _dot(vt, jnp.concatenate(ps, axis=0).astype(BF16))
        if out_ref is None:
            l_ref[hh] = alpha * l_ref[hh] + lsum
            acc_ref[hh] = alpha * acc_ref[hh] + pv
            m_ref[hh] = m_new
        else:
            out_ref[hh] = pv * (1.0 / lsum)

    def tail_kinds(n_far):
        return ("far",) * n_far + ("near", "diag")

    far_step = SEL_FAR_TILES
    wt = WINDOW // tq
    m_ref[...] = jnp.full(m_ref.shape, NEG_INF, F32)
    l_ref[...] = jnp.zeros(l_ref.shape, F32)
    acc_ref[...] = jnp.zeros(acc_ref.shape, F32)

    n_far = jnp.maximum(i - 1, 0) // far_step

    def far_body(c, carry):
        for hh in heads:
            chunk(hh, c * far_step, ("far",) * far_step, sk_ref, svt_ref, qt_sels[hh], None)
        return carry

    lax.fori_loop(0, n_far, far_body, 0)

    for c in range(wt):
        @pl.when(i == c)
        def _(c=c):
            kinds = tail_kinds(c - 1) if c >= 1 else ("diag",)
            for hh in heads:
                chunk(hh, 0, kinds, sk_ref, svt_ref, qt_sels[hh], None)
                chunk(hh, 0, kinds, wk_ref, wvt_ref, qts[hh], accw_ref)

    for r in range(far_step):
        @pl.when((i >= wt) & (lax.rem(i + far_step - 1, far_step) == r))
        def _(r=r):
            for hh in heads:
                chunk(hh, i - 1 - r, tail_kinds(r), sk_ref, svt_ref, qt_sels[hh], None)
                chunk(hh, i - wt, ("edge",) + tail_kinds(wt - 2), wk_ref, wvt_ref, qts[hh], accw_ref)

    for hh in heads:
        o_s = acc_ref[hh] * (1.0 / l_ref[hh])
        o_w = accw_ref[hh]
        o_c = o_cs[hh]
        gt = gt_ref[hh, 0]
        for g in range(GQA_GROUP):
            lanes = slice(g * tq, (g + 1) * tq)
            o = (gt[g:g + 1] * o_c[:, lanes] + gt[GQA_GROUP + g:GQA_GROUP + g + 1] * o_s[:, lanes]
                 + gt[2 * GQA_GROUP + g:2 * GQA_GROUP + g + 1] * o_w[:, lanes])
            col = (hh * GQA_GROUP + g) * HEAD_DIM
            o_ref[0, :, col:col + HEAD_DIM] = o.T.astype(o_ref.dtype)


def _attn_prompt(qt, gt, ckv, sk, svt, wk, wvt, toe, cmpb, B, S):
    tq = PAGE_ROWS
    nq = S // tq
    n_sub = ckv.shape[1]
    n_slc = -(-S // SLC_BLOCK)
    nsp = -(-n_slc // 16) * 16
    assert S % tq == 0 and nsp <= HEAD_DIM and nq > WINDOW // tq
    ovt = jnp.asarray(_pad_to(_overlap_np(n_sub, n_slc).T, (nsp, n_sub)))
    gw = GQA_GROUP * tq
    hps = PROMPT_HEADS_PER_STEP
    assert N_KV_HEADS % hps == 0
    return pl.pallas_call(
        functools.partial(_attn_prompt_kernel, n_slc=n_slc, n_sub=n_sub, hps=hps),
        grid=(B, N_KV_HEADS // hps, nq),
        in_specs=[pl.BlockSpec((hps, 1, HEAD_DIM, gw), lambda b, h, i: (h, b * nq + i, 0, 0)),
                  pl.BlockSpec((hps, 1, GATE_PAD, tq), lambda b, h, i: (h, b * nq + i, 0, 0)),
                  pl.BlockSpec((1, n_sub, hps * HEAD_DIM), lambda b, h, i: (b, 0, h)),
                  pl.BlockSpec((1, n_sub, hps * HEAD_DIM), lambda b, h, i: (b, 0, N_KV_HEADS // hps + h)),
                  pl.BlockSpec((hps, S, 2 * HEAD_DIM), lambda b, h, i: (h, b, 0)),
                  pl.BlockSpec((hps, nq, HEAD_DIM, tq), lambda b, h, i: (h, b, 0, 0)),
                  pl.BlockSpec((hps, S, HEAD_DIM), lambda b, h, i: (h, b, 0)),
                  pl.BlockSpec((hps, nq, HEAD_DIM, tq), lambda b, h, i: (h, b, 0, 0)),
                  pl.BlockSpec((hps * GQA_GROUP * 2, tq, tq), lambda b, h, i: (h, 0, 0)),
                  pl.BlockSpec((hps * GQA_GROUP, 1, n_sub, tq), lambda b, h, i: (h, i, 0, 0)),
                  pl.BlockSpec(ovt.shape, lambda b, h, i: (0, 0))],
        out_specs=pl.BlockSpec((1, tq, hps * GQA_GROUP * HEAD_DIM), lambda b, h, i: (b, i, h)),
        out_shape=jax.ShapeDtypeStruct((B, S, N_HEADS * HEAD_DIM), BF16),
        scratch_shapes=[pltpu.VMEM((hps, 1, gw), F32), pltpu.VMEM((hps, 1, gw), F32),
                        pltpu.VMEM((hps, HEAD_DIM, gw), F32), pltpu.VMEM((hps, HEAD_DIM, gw), F32)],
        compiler_params=_params("parallel", "parallel", "arbitrary"),
        name="attn_prompt",
    )(qt, gt, ckv, ckv, sk, svt, wk, wvt, toe, cmpb, ovt)


def _attn_sample_kernel(pt_ref, *refs, n_pages, n_slc, n_sub, past_len, win_buf, ds):
    del pt_ref
    page_refs = refs[:n_pages]
    (q_ref, gate_ref, ckv_ref, snew_ref, wst_ref, wnew_ref, bias_ref, ov_ref, e_ref,
     o_ref, wout_ref) = refs[n_pages:]
    nk_s = (n_pages + 1) * PAGE_ROWS
    nk_w = win_buf + PAGE_ROWS
    head_col = lambda r, col, n: r[pl.ds(col, n, stride=ROW_SPLIT), :]

    shift = ds * ROW_SPLIT
    wout_ref[0:win_buf * ROW_SPLIT - shift, :] = wst_ref[shift:win_buf * ROW_SPLIT, :]
    wout_ref[win_buf * ROW_SPLIT - shift:win_buf * ROW_SPLIT, :] = wnew_ref[...]

    t_idx = lax.broadcasted_iota(jnp.int32, (ds, 1), 0)
    qpos = past_len + t_idx
    pad = jnp.zeros((PAGE_ROWS - ds, HEAD_DIM), F32)
    gates = gate_ref[0]
    kv_heads = range(N_KV_HEADS)
    head_rows = GQA_GROUP * ds
    per_head = lambda x, reps: jnp.concatenate([x] * reps, axis=0)
    new_rows = lambda r, col: jnp.concatenate([r[pl.ds(col, ds, stride=ROW_SPLIT), :], pad], axis=0).astype(BF16)

    ks_pos = lax.broadcasted_iota(jnp.int32, (1, nk_s), 1)
    dist_w = win_buf + t_idx - lax.broadcasted_iota(jnp.int32, (1, nk_w), 1)
    ok_w = per_head((dist_w >= 0) & (dist_w < WINDOW), N_HEADS)
    n_idx = lax.broadcasted_iota(jnp.int32, (1, n_sub), 1)
    ok_c = per_head((n_idx * CMP_STRIDE + (CMP_BLOCK - 1) <= qpos) & (n_idx < n_sub - 1), N_HEADS)
    lanes = ov_ref.shape[1]
    j = lax.broadcasted_iota(jnp.int32, (1, lanes), 1)
    qb = per_head(qpos >> int(math.log2(SLC_BLOCK)), N_KV_HEADS)

    def softmax(logits, ok):
        logits = jnp.where(ok, logits, NEG_INF)
        p = jnp.exp(logits - jnp.max(logits, axis=-1, keepdims=True))
        return p / jnp.sum(p, axis=-1, keepdims=True)

    bias = bias_ref[...].reshape(N_HEADS * ds, bias_ref.shape[2])
    qs = [jnp.concatenate([q_ref[0, :, (hk * GQA_GROUP + g) * HEAD_DIM:(hk * GQA_GROUP + g + 1) * HEAD_DIM]
                           for g in range(GQA_GROUP)], axis=0) for hk in kv_heads]
    kcols = [slice(hk * HEAD_DIM, (hk + 1) * HEAD_DIM) for hk in kv_heads]
    vcols = [slice((N_KV_HEADS + hk) * HEAD_DIM, (N_KV_HEADS + hk + 1) * HEAD_DIM) for hk in kv_heads]

    def attend(k_past, v_past, k_new, v_new, bias_cols, ok):
        n = k_past(0).shape[0]
        logits = jnp.concatenate(
            [jnp.concatenate([_dot_nt(qs[hk], k_past(hk).astype(BF16)), _dot_nt(qs[hk], k_new(hk))], axis=1)
             for hk in kv_heads], axis=0) + bias_cols
        p = softmax(logits, ok).astype(BF16)
        return [_dot(p[hk * head_rows:(hk + 1) * head_rows, 0:n], v_past(hk).astype(BF16))
                + _dot(p[hk * head_rows:(hk + 1) * head_rows, n:], v_new(hk)) for hk in kv_heads]

    logits_c = jnp.concatenate([_dot_nt(qs[hk], ckv_ref[0, :, kcols[hk]].astype(BF16)) for hk in kv_heads], axis=0)
    p_c = jnp.where(ok_c, softmax(logits_c + bias[:, nk_s + nk_w:nk_s + nk_w + n_sub], ok_c), 0.0)
    o_c = [_dot(p_c[hk * head_rows:(hk + 1) * head_rows].astype(BF16), ckv_ref[0, :, vcols[hk]].astype(BF16))
           for hk in kv_heads]
    psum = jnp.concatenate(
        [functools.reduce(jnp.add, [p_c[(hk * GQA_GROUP + g) * ds:(hk * GQA_GROUP + g + 1) * ds]
                                    for g in range(GQA_GROUP)]) for hk in kv_heads], axis=0)
    imp = jnp.dot(psum, ov_ref[...], precision=lax.Precision.HIGHEST, preferred_element_type=F32)
    score = _score(imp, j, qb, n_slc)
    rank = jnp.zeros(score.shape, jnp.int32)
    for jj in range(n_slc):
        col = score[:, jj:jj + 1]
        rank = rank + ((col > score) | ((col == score) & (jj < j))).astype(jnp.int32)
    sel = ((rank < min(N_SELECT, n_slc)) & (j < n_slc)).astype(BF16)
    allowed = (_dot(sel, e_ref[...]) > 0.5) & (ks_pos <= per_head(qpos, N_KV_HEADS))
    ok_s = jnp.concatenate([per_head(allowed[hk * ds:(hk + 1) * ds], GQA_GROUP) for hk in kv_heads], axis=0)

    past_s = lambda col: jnp.concatenate([head_col(r, col, PAGE_ROWS) for r in page_refs], axis=0)
    o_s = attend(lambda hk: past_s(hk), lambda hk: past_s(N_KV_HEADS + hk),
                 lambda hk: new_rows(snew_ref, hk), lambda hk: new_rows(snew_ref, N_KV_HEADS + hk),
                 bias[:, 0:nk_s], ok_s)
    o_w = attend(lambda hk: head_col(wst_ref, hk, win_buf), lambda hk: head_col(wst_ref, N_KV_HEADS + hk, win_buf),
                 lambda hk: new_rows(wnew_ref, hk), lambda hk: new_rows(wnew_ref, N_KV_HEADS + hk),
                 bias[:, nk_s:nk_s + nk_w], ok_w)

    for hk in kv_heads:
        gcol = hk * GATE_PAD
        for g in range(GQA_GROUP):
            rows = slice(g * ds, (g + 1) * ds)
            gate = lambda br, g=g: gates[:, gcol + br * GQA_GROUP + g:gcol + br * GQA_GROUP + g + 1]
            o = gate(0) * o_c[hk][rows] + gate(1) * o_s[hk][rows] + gate(2) * o_w[hk][rows]
            col = (hk * GQA_GROUP + g) * HEAD_DIM
            o_ref[0, :, col:col + HEAD_DIM] = o.astype(o_ref.dtype)


def _attn_sample(q, gates, ckv, cache_slc, table, slc_new, win_state, win_new, bias, past_len, win_buf):
    DB, ds, _ = q.shape
    n_pages = table.shape[1]
    n_sub = ckv.shape[1]
    n_slc = -(-(past_len + ds) // SLC_BLOCK)
    lanes = PAGE_ROWS
    assert ds % 8 == 0 and ds <= PAGE_ROWS and n_slc <= lanes and n_sub <= lanes
    assert win_buf <= past_len and win_buf > ds
    nk_s = (n_pages + 1) * PAGE_ROWS
    ov = jnp.asarray(_pad_to(_overlap_np(n_sub, n_slc), (n_sub, lanes)))
    e = (np.arange(nk_s)[None, :] // SLC_BLOCK == np.arange(lanes)[:, None]) & (np.arange(lanes)[:, None] < n_slc)
    e = jnp.asarray(e.astype(np.float32), dtype=BF16)
    page_specs = [pl.BlockSpec((PAGE_ROWS * ROW_SPLIT, HEAD_DIM),
                               functools.partial(lambda b, pt, p: (pt[b, p], 0), p=p)) for p in range(n_pages)]
    per_b3 = lambda shape: pl.BlockSpec((1,) + shape[1:], lambda b, pt: (b, 0, 0))
    rows_b = lambda n: pl.BlockSpec((n * ROW_SPLIT, HEAD_DIM), lambda b, pt: (b, 0))
    const = lambda shape: pl.BlockSpec(shape, lambda b, pt: (0,) * len(shape))
    grid_spec = pltpu.PrefetchScalarGridSpec(
        num_scalar_prefetch=1,
        grid=(DB,),
        in_specs=page_specs + [per_b3(q.shape), per_b3(gates.shape), per_b3(ckv.shape), rows_b(ds),
                               rows_b(win_buf), rows_b(ds), const(bias.shape), const(ov.shape), const(e.shape)],
        out_specs=[per_b3(q.shape), rows_b(win_buf)],
    )
    return pl.pallas_call(
        functools.partial(_attn_sample_kernel, n_pages=n_pages, n_slc=n_slc, n_sub=n_sub,
                          past_len=past_len, win_buf=win_buf, ds=ds),
        grid_spec=grid_spec,
        out_shape=[jax.ShapeDtypeStruct(q.shape, BF16), jax.ShapeDtypeStruct(win_state.shape, F32)],
        compiler_params=_params("parallel"),
        name="attn_sample",
    )(table, *([cache_slc] * n_pages), q, gates, ckv, slc_new, win_state, win_new, bias, ov, e)


def _prompt_buckets(S, n_sub):
    tq = PAGE_ROWS
    k = np.arange(tq)[:, None]
    t = np.arange(tq)[None, :]
    toe = np.concatenate([_rel_bucket_np(t - k + d * tq) for d in range(2)], axis=0)
    kpos_c = np.arange(n_sub)[:, None] * CMP_STRIDE + CMP_BLOCK - 1
    cmp = np.concatenate([_rel_bucket_np(i * tq + t - kpos_c) for i in range(S // tq)], axis=0)
    return toe, cmp


def _sample_buckets(ds, past_len, n_pages, win_buf, n_sub):
    qpos = past_len + np.arange(ds)[:, None]
    ks = np.arange((n_pages + 1) * PAGE_ROWS)[None, :]
    kw = past_len - win_buf + np.arange(win_buf + PAGE_ROWS)[None, :]
    kc = np.arange(n_sub)[None, :] * CMP_STRIDE + CMP_BLOCK - 1
    return np.concatenate([_rel_bucket_np(qpos - ks), _rel_bucket_np(qpos - kw), _rel_bucket_np(qpos - kc)], axis=1)


def kernel(x_prompt, x_sample, state_pool, cache_cmp_kv, cache_slc_kv, state_win_kv, page_table, norms, w_pool,
           pool_scale, kv_norm, w_kv, cmp_pos, cmp_w1, cmp_b1, cmp_w2, w_q, w_o, rel_bias, w_ffn_in, w_ffn_out):
    B, S, D = x_prompt.shape
    DB, DS, _ = x_sample.shape
    n_pages = page_table.shape[1]
    past_len = n_pages * cache_slc_kv.shape[1]
    win_buf = state_win_kv.shape[1]
    assert norms.shape[0] == 2 and w_pool.shape[0] == 1 and w_q.shape[0] == 1, "one pooling + one attention layer"
    assert cache_slc_kv.shape[1] == PAGE_ROWS and S % PAGE_ROWS == 0
    assert past_len % CMP_STRIDE == 0 and DS < CMP_STRIDE and D == N_HEADS * HEAD_DIM
    assert S >= win_buf and N_BRANCH * GQA_GROUP <= GATE_PAD

    qdim = N_HEADS * HEAD_DIM
    w_pool_b = w_pool[0].astype(BF16)
    w_in_b = w_ffn_in.astype(BF16)
    w_out_b = w_ffn_out.astype(BF16)
    w_kv_b = [w_kv[:, k * KV_COLS:(k + 1) * KV_COLS].astype(BF16) for k in range(N_BRANCH)]
    w_qq = w_q[0, :, :qdim].astype(BF16)
    w_qg = w_q[0, :, qdim:].reshape(D, N_BRANCH, N_KV_HEADS, GQA_GROUP).transpose(0, 2, 1, 3)
    w_qg = jnp.pad(w_qg.reshape(D, N_KV_HEADS, N_BRANCH * GQA_GROUP),
                   ((0, 0), (0, 0), (0, GATE_PAD - N_BRANCH * GQA_GROUP)))
    w_qg = jnp.pad(w_qg.reshape(D, N_KV_HEADS * GATE_PAD), ((0, 0), (0, 128 - N_KV_HEADS * GATE_PAD))).astype(BF16)
    w_o_b = w_o.astype(BF16)
    w1_b = cmp_w1.astype(BF16)
    w1cat = w1_b.reshape(2, 2, CMP_STRIDE, HEAD_DIM, HEAD_DIM).transpose(2, 0, 3, 1, 4)
    w1cat = w1cat.reshape(CMP_STRIDE, 2 * HEAD_DIM, 2 * HEAD_DIM)
    w2cat = cmp_w2.astype(BF16).reshape(2 * HEAD_DIM, HEAD_DIM)
    cmp_bias = _cmp_bias(cmp_pos, w1_b, cmp_b1.reshape(2, 1, HEAD_DIM))

    def layer0(x, buf, pos0):
        Bx, Sx, _ = x.shape
        M = Bx * Sx
        mix, st = _pool_mix(x, buf, norms[0, 0], pos0)
        h = _proj_res(mix.reshape(M, D), w_pool_b, pool_scale[0], norms[0, 1], x.reshape(M, D), "pool_proj")
        return _ffn(h, norms[0, 2], norms[0, 3], w_in_b[0], w_out_b[0]), st

    def layer1_tail(h1, o):
        h2 = _proj_res(o, w_o_b, None, norms[1, 1], h1, "o_proj")
        return _ffn(h2, norms[1, 2], norms[1, 3], w_in_b[1], w_out_b[1])

    rows2d = lambda a: a.reshape(-1, HEAD_DIM)
    kv5 = lambda r, b, s: r.reshape(b, s, 2, N_KV_HEADS, HEAD_DIM)

    h1, st_p = layer0(x_prompt, jnp.zeros((B, POOL_BUF, D), F32), 0)
    cmp_p, slc_p, win_p, sk, svt, wk, wvt = _kv_proj(h1, kv_norm, w_kv_b, True, S)
    qt, gt = _q_proj(h1, norms[1, 0], w_qq, w_qg, True)
    pages_p = S // PAGE_ROWS
    ckv_p = _compress(cmp_p, jnp.arange(B * pages_p, dtype=jnp.int32).reshape(B, pages_p), w1cat, cmp_bias, w2cat)
    n_sub_p = ckv_p.shape[1]
    toe_bk, cmp_bk = _prompt_buckets(S, n_sub_p)
    toe = _bias_table(rel_bias, toe_bk, True, LOG2E).reshape(N_HEADS * 2, PAGE_ROWS, PAGE_ROWS)
    cmpb = _bias_table(rel_bias, cmp_bk, False, LOG2E).reshape(N_HEADS, pages_p, n_sub_p, PAGE_ROWS)
    o_p = _attn_prompt(qt, gt, ckv_p, sk, svt, wk, wvt, toe, cmpb, B, S)
    y_prompt = layer1_tail(h1, o_p.reshape(B * S, qdim)).reshape(B, S, D)
    win_state_p = win_p.reshape(B, S * ROW_SPLIT, HEAD_DIM)[:, (S - win_buf) * ROW_SPLIT:]

    h1s, st_s = layer0(x_sample, state_pool[0], past_len)
    cmp_s, slc_s, win_s = _kv_proj(h1s, kv_norm, w_kv_b, False, DS)
    q_s, gates_s = _q_proj(h1s, norms[1, 0], w_qq, w_qg, False)
    ckv_s = _compress(rows2d(cache_cmp_kv), page_table, w1cat, cmp_bias, w2cat)
    bias_s = _bias_table(rel_bias, _sample_buckets(DS, past_len, n_pages, win_buf, ckv_s.shape[1]))
    o_s, win_state_s = _attn_sample(
        q_s.reshape(DB, DS, qdim), gates_s.reshape(DB, DS, 128), ckv_s, rows2d(cache_slc_kv), page_table,
        slc_s, rows2d(state_win_kv), win_s, bias_s, past_len, win_buf)
    y_sample = layer1_tail(h1s, o_s.reshape(DB * DS, qdim)).reshape(DB, DS, D)

    return (y_prompt, y_sample, st_p[None], st_s[None],
            kv5(cmp_p, B, S), kv5(cmp_s, DB, DS), kv5(slc_p, B, S), kv5(slc_s, DB, DS),
            kv5(win_state_p, B, win_buf), kv5(win_state_s, DB, win_buf))
```

```python
import functools
import math

import numpy as np
import jax
import jax.numpy as jnp
from jax import lax
from jax.experimental import pallas as pl
from jax.experimental.pallas import tpu as pltpu

F32 = jnp.float32
BF16 = jnp.bfloat16

HEAD_DIM = 128
N_KV_HEADS = 4
GQA_GROUP = 4
N_HEADS = N_KV_HEADS * GQA_GROUP
N_BRANCH = 3
POOL_WINDOWS = (2, 4, 8, 16)
POOL_BUF = max(POOL_WINDOWS) - 1
POOL_HALO = POOL_BUF + 1
CMP_BLOCK = 32
CMP_STRIDE = 16
SLC_BLOCK = 64
N_SELECT = 16
WINDOW = 512
N_BUCKETS = 32
MAX_DISTANCE = 128
RMS_EPS = 1e-6
SEL_BIG = 1e3
NEG_INF = -1e30
LOG2E = math.log2(math.e)
PAGE_ROWS = 128
KV_COLS = 2 * N_KV_HEADS * HEAD_DIM
ROW_SPLIT = 2 * N_KV_HEADS
GATE_PAD = 16
SEL_FAR_TILES = 4
PROMPT_HEADS_PER_STEP = 4

VMEM_LIMIT_BYTES = 56 * 1024 * 1024

_NT = (((1,), (1,)), ((), ()))


def _params(*sem):
    return pltpu.CompilerParams(dimension_semantics=sem, vmem_limit_bytes=VMEM_LIMIT_BYTES)


def _rms(x):
    return x * lax.rsqrt(jnp.mean(x * x, axis=-1, keepdims=True) + RMS_EPS)


def _dot(a, b):
    return jnp.dot(a, b, preferred_element_type=F32)


def _dot_nt(a, b):
    return lax.dot_general(a, b, _NT, preferred_element_type=F32)


def _pool_layer_kernel(h_ref, buf_ref, g0_ref, w_ref, scale_ref, g1_ref, o_ref, st_ref, ext_ref, *, ts, pos0):
    i = pl.program_id(1)
    nb, _, D = h_ref.shape
    groups = w_ref.shape[0]
    gd = D // groups

    @pl.when(i == 0)
    def _():
        ext_ref[:, 0:POOL_HALO, :] = buf_ref[...]

    @pl.when(i > 0)
    def _():
        ext_ref[:, 0:POOL_HALO, :] = ext_ref[:, ts:ts + POOL_HALO, :]

    x = h_ref[...]
    ext_ref[:, POOL_HALO:POOL_HALO + ts, :] = _rms(x) * g0_ref[...]
    pos = pos0 + i * ts + lax.broadcasted_iota(jnp.int32, (1, ts, 1), 1)
    parts = []
    for gi, w in enumerate(POOL_WINDOWS):
        cols = slice(gi * gd, (gi + 1) * gd)
        s = ext_ref[:, :, cols]
        cur = s[:, POOL_HALO:POOL_HALO + ts]
        span = 1
        while span < w:
            s = s + pltpu.roll(s, span, 1)
            span *= 2
        s = s[:, POOL_HALO:POOL_HALO + ts]
        inv_cnt = 1.0 / jnp.minimum(w, pos + 1).astype(F32)
        mix = (s * inv_cnt - cur).astype(BF16).reshape(nb * ts, gd)
        parts.append(_dot(mix, w_ref[gi]))
    y = jnp.concatenate(parts, axis=-1) * scale_ref[...]
    o_ref[...] = x + (_rms(y) * g1_ref[...]).reshape(nb, ts, D)
    st_ref[...] = ext_ref[:, ts:ts + POOL_HALO, :]


def _pool_layer(h, buf, g0, w, scale, g1, pos0):
    B, S, D = h.shape
    rows = 256
    ts = min(S, rows)
    nb = max(1, min(B, rows // ts)) if S == ts else 1
    assert S % ts == 0 and B % nb == 0 and (ts >= POOL_HALO or S == ts) and ts % 8 == 0
    assert w.shape[0] == len(POOL_WINDOWS) and all(pw & (pw - 1) == 0 for pw in POOL_WINDOWS)
    buf16 = jnp.pad(buf, ((0, 0), (POOL_HALO - POOL_BUF, 0), (0, 0)))
    vec = lambda: pl.BlockSpec((1, D), lambda b, i: (0, 0))
    out, st = pl.pallas_call(
        functools.partial(_pool_layer_kernel, ts=ts, pos0=pos0),
        grid=(B // nb, S // ts),
        in_specs=[pl.BlockSpec((nb, ts, D), lambda b, i: (b, i, 0)),
                  pl.BlockSpec((nb, POOL_HALO, D), lambda b, i: (b, 0, 0)),
                  vec(), pl.BlockSpec(w.shape, lambda b, i: (0, 0, 0)), vec(), vec()],
        out_specs=[pl.BlockSpec((nb, ts, D), lambda b, i: (b, i, 0)),
                   pl.BlockSpec((nb, POOL_HALO, D), lambda b, i: (b, 0, 0))],
        out_shape=[jax.ShapeDtypeStruct((B, S, D), F32),
                   jax.ShapeDtypeStruct((B, POOL_HALO, D), F32)],
        scratch_shapes=[pltpu.VMEM((nb, POOL_HALO + ts, D), F32)],
        compiler_params=_params("parallel", "arbitrary"),
        name="pool_layer",
    )(h, buf16, g0.reshape(1, D), w, scale.reshape(1, D), g1.reshape(1, D))
    return out, st[:, POOL_HALO - POOL_BUF:]


def _proj_res_kernel(a_ref, w_ref, g_ref, res_ref, o_ref):
    o_ref[...] = res_ref[...] + _rms(_dot(a_ref[...], w_ref[...])) * g_ref[...]


def _proj_res(a, w, g, res):
    M, D = a.shape
    tm = min(M, 256)
    assert M % tm == 0
    row = lambda i: (i, 0)
    fixed = lambda i: (0, 0)
    return pl.pallas_call(
        _proj_res_kernel,
        grid=(M // tm,),
        in_specs=[pl.BlockSpec((tm, D), row), pl.BlockSpec(w.shape, fixed), pl.BlockSpec((1, D), fixed),
                  pl.BlockSpec((tm, D), row)],
        out_specs=pl.BlockSpec((tm, D), row),
        out_shape=jax.ShapeDtypeStruct((M, D), F32),
        compiler_params=_params("parallel"),
        name="o_proj",
    )(a, w, g.reshape(1, D), res)


def _ffn_kernel(h_ref, gi_ref, go_ref, wg_ref, wu_ref, wo_ref, out_ref, xn_ref, acc_ref):
    f = pl.program_id(1)

    @pl.when(f == 0)
    def _():
        xn_ref[...] = (_rms(h_ref[...]) * gi_ref[...]).astype(xn_ref.dtype)
        acc_ref[...] = jnp.zeros_like(acc_ref)

    xn = xn_ref[...]
    gate = _dot(xn, wg_ref[...])
    up = _dot(xn, wu_ref[...])
    act = (gate * jax.nn.sigmoid(gate) * up).astype(BF16)
    acc_ref[...] += _dot(act, wo_ref[...])

    @pl.when(f == pl.num_programs(1) - 1)
    def _():
        out_ref[...] = h_ref[...] + _rms(acc_ref[...]) * go_ref[...]


def _ffn(h, g_in, g_out, w_in, w_out):
    M, D = h.shape
    F = w_out.shape[0]
    tm = min(M, 512)
    tf = 512
    assert M % tm == 0 and F % tf == 0
    nf = F // tf
    return pl.pallas_call(
        _ffn_kernel,
        grid=(M // tm, nf),
        in_specs=[pl.BlockSpec((tm, D), lambda i, f: (i, 0)),
                  pl.BlockSpec((1, D), lambda i, f: (0, 0)),
                  pl.BlockSpec((1, D), lambda i, f: (0, 0)),
                  pl.BlockSpec((D, tf), lambda i, f: (0, f)),
                  pl.BlockSpec((D, tf), lambda i, f: (0, nf + f)),
                  pl.BlockSpec((tf, D), lambda i, f: (f, 0))],
        out_specs=pl.BlockSpec((tm, D), lambda i, f: (i, 0)),
        out_shape=jax.ShapeDtypeStruct((M, D), F32),
        scratch_shapes=[pltpu.VMEM((tm, D), BF16), pltpu.VMEM((tm, D), F32)],
        compiler_params=_params("parallel", "arbitrary"),
        name="ffn",
    )(h, g_in.reshape(1, D), g_out.reshape(1, D), w_in, w_in, w_out)


def _kv_proj_kernel(x_ref, g_ref, *refs, emit_aux, seq):
    w_refs, out_refs = refs[:N_BRANCH], refs[N_BRANCH:]
    tm = x_ref.shape[0]
    xn = (_rms(x_ref[...]) * g_ref[...]).astype(BF16)
    for br in range(N_BRANCH):
        y = _dot(xn, w_refs[br][...])
        for j in range(ROW_SPLIT):
            out_refs[br][pl.ds(j, tm, stride=ROW_SPLIT), :] = y[:, j * HEAD_DIM:(j + 1) * HEAD_DIM]
        if emit_aux and br >= 1:
            kb_ref = out_refs[N_BRANCH + 2 * (br - 1)]
            vt_ref = out_refs[N_BRANCH + 2 * (br - 1) + 1]
            if br == 1:
                pos = lax.rem(pl.program_id(0) * tm, seq) + lax.broadcasted_iota(jnp.int32, (tm, 1), 0)
                col = lax.broadcasted_iota(jnp.int32, (1, HEAD_DIM), 1)
                onehot = (col == (pos >> int(math.log2(SLC_BLOCK)))).astype(BF16)
            for h in range(N_KV_HEADS):
                kh = y[:, h * HEAD_DIM:(h + 1) * HEAD_DIM].astype(BF16)
                kb_ref[h] = jnp.concatenate([kh, onehot], axis=1) if br == 1 else kh
                vcol = (N_KV_HEADS + h) * HEAD_DIM
                for r in range(tm // PAGE_ROWS):
                    v = y[r * PAGE_ROWS:(r + 1) * PAGE_ROWS, vcol:vcol + HEAD_DIM]
                    vt_ref[h, r] = v.T.astype(BF16)


def _kv_proj(x, g, weights, emit_aux, seq):
    M, D = x.shape
    tm = min(M, 256)
    assert M % tm == 0 and (not emit_aux or (tm % PAGE_ROWS == 0 and seq % tm == 0
                                             and -(-seq // SLC_BLOCK) <= HEAD_DIM))
    out_specs = [pl.BlockSpec((tm * ROW_SPLIT, HEAD_DIM), lambda i: (i, 0))] * N_BRANCH
    out_shape = [jax.ShapeDtypeStruct((M * ROW_SPLIT, HEAD_DIM), F32)] * N_BRANCH
    if emit_aux:
        tiles = tm // PAGE_ROWS
        for kw in (2 * HEAD_DIM, HEAD_DIM):
            out_specs += [pl.BlockSpec((N_KV_HEADS, tm, kw), lambda i: (0, i, 0)),
                          pl.BlockSpec((N_KV_HEADS, tiles, HEAD_DIM, PAGE_ROWS), lambda i: (0, i, 0, 0))]
            out_shape += [jax.ShapeDtypeStruct((N_KV_HEADS, M, kw), BF16),
                          jax.ShapeDtypeStruct((N_KV_HEADS, M // PAGE_ROWS, HEAD_DIM, PAGE_ROWS), BF16)]
    return pl.pallas_call(
        functools.partial(_kv_proj_kernel, emit_aux=emit_aux, seq=seq),
        grid=(M // tm,),
        in_specs=[pl.BlockSpec((tm, D), lambda i: (i, 0)), pl.BlockSpec((1, D), lambda i: (0, 0))]
        + [pl.BlockSpec(w.shape, lambda i: (0, 0)) for w in weights],
        out_specs=out_specs,
        out_shape=out_shape,
        compiler_params=_params("parallel"),
        name="kv_proj",
    )(x, g.reshape(1, D), *weights)


def _q_proj_kernel(x_ref, g_ref, wq_ref, wg_ref, q_ref, gate_ref, *, transposed):
    tm = x_ref.shape[0]
    xn = (_rms(x_ref[...]) * g_ref[...]).astype(BF16)
    q = _dot(xn, wq_ref[...]) * (HEAD_DIM ** -0.5 * (LOG2E if transposed else 1.0))
    gates = jax.nn.sigmoid(_dot(xn, wg_ref[...]))
    if not transposed:
        q_ref[...] = q.astype(BF16)
        gate_ref[...] = gates
        return
    for r in range(tm // PAGE_ROWS):
        rows = slice(r * PAGE_ROWS, (r + 1) * PAGE_ROWS)
        gt = gates[rows].T
        for hk in range(N_KV_HEADS):
            gate_ref[hk, r] = gt[hk * GATE_PAD:(hk + 1) * GATE_PAD]
            for g in range(GQA_GROUP):
                col = (hk * GQA_GROUP + g) * HEAD_DIM
                q_ref[hk, r, :, g * PAGE_ROWS:(g + 1) * PAGE_ROWS] = q[rows, col:col + HEAD_DIM].T.astype(BF16)


def _q_proj(x, g, wq, wg, transposed):
    M, D = x.shape
    tm = min(M, 256)
    assert M % tm == 0 and (tm % PAGE_ROWS == 0 or not transposed)
    tiles = tm // PAGE_ROWS
    gw = GQA_GROUP * PAGE_ROWS
    if transposed:
        out_specs = [pl.BlockSpec((N_KV_HEADS, tiles, HEAD_DIM, gw), lambda i: (0, i, 0, 0)),
                     pl.BlockSpec((N_KV_HEADS, tiles, GATE_PAD, PAGE_ROWS), lambda i: (0, i, 0, 0))]
        out_shape = [jax.ShapeDtypeStruct((N_KV_HEADS, M // PAGE_ROWS, HEAD_DIM, gw), BF16),
                     jax.ShapeDtypeStruct((N_KV_HEADS, M // PAGE_ROWS, GATE_PAD, PAGE_ROWS), F32)]
    else:
        out_specs = [pl.BlockSpec((tm, wq.shape[1]), lambda i: (i, 0)),
                     pl.BlockSpec((tm, wg.shape[1]), lambda i: (i, 0))]
        out_shape = [jax.ShapeDtypeStruct((M, wq.shape[1]), BF16), jax.ShapeDtypeStruct((M, wg.shape[1]), F32)]
    return pl.pallas_call(
        functools.partial(_q_proj_kernel, transposed=transposed),
        grid=(M // tm,),
        in_specs=[pl.BlockSpec((tm, D), lambda i: (i, 0)), pl.BlockSpec((1, D), lambda i: (0, 0)),
                  pl.BlockSpec(wq.shape, lambda i: (0, 0)), pl.BlockSpec(wg.shape, lambda i: (0, 0))],
        out_specs=out_specs,
        out_shape=out_shape,
        compiler_params=_params("parallel"),
        name="q_proj",
    )(x, g.reshape(1, D), wq, wg)


def _gelu_tanh(x):
    return 0.5 * x * (1.0 + jnp.tanh(math.sqrt(2.0 / math.pi) * (x + 0.044715 * (x * x * x))))


def _cmp_bias_kernel(pos_ref, w1_ref, b1_ref, o_ref):
    for c in range(2):
        acc = jnp.zeros((8, HEAD_DIM), F32)
        for j in range(CMP_BLOCK):
            pj = jnp.broadcast_to(pos_ref[c, j:j + 1, :], (8, HEAD_DIM)).astype(BF16)
            acc = acc + _dot(pj, w1_ref[c, j])
        o_ref[c] = acc + b1_ref[c]


def _cmp_bias(pos, w1, b1):
    return pl.pallas_call(
        _cmp_bias_kernel,
        out_shape=jax.ShapeDtypeStruct((2, 8, HEAD_DIM), F32),
        compiler_params=pltpu.CompilerParams(vmem_limit_bytes=VMEM_LIMIT_BYTES),
        name="cmp_bias",
    )(pos, w1, b1)


def _compress_kernel(pt_ref, *refs, n_pages):
    del pt_ref
    page_refs = refs[:n_pages]
    w1_ref, bias_ref, w2_ref, o_ref, scr_ref, sum_ref = refs[n_pages:]
    sub_per_page = PAGE_ROWS // CMP_STRIDE
    n_sub = n_pages * sub_per_page
    sub_rows = CMP_STRIDE * ROW_SPLIT
    pages_per_chunk = 4 if n_pages % 4 == 0 else 1
    chunk_rows = pages_per_chunk * sub_per_page * ROW_SPLIT
    is_k = (lax.broadcasted_iota(jnp.int32, (chunk_rows, 1), 0) & (ROW_SPLIT - 1)) < N_KV_HEADS
    bias = jnp.where(is_k[0:ROW_SPLIT], bias_ref[0], bias_ref[1])
    bias = jnp.concatenate([bias] * (chunk_rows // ROW_SPLIT), axis=0)

    def by_side(x):
        return jnp.concatenate([jnp.where(is_k, x, 0.0), jnp.where(is_k, 0.0, x)], axis=1).astype(BF16)

    for ck in range(n_pages // pages_per_chunk):
        chunk_pages = page_refs[ck * pages_per_chunk:(ck + 1) * pages_per_chunk]
        acc = jnp.zeros((chunk_rows, 2 * HEAD_DIM), F32)
        for j in range(CMP_STRIDE):
            x = jnp.concatenate([r[n * sub_rows + j * ROW_SPLIT:n * sub_rows + (j + 1) * ROW_SPLIT, :]
                                 for r in chunk_pages for n in range(sub_per_page)], axis=0)
            acc = acc + _dot(by_side(x), w1_ref[j])
        scr_ref[ck * chunk_rows:(ck + 1) * chunk_rows, :] = acc
    total = n_sub * ROW_SPLIT
    for ck in range(n_pages // pages_per_chunk):
        rows = slice(ck * chunk_rows, (ck + 1) * chunk_rows)
        first = scr_ref[rows, 0:HEAD_DIM]
        nxt_rows = slice(ck * chunk_rows + ROW_SPLIT, min((ck + 1) * chunk_rows + ROW_SPLIT, total))
        nxt = scr_ref[nxt_rows, HEAD_DIM:2 * HEAD_DIM]
        if nxt.shape[0] < chunk_rows:
            nxt = jnp.concatenate([nxt, jnp.zeros((chunk_rows - nxt.shape[0], HEAD_DIM), F32)], axis=0)
        hid = _gelu_tanh(first + nxt + bias)
        out = _dot(by_side(hid), w2_ref[...])
        if (ck + 1) * chunk_rows == total:
            row = lax.broadcasted_iota(jnp.int32, (chunk_rows, 1), 0)
            out = jnp.where(row < chunk_rows - ROW_SPLIT, out, 0.0)
        sum_ref[rows, :] = out
    for ch in range(ROW_SPLIT):
        o_ref[0, :, ch * HEAD_DIM:(ch + 1) * HEAD_DIM] = sum_ref[pl.ds(ch, n_sub, stride=ROW_SPLIT), :]


def _compress(pages, table, w1cat, bias, w2cat):
    nb, n_pages = table.shape
    n_sub = n_pages * (PAGE_ROWS // CMP_STRIDE)
    page_rows2d = PAGE_ROWS * ROW_SPLIT
    page_specs = [pl.BlockSpec((page_rows2d, HEAD_DIM), functools.partial(lambda b, pt, p: (pt[b, p], 0), p=p))
                  for p in range(n_pages)]
    const = lambda shape: pl.BlockSpec(shape, lambda b, pt: (0,) * len(shape))
    grid_spec = pltpu.PrefetchScalarGridSpec(
        num_scalar_prefetch=1,
        grid=(nb,),
        in_specs=page_specs + [const(w1cat.shape), const(bias.shape), const(w2cat.shape)],
        out_specs=pl.BlockSpec((1, n_sub, KV_COLS), lambda b, pt: (b, 0, 0)),
        scratch_shapes=[pltpu.VMEM((n_sub * ROW_SPLIT, 2 * HEAD_DIM), F32),
                        pltpu.VMEM((n_sub * ROW_SPLIT, HEAD_DIM), F32)],
    )
    return pl.pallas_call(
        functools.partial(_compress_kernel, n_pages=n_pages),
        grid_spec=grid_spec,
        out_shape=jax.ShapeDtypeStruct((nb, n_sub, KV_COLS), F32),
        compiler_params=_params("parallel"),
        name="compress",
    )(table, *([pages] * n_pages), w1cat, bias, w2cat)


def _rel_bucket_np(dist):
    n = np.maximum(dist, 0)
    max_exact = N_BUCKETS // 2
    nf = np.maximum(n, 1).astype(np.float32)
    large = max_exact + (np.log(nf / np.float32(max_exact)) / np.float32(math.log(MAX_DISTANCE / max_exact))
                         * np.float32(N_BUCKETS - max_exact)).astype(np.int32)
    large = np.minimum(large, N_BUCKETS - 1)
    return np.where(n < max_exact, n, large).astype(np.int32)


def _bias_table_kernel(rb_ref, bk_ref, o_ref, *, minus_far, scale):
    bk = bk_ref[...]
    for h in range(N_HEADS):
        v = jnp.full(bk.shape, rb_ref[0, h], F32)
        for b in range(1, N_BUCKETS):
            v = jnp.where(bk == b, rb_ref[b, h], v)
        if minus_far:
            v = v - rb_ref[N_BUCKETS - 1, h]
        o_ref[h] = v if scale == 1.0 else v * scale


def _bias_table(rel_bias, bucket, minus_far=False, scale=1.0):
    R, C = bucket.shape
    tr = min(R, 128)
    assert R % tr == 0
    return pl.pallas_call(
        functools.partial(_bias_table_kernel, minus_far=minus_far, scale=scale),
        grid=(R // tr,),
        in_specs=[pl.BlockSpec(memory_space=pltpu.SMEM), pl.BlockSpec((tr, C), lambda r: (r, 0))],
        out_specs=pl.BlockSpec((N_HEADS, tr, C), lambda r: (0, r, 0)),
        out_shape=jax.ShapeDtypeStruct((N_HEADS, R, C), F32),
        compiler_params=_params("parallel"),
        name="bias_table",
    )(rel_bias, jnp.asarray(bucket))


def _overlap_np(n_sub, n_slc):
    n_cmp = n_sub - 1
    cs = np.arange(n_cmp) * CMP_STRIDE
    ce = cs + CMP_BLOCK - 1
    ss = np.arange(n_slc) * SLC_BLOCK
    se = ss + SLC_BLOCK - 1
    ov = np.minimum(ce[:, None], se[None, :]) - np.maximum(cs[:, None], ss[None, :]) + 1
    out = np.zeros((n_sub, n_slc), np.float32)
    out[:n_cmp] = np.maximum(ov, 0).astype(np.float32) / CMP_BLOCK
    return out


def _pad_to(a, shape):
    return np.pad(a, [(0, s - d) for d, s in zip(a.shape, shape)])


def _score(imp, j, qb, n_slc):
    prio = 2.0 * (j == qb).astype(F32) + (j == qb - 1).astype(F32) + (j == 0).astype(F32)
    score = jnp.where(j <= qb, imp + SEL_BIG * prio, -SEL_BIG)
    return jnp.where(j < n_slc, score, -3e38)


def _attn_prompt_kernel(qt_ref, gt_ref, ck_ref, cv_ref, sk_ref, svt_ref, wk_ref, wvt_ref,
                        toe_ref, cmpb_ref, ovt_ref, o_ref,
                        m_ref, l_ref, acc_ref, accw_ref, *, n_slc, n_sub, hps):
    tq = PAGE_ROWS
    i = pl.program_id(2)
    heads = range(hps)
    t_idx = lax.broadcasted_iota(jnp.int32, (1, tq), 1)
    qpos = i * tq + t_idx
    lane4 = lambda x: jnp.concatenate([x] * GQA_GROUP, axis=1)
    n_idx = lax.broadcasted_iota(jnp.int32, (n_sub, 1), 0)
    valid = lane4((n_idx * CMP_STRIDE + (CMP_BLOCK - 1) <= qpos) & (n_idx < n_sub - 1))
    nsp = ovt_ref.shape[0]
    j = lax.broadcasted_iota(jnp.int32, (nsp, 1), 0)
    qb = qpos >> int(math.log2(SLC_BLOCK))

    qts, qt_sels, o_cs = [], [], []
    for hh in heads:
        qt = qt_ref[hh, 0]
        cols = slice(hh * HEAD_DIM, (hh + 1) * HEAD_DIM)
        kc = ck_ref[0, :, cols].astype(BF16)
        vct = cv_ref[0, :, cols].T.astype(BF16)
        bias_c = jnp.concatenate([cmpb_ref[hh * GQA_GROUP + g, 0] for g in range(GQA_GROUP)], axis=1)
        logits = jnp.where(valid, _dot(kc, qt) + bias_c, NEG_INF)
        p = jnp.exp2(logits - jnp.max(logits, axis=0, keepdims=True))
        p = p * (1.0 / jnp.sum(p, axis=0, keepdims=True))
        p = jnp.where(valid, p, 0.0)
        o_cs.append(_dot(vct, p.astype(BF16)))
        psum = p[:, 0:tq]
        for g in range(1, GQA_GROUP):
            psum = psum + p[:, g * tq:(g + 1) * tq]
        imp = jnp.dot(ovt_ref[...], psum, precision=lax.Precision.HIGHEST, preferred_element_type=F32)
        score = _score(imp, j, qb, n_slc)
        rank = jnp.zeros((nsp, tq), jnp.int32)
        for jj in range(n_slc):
            row = score[jj:jj + 1, :]
            rank = rank + ((row > score) | ((row == score) & (jj < j))).astype(jnp.int32)
        madd = jnp.where(rank < min(N_SELECT, n_slc), 0.0, NEG_INF)
        qts.append(qt)
        qt_sels.append(jnp.concatenate([qt, lane4(madd).astype(BF16),
                                        jnp.zeros((HEAD_DIM - nsp, GQA_GROUP * tq), BF16)], axis=0))

    k_row = lax.broadcasted_iota(jnp.int32, (tq, 1), 0)

    def chunk(hh, k, kinds, k_ref, vt_ref, q_op, out_ref):
        nt = len(kinds)
        k0 = pl.multiple_of(k * tq, tq)
        s = _dot(k_ref[hh, pl.ds(k0, nt * tq), :], q_op)
        parts = []
        for jx, kind in enumerate(kinds):
            sj = s[jx * tq:(jx + 1) * tq]
            if kind in ("near", "diag"):
                d = 0 if kind == "diag" else 1
                sj = sj + jnp.concatenate([toe_ref[(hh * GQA_GROUP + g) * 2 + d] for g in range(GQA_GROUP)],
                                          axis=1)
            if kind == "diag":
                sj = jnp.where(lane4(k_row <= t_idx), sj, NEG_INF)
            elif kind == "edge":
                sj = jnp.where(lane4(k_row > t_idx), sj, NEG_INF)
            parts.append(sj)
        m_new = functools.reduce(jnp.maximum, parts)
        m_new = jnp.max(m_new, axis=0, keepdims=True)
        if out_ref is None:
            m_old = m_ref[hh]
            m_new = jnp.maximum(m_old, m_new)
            alpha = jnp.exp2(m_old - m_new)
        ps = [jnp.exp2(sj - m_new) for sj in parts]
        lsum = jnp.sum(functools.reduce(jnp.add, ps), axis=0, keepdims=True)
        vt = jnp.concatenate([vt_ref[hh, k + jx] for jx in range(nt)], axis=1)
        pv = _dot(vt, jnp.concatenate(ps, axis=0).astype(BF16))
        if out_ref is None:
            l_ref[hh] = alpha * l_ref[hh] + lsum
            acc_ref[hh] = alpha * acc_ref[hh] + pv
            m_ref[hh] = m_new
        else:
            out_ref[hh] = pv * (1.0 / lsum)

    def tail_kinds(n_far):
        return ("far",) * n_far + ("near", "diag")

    far_step = SEL_FAR_TILES
    wt = WINDOW // tq
    m_ref[...] = jnp.full(m_ref.shape, NEG_INF, F32)
    l_ref[...] = jnp.zeros(l_ref.shape, F32)
    acc_ref[...] = jnp.zeros(acc_ref.shape, F32)

    n_far = jnp.maximum(i - 1, 0) // far_step

    def far_body(c, carry):
        for hh in heads:
            chunk(hh, c * far_step, ("far",) * far_step, sk_ref, svt_ref, qt_sels[hh], None)
        return carry

    lax.fori_loop(0, n_far, far_body, 0)

    for c in range(wt):
        @pl.when(i == c)
        def _(c=c):
            kinds = tail_kinds(c - 1) if c >= 1 else ("diag",)
            for hh in heads:
                chunk(hh, 0, kinds, sk_ref, svt_ref, qt_sels[hh], None)
                chunk(hh, 0, kinds, wk_ref, wvt_ref, qts[hh], accw_ref)

    for r in range(far_step):
        @pl.when((i >= wt) & (lax.rem(i + far_step - 1, far_step) == r))
        def _(r=r):
            for hh in heads:
                chunk(hh, i - 1 - r, tail_kinds(r), sk_ref, svt_ref, qt_sels[hh], None)
                chunk(hh, i - wt, ("edge",) + tail_kinds(wt - 2), wk_ref, wvt_ref, qts[hh], accw_ref)

    for hh in heads:
        o_s = acc_ref[hh] * (1.0 / l_ref[hh])
        o_w = accw_ref[hh]
        o_c = o_cs[hh]
        gt = gt_ref[hh, 0]
        for g in range(GQA_GROUP):
            lanes = slice(g * tq, (g + 1) * tq)
            o = (gt[g:g + 1] * o_c[:, lanes] + gt[GQA_GROUP + g:GQA_GROUP + g + 1] * o_s[:, lanes]
                 + gt[2 * GQA_GROUP + g:2 * GQA_GROUP + g + 1] * o_w[:, lanes])
            col = (hh * GQA_GROUP + g) * HEAD_DIM
            o_ref[0, :, col:col + HEAD_DIM] = o.T.astype(o_ref.dtype)


def _attn_prompt(qt, gt, ckv, sk, svt, wk, wvt, toe, cmpb, B, S):
    tq = PAGE_ROWS
    nq = S // tq
    n_sub = ckv.shape[1]
    n_slc = -(-S // SLC_BLOCK)
    nsp = -(-n_slc // 16) * 16
    assert S % tq == 0 and nsp <= HEAD_DIM and nq > WINDOW // tq
    ovt = jnp.asarray(_pad_to(_overlap_np(n_sub, n_slc).T, (nsp, n_sub)))
    gw = GQA_GROUP * tq
    hps = PROMPT_HEADS_PER_STEP
    assert N_KV_HEADS % hps == 0
    return pl.pallas_call(
        functools.partial(_attn_prompt_kernel, n_slc=n_slc, n_sub=n_sub, hps=hps),
        grid=(B, N_KV_HEADS // hps, nq),
        in_specs=[pl.BlockSpec((hps, 1, HEAD_DIM, gw), lambda b, h, i: (h, b * nq + i, 0, 0)),
                  pl.BlockSpec((hps, 1, GATE_PAD, tq), lambda b, h, i: (h, b * nq + i, 0, 0)),
                  pl.BlockSpec((1, n_sub, hps * HEAD_DIM), lambda b, h, i: (b, 0, h)),
                  pl.BlockSpec((1, n_sub, hps * HEAD_DIM), lambda b, h, i: (b, 0, N_KV_HEADS // hps + h)),
                  pl.BlockSpec((hps, S, 2 * HEAD_DIM), lambda b, h, i: (h, b, 0)),
                  pl.BlockSpec((hps, nq, HEAD_DIM, tq), lambda b, h, i: (h, b, 0, 0)),
                  pl.BlockSpec((hps, S, HEAD_DIM), lambda b, h, i: (h, b, 0)),
                  pl.BlockSpec((hps, nq, HEAD_DIM, tq), lambda b, h, i: (h, b, 0, 0)),
                  pl.BlockSpec((hps * GQA_GROUP * 2, tq, tq), lambda b, h, i: (h, 0, 0)),
                  pl.BlockSpec((hps * GQA_GROUP, 1, n_sub, tq), lambda b, h, i: (h, i, 0, 0)),
                  pl.BlockSpec(ovt.shape, lambda b, h, i: (0, 0))],
        out_specs=pl.BlockSpec((1, tq, hps * GQA_GROUP * HEAD_DIM), lambda b, h, i: (b, i, h)),
        out_shape=jax.ShapeDtypeStruct((B, S, N_HEADS * HEAD_DIM), BF16),
        scratch_shapes=[pltpu.VMEM((hps, 1, gw), F32), pltpu.VMEM((hps, 1, gw), F32),
                        pltpu.VMEM((hps, HEAD_DIM, gw), F32), pltpu.VMEM((hps, HEAD_DIM, gw), F32)],
        compiler_params=_params("parallel", "parallel", "arbitrary"),
        name="attn_prompt",
    )(qt, gt, ckv, ckv, sk, svt, wk, wvt, toe, cmpb, ovt)


def _attn_sample_kernel(pt_ref, *refs, n_pages, n_slc, n_sub, past_len, win_buf, ds):
    del pt_ref
    page_refs = refs[:n_pages]
    (q_ref, gate_ref, ckv_ref, snew_ref, wst_ref, wnew_ref, bias_ref, ov_ref, e_ref,
     o_ref, wout_ref) = refs[n_pages:]
    nk_s = (n_pages + 1) * PAGE_ROWS
    nk_w = win_buf + PAGE_ROWS
    head_col = lambda r, col, n: r[pl.ds(col, n, stride=ROW_SPLIT), :]

    shift = ds * ROW_SPLIT
    wout_ref[0:win_buf * ROW_SPLIT - shift, :] = wst_ref[shift:win_buf * ROW_SPLIT, :]
    wout_ref[win_buf * ROW_SPLIT - shift:win_buf * ROW_SPLIT, :] = wnew_ref[...]

    t_idx = lax.broadcasted_iota(jnp.int32, (ds, 1), 0)
    qpos = past_len + t_idx
    pad = jnp.zeros((PAGE_ROWS - ds, HEAD_DIM), F32)
    gates = gate_ref[0]
    kv_heads = range(N_KV_HEADS)
    head_rows = GQA_GROUP * ds
    per_head = lambda x, reps: jnp.concatenate([x] * reps, axis=0)
    new_rows = lambda r, col: jnp.concatenate([r[pl.ds(col, ds, stride=ROW_SPLIT), :], pad], axis=0).astype(BF16)

    ks_pos = lax.broadcasted_iota(jnp.int32, (1, nk_s), 1)
    dist_w = win_buf + t_idx - lax.broadcasted_iota(jnp.int32, (1, nk_w), 1)
    ok_w = per_head((dist_w >= 0) & (dist_w < WINDOW), N_HEADS)
    n_idx = lax.broadcasted_iota(jnp.int32, (1, n_sub), 1)
    ok_c = per_head((n_idx * CMP_STRIDE + (CMP_BLOCK - 1) <= qpos) & (n_idx < n_sub - 1), N_HEADS)
    lanes = ov_ref.shape[1]
    j = lax.broadcasted_iota(jnp.int32, (1, lanes), 1)
    qb = per_head(qpos >> int(math.log2(SLC_BLOCK)), N_KV_HEADS)

    def softmax(logits, ok):
        logits = jnp.where(ok, logits, NEG_INF)
        p = jnp.exp(logits - jnp.max(logits, axis=-1, keepdims=True))
        return p / jnp.sum(p, axis=-1, keepdims=True)

    bias = bias_ref[...].reshape(N_HEADS * ds, bias_ref.shape[2])
    qs = [jnp.concatenate([q_ref[0, :, (hk * GQA_GROUP + g) * HEAD_DIM:(hk * GQA_GROUP + g + 1) * HEAD_DIM]
                           for g in range(GQA_GROUP)], axis=0) for hk in kv_heads]
    kcols = [slice(hk * HEAD_DIM, (hk + 1) * HEAD_DIM) for hk in kv_heads]
    vcols = [slice((N_KV_HEADS + hk) * HEAD_DIM, (N_KV_HEADS + hk + 1) * HEAD_DIM) for hk in kv_heads]

    def attend(k_past, v_past, k_new, v_new, bias_cols, ok):
        n = k_past(0).shape[0]
        logits = jnp.concatenate(
            [jnp.concatenate([_dot_nt(qs[hk], k_past(hk).astype(BF16)), _dot_nt(qs[hk], k_new(hk))], axis=1)
             for hk in kv_heads], axis=0) + bias_cols
        p = softmax(logits, ok).astype(BF16)
        return [_dot(p[hk * head_rows:(hk + 1) * head_rows, 0:n], v_past(hk).astype(BF16))
                + _dot(p[hk * head_rows:(hk + 1) * head_rows, n:], v_new(hk)) for hk in kv_heads]

    logits_c = jnp.concatenate([_dot_nt(qs[hk], ckv_ref[0, :, kcols[hk]].astype(BF16)) for hk in kv_heads], axis=0)
    p_c = jnp.where(ok_c, softmax(logits_c + bias[:, nk_s + nk_w:nk_s + nk_w + n_sub], ok_c), 0.0)
    o_c = [_dot(p_c[hk * head_rows:(hk + 1) * head_rows].astype(BF16), ckv_ref[0, :, vcols[hk]].astype(BF16))
           for hk in kv_heads]
    psum = jnp.concatenate(
        [functools.reduce(jnp.add, [p_c[(hk * GQA_GROUP + g) * ds:(hk * GQA_GROUP + g + 1) * ds]
                                    for g in range(GQA_GROUP)]) for hk in kv_heads], axis=0)
    imp = jnp.dot(psum, ov_ref[...], precision=lax.Precision.HIGHEST, preferred_element_type=F32)
    score = _score(imp, j, qb, n_slc)
    rank = jnp.zeros(score.shape, jnp.int32)
    for jj in range(n_slc):
        col = score[:, jj:jj + 1]
        rank = rank + ((col > score) | ((col == score) & (jj < j))).astype(jnp.int32)
    sel = ((rank < min(N_SELECT, n_slc)) & (j < n_slc)).astype(BF16)
    allowed = (_dot(sel, e_ref[...]) > 0.5) & (ks_pos <= per_head(qpos, N_KV_HEADS))
    ok_s = jnp.concatenate([per_head(allowed[hk * ds:(hk + 1) * ds], GQA_GROUP) for hk in kv_heads], axis=0)

    past_s = lambda col: jnp.concatenate([head_col(r, col, PAGE_ROWS) for r in page_refs], axis=0)
    o_s = attend(lambda hk: past_s(hk), lambda hk: past_s(N_KV_HEADS + hk),
                 lambda hk: new_rows(snew_ref, hk), lambda hk: new_rows(snew_ref, N_KV_HEADS + hk),
                 bias[:, 0:nk_s], ok_s)
    o_w = attend(lambda hk: head_col(wst_ref, hk, win_buf), lambda hk: head_col(wst_ref, N_KV_HEADS + hk, win_buf),
                 lambda hk: new_rows(wnew_ref, hk), lambda hk: new_rows(wnew_ref, N_KV_HEADS + hk),
                 bias[:, nk_s:nk_s + nk_w], ok_w)

    for hk in kv_heads:
        gcol = hk * GATE_PAD
        for g in range(GQA_GROUP):
            rows = slice(g * ds, (g + 1) * ds)
            gate = lambda br, g=g: gates[:, gcol + br * GQA_GROUP + g:gcol + br * GQA_GROUP + g + 1]
            o = gate(0) * o_c[hk][rows] + gate(1) * o_s[hk][rows] + gate(2) * o_w[hk][rows]
            col = (hk * GQA_GROUP + g) * HEAD_DIM
            o_ref[0, :, col:col + HEAD_DIM] = o.astype(o_ref.dtype)


def _attn_sample(q, gates, ckv, cache_slc, table, slc_new, win_state, win_new, bias, past_len, win_buf):
    DB, ds, _ = q.shape
    n_pages = table.shape[1]
    n_sub = ckv.shape[1]
    n_slc = -(-(past_len + ds) // SLC_BLOCK)
    lanes = PAGE_ROWS
    assert ds % 8 == 0 and ds <= PAGE_ROWS and n_slc <= lanes and n_sub <= lanes
    assert win_buf <= past_len and win_buf > ds
    nk_s = (n_pages + 1) * PAGE_ROWS
    ov = jnp.asarray(_pad_to(_overlap_np(n_sub, n_slc), (n_sub, lanes)))
    e = (np.arange(nk_s)[None, :] // SLC_BLOCK == np.arange(lanes)[:, None]) & (np.arange(lanes)[:, None] < n_slc)
    e = jnp.asarray(e.astype(np.float32), dtype=BF16)
    page_specs = [pl.BlockSpec((PAGE_ROWS * ROW_SPLIT, HEAD_DIM),
                               functools.partial(lambda b, pt, p: (pt[b, p], 0), p=p)) for p in range(n_pages)]
    per_b3 = lambda shape: pl.BlockSpec((1,) + shape[1:], lambda b, pt: (b, 0, 0))
    rows_b = lambda n: pl.BlockSpec((n * ROW_SPLIT, HEAD_DIM), lambda b, pt: (b, 0))
    const = lambda shape: pl.BlockSpec(shape, lambda b, pt: (0,) * len(shape))
    grid_spec = pltpu.PrefetchScalarGridSpec(
        num_scalar_prefetch=1,
        grid=(DB,),
        in_specs=page_specs + [per_b3(q.shape), per_b3(gates.shape), per_b3(ckv.shape), rows_b(ds),
                               rows_b(win_buf), rows_b(ds), const(bias.shape), const(ov.shape), const(e.shape)],
        out_specs=[per_b3(q.shape), rows_b(win_buf)],
    )
    return pl.pallas_call(
        functools.partial(_attn_sample_kernel, n_pages=n_pages, n_slc=n_slc, n_sub=n_sub,
                          past_len=past_len, win_buf=win_buf, ds=ds),
        grid_spec=grid_spec,
        out_shape=[jax.ShapeDtypeStruct(q.shape, BF16), jax.ShapeDtypeStruct(win_state.shape, F32)],
        compiler_params=_params("parallel"),
        name="attn_sample",
    )(table, *([cache_slc] * n_pages), q, gates, ckv, slc_new, win_state, win_new, bias, ov, e)


def _prompt_buckets(S, n_sub):
    tq = PAGE_ROWS
    k = np.arange(tq)[:, None]
    t = np.arange(tq)[None, :]
    toe = np.concatenate([_rel_bucket_np(t - k + d * tq) for d in range(2)], axis=0)
    kpos_c = np.arange(n_sub)[:, None] * CMP_STRIDE + CMP_BLOCK - 1
    cmp = np.concatenate([_rel_bucket_np(i * tq + t - kpos_c) for i in range(S // tq)], axis=0)
    return toe, cmp


def _sample_buckets(ds, past_len, n_pages, win_buf, n_sub):
    qpos = past_len + np.arange(ds)[:, None]
    ks = np.arange((n_pages + 1) * PAGE_ROWS)[None, :]
    kw = past_len - win_buf + np.arange(win_buf + PAGE_ROWS)[None, :]
    kc = np.arange(n_sub)[None, :] * CMP_STRIDE + CMP_BLOCK - 1
    return np.concatenate([_rel_bucket_np(qpos - ks), _rel_bucket_np(qpos - kw), _rel_bucket_np(qpos - kc)], axis=1)


def kernel(x_prompt, x_sample, state_pool, cache_cmp_kv, cache_slc_kv, state_win_kv, page_table, norms, w_pool,
           pool_scale, kv_norm, w_kv, cmp_pos, cmp_w1, cmp_b1, cmp_w2, w_q, w_o, rel_bias, w_ffn_in, w_ffn_out):
    B, S, D = x_prompt.shape
    DB, DS, _ = x_sample.shape
    n_pages = page_table.shape[1]
    past_len = n_pages * cache_slc_kv.shape[1]
    win_buf = state_win_kv.shape[1]
    assert norms.shape[0] == 2 and w_pool.shape[0] == 1 and w_q.shape[0] == 1, "one pooling + one attention layer"
    assert cache_slc_kv.shape[1] == PAGE_ROWS and S % PAGE_ROWS == 0
    assert past_len % CMP_STRIDE == 0 and DS < CMP_STRIDE and D == N_HEADS * HEAD_DIM
    assert S >= win_buf and N_BRANCH * GQA_GROUP <= GATE_PAD

    qdim = N_HEADS * HEAD_DIM
    w_pool_b = w_pool[0].astype(BF16)
    w_in_b = w_ffn_in.astype(BF16)
    w_out_b = w_ffn_out.astype(BF16)
    w_kv_b = [w_kv[:, k * KV_COLS:(k + 1) * KV_COLS].astype(BF16) for k in range(N_BRANCH)]
    w_qq = w_q[0, :, :qdim].astype(BF16)
    w_qg = w_q[0, :, qdim:].reshape(D, N_BRANCH, N_KV_HEADS, GQA_GROUP).transpose(0, 2, 1, 3)
    w_qg = jnp.pad(w_qg.reshape(D, N_KV_HEADS, N_BRANCH * GQA_GROUP),
                   ((0, 0), (0, 0), (0, GATE_PAD - N_BRANCH * GQA_GROUP)))
    w_qg = jnp.pad(w_qg.reshape(D, N_KV_HEADS * GATE_PAD), ((0, 0), (0, 128 - N_KV_HEADS * GATE_PAD))).astype(BF16)
    w_o_b = w_o[0].astype(BF16)
    w1_b = cmp_w1.astype(BF16)
    w1cat = w1_b.reshape(2, 2, CMP_STRIDE, HEAD_DIM, HEAD_DIM).transpose(2, 0, 3, 1, 4)
    w1cat = w1cat.reshape(CMP_STRIDE, 2 * HEAD_DIM, 2 * HEAD_DIM)
    w2cat = cmp_w2.astype(BF16).reshape(2 * HEAD_DIM, HEAD_DIM)
    cmp_bias = _cmp_bias(cmp_pos, w1_b, cmp_b1.reshape(2, 1, HEAD_DIM))

    def layer0(x, buf, pos0):
        Bx, Sx, _ = x.shape
        h, st = _pool_layer(x, buf, norms[0, 0], w_pool_b, pool_scale[0], norms[0, 1], pos0)
        return _ffn(h.reshape(Bx * Sx, D), norms[0, 2], norms[0, 3], w_in_b[0], w_out_b[0]), st

    def layer1_tail(h1, o):
        h2 = _proj_res(o, w_o_b, norms[1, 1], h1)
        return _ffn(h2, norms[1, 2], norms[1, 3], w_in_b[1], w_out_b[1])

    rows2d = lambda a: a.reshape(-1, HEAD_DIM)
    kv5 = lambda r, b, s: r.reshape(b, s, 2, N_KV_HEADS, HEAD_DIM)

    h1, st_p = layer0(x_prompt, jnp.zeros((B, POOL_BUF, D), F32), 0)
    cmp_p, slc_p, win_p, sk, svt, wk, wvt = _kv_proj(h1, kv_norm, w_kv_b, True, S)
    qt, gt = _q_proj(h1, norms[1, 0], w_qq, w_qg, True)
    pages_p = S // PAGE_ROWS
    ckv_p = _compress(cmp_p, jnp.arange(B * pages_p, dtype=jnp.int32).reshape(B, pages_p), w1cat, cmp_bias, w2cat)
    n_sub_p = ckv_p.shape[1]
    toe_bk, cmp_bk = _prompt_buckets(S, n_sub_p)
    toe = _bias_table(rel_bias, toe_bk, True, LOG2E).reshape(N_HEADS * 2, PAGE_ROWS, PAGE_ROWS)
    cmpb = _bias_table(rel_bias, cmp_bk, False, LOG2E).reshape(N_HEADS, pages_p, n_sub_p, PAGE_ROWS)
    o_p = _attn_prompt(qt, gt, ckv_p, sk, svt, wk, wvt, toe, cmpb, B, S)
    y_prompt = layer1_tail(h1, o_p.reshape(B * S, qdim)).reshape(B, S, D)
    win_state_p = win_p.reshape(B, S * ROW_SPLIT, HEAD_DIM)[:, (S - win_buf) * ROW_SPLIT:]

    h1s, st_s = layer0(x_sample, state_pool[0], past_len)
    cmp_s, slc_s, win_s = _kv_proj(h1s, kv_norm, w_kv_b, False, DS)
    q_s, gates_s = _q_proj(h1s, norms[1, 0], w_qq, w_qg, False)
    ckv_s = _compress(rows2d(cache_cmp_kv), page_table, w1cat, cmp_bias, w2cat)
    bias_s = _bias_table(rel_bias, _sample_buckets(DS, past_len, n_pages, win_buf, ckv_s.shape[1]))
    o_s, win_state_s = _attn_sample(
        q_s.reshape(DB, DS, qdim), gates_s.reshape(DB, DS, 128), ckv_s, rows2d(cache_slc_kv), page_table,
        slc_s, rows2d(state_win_kv), win_s, bias_s, past_len, win_buf)
    y_sample = layer1_tail(h1s, o_s.reshape(DB * DS, qdim)).reshape(DB, DS, D)

    return (y_prompt, y_sample, st_p[None], st_s[None],
            kv5(cmp_p, B, S), kv5(cmp_s, DB, DS), kv5(slc_p, B, S), kv5(slc_s, DB, DS),
            kv5(win_state_p, B, win_buf), kv5(win_state_s, DB, win_buf))
```

```python
import functools
import math

import numpy as np
import jax
import jax.numpy as jnp
from jax import lax
from jax.experimental import pallas as pl
from jax.experimental.pallas import tpu as pltpu

F32 = jnp.float32
BF16 = jnp.bfloat16

HEAD_DIM = 128
N_KV_HEADS = 4
GQA_GROUP = 4
N_HEADS = N_KV_HEADS * GQA_GROUP
N_BRANCH = 3
POOL_WINDOWS = (2, 4, 8, 16)
POOL_BUF = max(POOL_WINDOWS) - 1
POOL_HALO = POOL_BUF + 1
CMP_BLOCK = 32
CMP_STRIDE = 16
SLC_BLOCK = 64
N_SELECT = 16
WINDOW = 512
N_BUCKETS = 32
MAX_DISTANCE = 128
RMS_EPS = 1e-6
SEL_BIG = 1e3
NEG_INF = -1e30
LOG2E = math.log2(math.e)
PAGE_ROWS = 128
KV_COLS = 2 * N_KV_HEADS * HEAD_DIM
ROW_SPLIT = 2 * N_KV_HEADS
GATE_PAD = 16
SEL_FAR_TILES = 4
PROMPT_HEADS_PER_STEP = 4

VMEM_LIMIT_BYTES = 56 * 1024 * 1024

_NT = (((1,), (1,)), ((), ()))


def _params(*sem):
    return pltpu.CompilerParams(dimension_semantics=sem, vmem_limit_bytes=VMEM_LIMIT_BYTES)


def _rms(x):
    return x * lax.rsqrt(jnp.mean(x * x, axis=-1, keepdims=True) + RMS_EPS)


def _dot(a, b):
    return jnp.dot(a, b, preferred_element_type=F32)


def _dot_nt(a, b):
    return lax.dot_general(a, b, _NT, preferred_element_type=F32)


def _pool_layer_kernel(h_ref, buf_ref, g0_ref, w_ref, scale_ref, g1_ref, o_ref, st_ref, ext_ref, *, ts, pos0):
    i = pl.program_id(1)
    nb, _, D = h_ref.shape
    groups = w_ref.shape[0]
    gd = D // groups

    @pl.when(i == 0)
    def _():
        ext_ref[:, 0:POOL_HALO, :] = buf_ref[...]

    @pl.when(i > 0)
    def _():
        ext_ref[:, 0:POOL_HALO, :] = ext_ref[:, ts:ts + POOL_HALO, :]

    x = h_ref[...]
    ext_ref[:, POOL_HALO:POOL_HALO + ts, :] = _rms(x) * g0_ref[...]
    pos = pos0 + i * ts + lax.broadcasted_iota(jnp.int32, (1, ts, 1), 1)
    parts = []
    for gi, w in enumerate(POOL_WINDOWS):
        cols = slice(gi * gd, (gi + 1) * gd)
        s = ext_ref[:, :, cols]
        cur = s[:, POOL_HALO:POOL_HALO + ts]
        span = 1
        while span < w:
            s = s + pltpu.roll(s, span, 1)
            span *= 2
        s = s[:, POOL_HALO:POOL_HALO + ts]
        inv_cnt = 1.0 / jnp.minimum(w, pos + 1).astype(F32)
        mix = (s * inv_cnt - cur).astype(BF16).reshape(nb * ts, gd)
        parts.append(_dot(mix, w_ref[gi]))
    y = jnp.concatenate(parts, axis=-1) * scale_ref[...]
    o_ref[...] = x + (_rms(y) * g1_ref[...]).reshape(nb, ts, D)
    st_ref[...] = ext_ref[:, ts:ts + POOL_HALO, :]


def _pool_layer(h, buf, g0, w, scale, g1, pos0):
    B, S, D = h.shape
    rows = 256
    ts = min(S, rows)
    nb = max(1, min(B, rows // ts)) if S == ts else 1
    assert S % ts == 0 and B % nb == 0 and (ts >= POOL_HALO or S == ts) and ts % 8 == 0
    assert w.shape[0] == len(POOL_WINDOWS) and all(pw & (pw - 1) == 0 for pw in POOL_WINDOWS)
    buf16 = jnp.pad(buf, ((0, 0), (POOL_HALO - POOL_BUF, 0), (0, 0)))
    vec = lambda: pl.BlockSpec((1, D), lambda b, i: (0, 0))
    out, st = pl.pallas_call(
        functools.partial(_pool_layer_kernel, ts=ts, pos0=pos0),
        grid=(B // nb, S // ts),
        in_specs=[pl.BlockSpec((nb, ts, D), lambda b, i: (b, i, 0)),
                  pl.BlockSpec((nb, POOL_HALO, D), lambda b, i: (b, 0, 0)),
                  vec(), pl.BlockSpec(w.shape, lambda b, i: (0, 0, 0)), vec(), vec()],
        out_specs=[pl.BlockSpec((nb, ts, D), lambda b, i: (b, i, 0)),
                   pl.BlockSpec((nb, POOL_HALO, D), lambda b, i: (b, 0, 0))],
        out_shape=[jax.ShapeDtypeStruct((B, S, D), F32),
                   jax.ShapeDtypeStruct((B, POOL_HALO, D), F32)],
        scratch_shapes=[pltpu.VMEM((nb, POOL_HALO + ts, D), F32)],
        compiler_params=_params("parallel", "arbitrary"),
        name="pool_layer",
    )(h, buf16, g0.reshape(1, D), w, scale.reshape(1, D), g1.reshape(1, D))
    return out, st[:, POOL_HALO - POOL_BUF:]


def _proj_res_kernel(a_ref, w_ref, g_ref, res_ref, o_ref):
    o_ref[...] = res_ref[...] + _rms(_dot(a_ref[...], w_ref[...])) * g_ref[...]


def _proj_res(a, w, g, res):
    M, D = a.shape
    tm = min(M, 256)
    assert M % tm == 0
    row = lambda i: (i, 0)
    fixed = lambda i: (0, 0)
    return pl.pallas_call(
        _proj_res_kernel,
        grid=(M // tm,),
        in_specs=[pl.BlockSpec((tm, D), row), pl.BlockSpec(w.shape, fixed), pl.BlockSpec((1, D), fixed),
                  pl.BlockSpec((tm, D), row)],
        out_specs=pl.BlockSpec((tm, D), row),
        out_shape=jax.ShapeDtypeStruct((M, D), F32),
        compiler_params=_params("parallel"),
        name="o_proj",
    )(a, w, g.reshape(1, D), res)


def _ffn_kernel(h_ref, gi_ref, go_ref, wg_ref, wu_ref, wo_ref, out_ref, xn_ref, acc_ref):
    f = pl.program_id(1)

    @pl.when(f == 0)
    def _():
        xn_ref[...] = (_rms(h_ref[...]) * gi_ref[...]).astype(xn_ref.dtype)
        acc_ref[...] = jnp.zeros_like(acc_ref)

    xn = xn_ref[...]
    gate = _dot(xn, wg_ref[...])
    up = _dot(xn, wu_ref[...])
    act = (gate * jax.nn.sigmoid(gate) * up).astype(BF16)
    acc_ref[...] += _dot(act, wo_ref[...])

    @pl.when(f == pl.num_programs(1) - 1)
    def _():
        out_ref[...] = h_ref[...] + _rms(acc_ref[...]) * go_ref[...]


def _ffn(h, g_in, g_out, w_in, w_out, layer):
    M, D = h.shape
    F = w_out.shape[1]
    tm = min(M, 512)
    tf = 512
    assert M % tm == 0 and F % tf == 0
    nf = F // tf
    return pl.pallas_call(
        _ffn_kernel,
        grid=(M // tm, nf),
        in_specs=[pl.BlockSpec((tm, D), lambda i, f: (i, 0)),
                  pl.BlockSpec((1, D), lambda i, f: (0, 0)),
                  pl.BlockSpec((1, D), lambda i, f: (0, 0)),
                  pl.BlockSpec((None, D, tf), lambda i, f: (layer, 0, f)),
                  pl.BlockSpec((None, D, tf), lambda i, f: (layer, 0, nf + f)),
                  pl.BlockSpec((None, tf, D), lambda i, f: (layer, f, 0))],
        out_specs=pl.BlockSpec((tm, D), lambda i, f: (i, 0)),
        out_shape=jax.ShapeDtypeStruct((M, D), F32),
        scratch_shapes=[pltpu.VMEM((tm, D), BF16), pltpu.VMEM((tm, D), F32)],
        compiler_params=_params("parallel", "arbitrary"),
        name="ffn",
    )(h, g_in.reshape(1, D), g_out.reshape(1, D), w_in, w_in, w_out)


def _kv_proj_kernel(x_ref, g_ref, *refs, emit_aux, seq):
    w_refs, out_refs = refs[:N_BRANCH], refs[N_BRANCH:]
    tm = x_ref.shape[0]
    xn = (_rms(x_ref[...]) * g_ref[...]).astype(BF16)
    for br in range(N_BRANCH):
        y = _dot(xn, w_refs[br][...])
        for j in range(ROW_SPLIT):
            out_refs[br][pl.ds(j, tm, stride=ROW_SPLIT), :] = y[:, j * HEAD_DIM:(j + 1) * HEAD_DIM]
        if emit_aux and br >= 1:
            kb_ref = out_refs[N_BRANCH + 2 * (br - 1)]
            vt_ref = out_refs[N_BRANCH + 2 * (br - 1) + 1]
            if br == 1:
                pos = lax.rem(pl.program_id(0) * tm, seq) + lax.broadcasted_iota(jnp.int32, (tm, 1), 0)
                col = lax.broadcasted_iota(jnp.int32, (1, HEAD_DIM), 1)
                onehot = (col == (pos >> int(math.log2(SLC_BLOCK)))).astype(BF16)
            for h in range(N_KV_HEADS):
                kh = y[:, h * HEAD_DIM:(h + 1) * HEAD_DIM].astype(BF16)
                kb_ref[h] = jnp.concatenate([kh, onehot], axis=1) if br == 1 else kh
                vcol = (N_KV_HEADS + h) * HEAD_DIM
                for r in range(tm // PAGE_ROWS):
                    v = y[r * PAGE_ROWS:(r + 1) * PAGE_ROWS, vcol:vcol + HEAD_DIM]
                    vt_ref[h, r] = v.T.astype(BF16)


def _kv_proj(x, g, weights, emit_aux, seq):
    M, D = x.shape
    tm = min(M, 256)
    assert M % tm == 0 and (not emit_aux or (tm % PAGE_ROWS == 0 and seq % tm == 0
                                             and -(-seq // SLC_BLOCK) <= HEAD_DIM))
    out_specs = [pl.BlockSpec((tm * ROW_SPLIT, HEAD_DIM), lambda i: (i, 0))] * N_BRANCH
    out_shape = [jax.ShapeDtypeStruct((M * ROW_SPLIT, HEAD_DIM), F32)] * N_BRANCH
    if emit_aux:
        tiles = tm // PAGE_ROWS
        for kw in (2 * HEAD_DIM, HEAD_DIM):
            out_specs += [pl.BlockSpec((N_KV_HEADS, tm, kw), lambda i: (0, i, 0)),
                          pl.BlockSpec((N_KV_HEADS, tiles, HEAD_DIM, PAGE_ROWS), lambda i: (0, i, 0, 0))]
            out_shape += [jax.ShapeDtypeStruct((N_KV_HEADS, M, kw), BF16),
                          jax.ShapeDtypeStruct((N_KV_HEADS, M // PAGE_ROWS, HEAD_DIM, PAGE_ROWS), BF16)]
    return pl.pallas_call(
        functools.partial(_kv_proj_kernel, emit_aux=emit_aux, seq=seq),
        grid=(M // tm,),
        in_specs=[pl.BlockSpec((tm, D), lambda i: (i, 0)), pl.BlockSpec((1, D), lambda i: (0, 0))]
        + [pl.BlockSpec((D, KV_COLS), functools.partial(lambda i, br: (0, br), br=br)) for br in range(N_BRANCH)],
        out_specs=out_specs,
        out_shape=out_shape,
        compiler_params=_params("parallel"),
        name="kv_proj",
    )(x, g.reshape(1, D), *([weights] * N_BRANCH))


def _q_proj_kernel(x_ref, g_ref, wq_ref, wg_ref, q_ref, gate_ref, *, transposed):
    tm = x_ref.shape[0]
    xn = (_rms(x_ref[...]) * g_ref[...]).astype(BF16)
    q = _dot(xn, wq_ref[...]) * (HEAD_DIM ** -0.5 * (LOG2E if transposed else 1.0))
    gates = jax.nn.sigmoid(_dot(xn, wg_ref[...]))
    if not transposed:
        q_ref[...] = q.astype(BF16)
        gate_ref[...] = gates
        return
    for r in range(tm // PAGE_ROWS):
        rows = slice(r * PAGE_ROWS, (r + 1) * PAGE_ROWS)
        gt = gates[rows].T
        for hk in range(N_KV_HEADS):
            gate_ref[hk, r] = gt[hk * GATE_PAD:(hk + 1) * GATE_PAD]
            for g in range(GQA_GROUP):
                col = (hk * GQA_GROUP + g) * HEAD_DIM
                q_ref[hk, r, :, g * PAGE_ROWS:(g + 1) * PAGE_ROWS] = q[rows, col:col + HEAD_DIM].T.astype(BF16)


def _q_proj(x, g, wq, wg, transposed):
    M, D = x.shape
    qdim = N_HEADS * HEAD_DIM
    tm = min(M, 256)
    assert M % tm == 0 and (tm % PAGE_ROWS == 0 or not transposed)
    tiles = tm // PAGE_ROWS
    gw = GQA_GROUP * PAGE_ROWS
    if transposed:
        out_specs = [pl.BlockSpec((N_KV_HEADS, tiles, HEAD_DIM, gw), lambda i: (0, i, 0, 0)),
                     pl.BlockSpec((N_KV_HEADS, tiles, GATE_PAD, PAGE_ROWS), lambda i: (0, i, 0, 0))]
        out_shape = [jax.ShapeDtypeStruct((N_KV_HEADS, M // PAGE_ROWS, HEAD_DIM, gw), BF16),
                     jax.ShapeDtypeStruct((N_KV_HEADS, M // PAGE_ROWS, GATE_PAD, PAGE_ROWS), F32)]
    else:
        out_specs = [pl.BlockSpec((tm, qdim), lambda i: (i, 0)),
                     pl.BlockSpec((tm, wg.shape[1]), lambda i: (i, 0))]
        out_shape = [jax.ShapeDtypeStruct((M, qdim), BF16), jax.ShapeDtypeStruct((M, wg.shape[1]), F32)]
    return pl.pallas_call(
        functools.partial(_q_proj_kernel, transposed=transposed),
        grid=(M // tm,),
        in_specs=[pl.BlockSpec((tm, D), lambda i: (i, 0)), pl.BlockSpec((1, D), lambda i: (0, 0)),
                  pl.BlockSpec((None, D, qdim), lambda i: (0, 0, 0)), pl.BlockSpec(wg.shape, lambda i: (0, 0))],
        out_specs=out_specs,
        out_shape=out_shape,
        compiler_params=_params("parallel"),
        name="q_proj",
    )(x, g.reshape(1, D), wq, wg)


def _gelu_tanh(x):
    return 0.5 * x * (1.0 + jnp.tanh(math.sqrt(2.0 / math.pi) * (x + 0.044715 * (x * x * x))))


def _cmp_bias_kernel(pos_ref, w1_ref, b1_ref, o_ref):
    for c in range(2):
        acc = jnp.zeros((8, HEAD_DIM), F32)
        for j in range(CMP_BLOCK):
            pj = jnp.broadcast_to(pos_ref[c, j:j + 1, :], (8, HEAD_DIM)).astype(BF16)
            acc = acc + _dot(pj, w1_ref[c, j])
        o_ref[c] = acc + b1_ref[c]


def _cmp_bias(pos, w1, b1):
    return pl.pallas_call(
        _cmp_bias_kernel,
        out_shape=jax.ShapeDtypeStruct((2, 8, HEAD_DIM), F32),
        compiler_params=pltpu.CompilerParams(vmem_limit_bytes=VMEM_LIMIT_BYTES),
        name="cmp_bias",
    )(pos, w1, b1)


def _compress_kernel(pt_ref, *refs, n_pages):
    del pt_ref
    page_refs = refs[:n_pages]
    w1_ref, bias_ref, w2_ref, o_ref, scr_ref, sum_ref = refs[n_pages:]
    sub_per_page = PAGE_ROWS // CMP_STRIDE
    n_sub = n_pages * sub_per_page
    sub_rows = CMP_STRIDE * ROW_SPLIT
    pages_per_chunk = 4 if n_pages % 4 == 0 else 1
    chunk_rows = pages_per_chunk * sub_per_page * ROW_SPLIT
    is_k = (lax.broadcasted_iota(jnp.int32, (chunk_rows, 1), 0) & (ROW_SPLIT - 1)) < N_KV_HEADS
    bias = jnp.where(is_k[0:ROW_SPLIT], bias_ref[0], bias_ref[1])
    bias = jnp.concatenate([bias] * (chunk_rows // ROW_SPLIT), axis=0)

    def by_side(x):
        return jnp.concatenate([jnp.where(is_k, x, 0.0), jnp.where(is_k, 0.0, x)], axis=1).astype(BF16)

    for ck in range(n_pages // pages_per_chunk):
        chunk_pages = page_refs[ck * pages_per_chunk:(ck + 1) * pages_per_chunk]
        acc = jnp.zeros((chunk_rows, 2 * HEAD_DIM), F32)
        for j in range(CMP_STRIDE):
            x = jnp.concatenate([r[n * sub_rows + j * ROW_SPLIT:n * sub_rows + (j + 1) * ROW_SPLIT, :]
                                 for r in chunk_pages for n in range(sub_per_page)], axis=0)
            acc = acc + _dot(by_side(x), w1_ref[j])
        scr_ref[ck * chunk_rows:(ck + 1) * chunk_rows, :] = acc
    total = n_sub * ROW_SPLIT
    for ck in range(n_pages // pages_per_chunk):
        rows = slice(ck * chunk_rows, (ck + 1) * chunk_rows)
        first = scr_ref[rows, 0:HEAD_DIM]
        nxt_rows = slice(ck * chunk_rows + ROW_SPLIT, min((ck + 1) * chunk_rows + ROW_SPLIT, total))
        nxt = scr_ref[nxt_rows, HEAD_DIM:2 * HEAD_DIM]
        if nxt.shape[0] < chunk_rows:
            nxt = jnp.concatenate([nxt, jnp.zeros((chunk_rows - nxt.shape[0], HEAD_DIM), F32)], axis=0)
        hid = _gelu_tanh(first + nxt + bias)
        out = _dot(by_side(hid), w2_ref[...])
        if (ck + 1) * chunk_rows == total:
            row = lax.broadcasted_iota(jnp.int32, (chunk_rows, 1), 0)
            out = jnp.where(row < chunk_rows - ROW_SPLIT, out, 0.0)
        sum_ref[rows, :] = out
    for ch in range(ROW_SPLIT):
        o_ref[0, :, ch * HEAD_DIM:(ch + 1) * HEAD_DIM] = sum_ref[pl.ds(ch, n_sub, stride=ROW_SPLIT), :]


def _compress(pages, table, w1cat, bias, w2cat):
    nb, n_pages = table.shape
    n_sub = n_pages * (PAGE_ROWS // CMP_STRIDE)
    page_rows2d = PAGE_ROWS * ROW_SPLIT
    page_specs = [pl.BlockSpec((page_rows2d, HEAD_DIM), functools.partial(lambda b, pt, p: (pt[b, p], 0), p=p))
                  for p in range(n_pages)]
    const = lambda shape: pl.BlockSpec(shape, lambda b, pt: (0,) * len(shape))
    grid_spec = pltpu.PrefetchScalarGridSpec(
        num_scalar_prefetch=1,
        grid=(nb,),
        in_specs=page_specs + [const(w1cat.shape), const(bias.shape), const(w2cat.shape)],
        out_specs=pl.BlockSpec((1, n_sub, KV_COLS), lambda b, pt: (b, 0, 0)),
        scratch_shapes=[pltpu.VMEM((n_sub * ROW_SPLIT, 2 * HEAD_DIM), F32),
                        pltpu.VMEM((n_sub * ROW_SPLIT, HEAD_DIM), F32)],
    )
    return pl.pallas_call(
        functools.partial(_compress_kernel, n_pages=n_pages),
        grid_spec=grid_spec,
        out_shape=jax.ShapeDtypeStruct((nb, n_sub, KV_COLS), F32),
        compiler_params=_params("parallel"),
        name="compress",
    )(table, *([pages] * n_pages), w1cat, bias, w2cat)


def _rel_bucket_np(dist):
    n = np.maximum(dist, 0)
    max_exact = N_BUCKETS // 2
    nf = np.maximum(n, 1).astype(np.float32)
    large = max_exact + (np.log(nf / np.float32(max_exact)) / np.float32(math.log(MAX_DISTANCE / max_exact))
                         * np.float32(N_BUCKETS - max_exact)).astype(np.int32)
    large = np.minimum(large, N_BUCKETS - 1)
    return np.where(n < max_exact, n, large).astype(np.int32)


def _bias_table_kernel(rb_ref, bk_ref, o_ref, *, minus_far, scale):
    bk = bk_ref[...]
    for h in range(N_HEADS):
        v = jnp.full(bk.shape, rb_ref[0, h], F32)
        for b in range(1, N_BUCKETS):
            v = jnp.where(bk == b, rb_ref[b, h], v)
        if minus_far:
            v = v - rb_ref[N_BUCKETS - 1, h]
        o_ref[h] = v if scale == 1.0 else v * scale


def _bias_table(rel_bias, bucket, minus_far=False, scale=1.0):
    R, C = bucket.shape
    tr = min(R, 128)
    assert R % tr == 0
    return pl.pallas_call(
        functools.partial(_bias_table_kernel, minus_far=minus_far, scale=scale),
        grid=(R // tr,),
        in_specs=[pl.BlockSpec(memory_space=pltpu.SMEM), pl.BlockSpec((tr, C), lambda r: (r, 0))],
        out_specs=pl.BlockSpec((N_HEADS, tr, C), lambda r: (0, r, 0)),
        out_shape=jax.ShapeDtypeStruct((N_HEADS, R, C), F32),
        compiler_params=_params("parallel"),
        name="bias_table",
    )(rel_bias, jnp.asarray(bucket))


def _overlap_np(n_sub, n_slc):
    n_cmp = n_sub - 1
    cs = np.arange(n_cmp) * CMP_STRIDE
    ce = cs + CMP_BLOCK - 1
    ss = np.arange(n_slc) * SLC_BLOCK
    se = ss + SLC_BLOCK - 1
    ov = np.minimum(ce[:, None], se[None, :]) - np.maximum(cs[:, None], ss[None, :]) + 1
    out = np.zeros((n_sub, n_slc), np.float32)
    out[:n_cmp] = np.maximum(ov, 0).astype(np.float32) / CMP_BLOCK
    return out


def _pad_to(a, shape):
    return np.pad(a, [(0, s - d) for d, s in zip(a.shape, shape)])


def _score(imp, j, qb, n_slc):
    prio = 2.0 * (j == qb).astype(F32) + (j == qb - 1).astype(F32) + (j == 0).astype(F32)
    score = jnp.where(j <= qb, imp + SEL_BIG * prio, -SEL_BIG)
    return jnp.where(j < n_slc, score, -3e38)


def _attn_prompt_kernel(qt_ref, gt_ref, ck_ref, cv_ref, sk_ref, svt_ref, wk_ref, wvt_ref,
                        toe_ref, cmpb_ref, ovt_ref, o_ref,
                        m_ref, l_ref, acc_ref, accw_ref, *, n_slc, n_sub, hps):
    tq = PAGE_ROWS
    i = pl.program_id(2)
    heads = range(hps)
    t_idx = lax.broadcasted_iota(jnp.int32, (1, tq), 1)
    qpos = i * tq + t_idx
    lane4 = lambda x: jnp.concatenate([x] * GQA_GROUP, axis=1)
    n_idx = lax.broadcasted_iota(jnp.int32, (n_sub, 1), 0)
    valid = lane4((n_idx * CMP_STRIDE + (CMP_BLOCK - 1) <= qpos) & (n_idx < n_sub - 1))
    nsp = ovt_ref.shape[0]
    j = lax.broadcasted_iota(jnp.int32, (nsp, 1), 0)
    qb = qpos >> int(math.log2(SLC_BLOCK))

    qts, qt_sels, o_cs = [], [], []
    for hh in heads:
        qt = qt_ref[hh, 0]
        cols = slice(hh * HEAD_DIM, (hh + 1) * HEAD_DIM)
        kc = ck_ref[0, :, cols].astype(BF16)
        vct = cv_ref[0, :, cols].T.astype(BF16)
        bias_c = jnp.concatenate([cmpb_ref[hh * GQA_GROUP + g, 0] for g in range(GQA_GROUP)], axis=1)
        logits = jnp.where(valid, _dot(kc, qt) + bias_c, NEG_INF)
        p = jnp.exp2(logits - jnp.max(logits, axis=0, keepdims=True))
        p = p * (1.0 / jnp.sum(p, axis=0, keepdims=True))
        p = jnp.where(valid, p, 0.0)
        o_cs.append(_dot(vct, p.astype(BF16)))
        psum = p[:, 0:tq]
        for g in range(1, GQA_GROUP):
            psum = psum + p[:, g * tq:(g + 1) * tq]
        imp = jnp.dot(ovt_ref[...], psum, precision=lax.Precision.HIGHEST, preferred_element_type=F32)
        score = _score(imp, j, qb, n_slc)
        rank = jnp.zeros((nsp, tq), jnp.int32)
        for jj in range(n_slc):
            row = score[jj:jj + 1, :]
            rank = rank + ((row > score) | ((row == score) & (jj < j))).astype(jnp.int32)
        madd = jnp.where(rank < min(N_SELECT, n_slc), 0.0, NEG_INF)
        qts.append(qt)
        qt_sels.append(jnp.concatenate([qt, lane4(madd).astype(BF16),
                                        jnp.zeros((HEAD_DIM - nsp, GQA_GROUP * tq), BF16)], axis=0))

    k_row = lax.broadcasted_iota(jnp.int32, (tq, 1), 0)

    def chunk(hh, k, kinds, k_ref, vt_ref, q_op, out_ref):
        nt = len(kinds)
        k0 = pl.multiple_of(k * tq, tq)
        s = _dot(k_ref[hh, pl.ds(k0, nt * tq), :], q_op)
        parts = []
        for jx, kind in enumerate(kinds):
            sj = s[jx * tq:(jx + 1) * tq]
            if kind in ("near", "diag"):
                d = 0 if kind == "diag" else 1
                sj = sj + jnp.concatenate([toe_ref[(hh * GQA_GROUP + g) * 2 + d] for g in range(GQA_GROUP)],
                                          axis=1)
            if kind == "diag":
                sj = jnp.where(lane4(k_row <= t_idx), sj, NEG_INF)
            elif kind == "edge":
                sj = jnp.where(lane4(k_row > t_idx), sj, NEG_INF)
            parts.append(sj)
        m_new = functools.reduce(jnp.maximum, parts)
        m_new = jnp.max(m_new, axis=0, keepdims=True)
        if out_ref is None:
            m_old = m_ref[hh]
            m_new = jnp.maximum(m_old, m_new)
            alpha = jnp.exp2(m_old - m_new)
        ps = [jnp.exp2(sj - m_new) for sj in parts]
        lsum = jnp.sum(functools.reduce(jnp.add, ps), axis=0, keepdims=True)
        vt = jnp.concatenate([vt_ref[hh, k + jx] for jx in range(nt)], axis=1)
        pv = _dot(vt, jnp.concatenate(ps, axis=0).astype(BF16))
        if out_ref is None:
            l_ref[hh] = alpha * l_ref[hh] + lsum
            acc_ref[hh] = alpha * acc_ref[hh] + pv
            m_ref[hh] = m_new
        else:
            out_ref[hh] = pv * (1.0 / lsum)

    def tail_kinds(n_far):
        return ("far",) * n_far + ("near", "diag")

    far_step = SEL_FAR_TILES
    wt = WINDOW // tq
    m_ref[...] = jnp.full(m_ref.shape, NEG_INF, F32)
    l_ref[...] = jnp.zeros(l_ref.shape, F32)
    acc_ref[...] = jnp.zeros(acc_ref.shape, F32)

    n_far = jnp.maximum(i - 1, 0) // far_step

    def far_body(c, carry):
        for hh in heads:
            chunk(hh, c * far_step, ("far",) * far_step, sk_ref, svt_ref, qt_sels[hh], None)
        return carry

    lax.fori_loop(0, n_far, far_body, 0)

    for c in range(wt):
        @pl.when(i == c)
        def _(c=c):
            kinds = tail_kinds(c - 1) if c >= 1 else ("diag",)
            for hh in heads:
                chunk(hh, 0, kinds, sk_ref, svt_ref, qt_sels[hh], None)
                chunk(hh, 0, kinds, wk_ref, wvt_ref, qts[hh], accw_ref)

    for r in range(far_step):
        @pl.when((i >= wt) & (lax.rem(i + far_step - 1, far_step) == r))
        def _(r=r):
            for hh in heads:
                chunk(hh, i - 1 - r, tail_kinds(r), sk_ref, svt_ref, qt_sels[hh], None)
                chunk(hh, i - wt, ("edge",) + tail_kinds(wt - 2), wk_ref, wvt_ref, qts[hh], accw_ref)

    for hh in heads:
        o_s = acc_ref[hh] * (1.0 / l_ref[hh])
        o_w = accw_ref[hh]
        o_c = o_cs[hh]
        gt = gt_ref[hh, 0]
        for g in range(GQA_GROUP):
            lanes = slice(g * tq, (g + 1) * tq)
            o = (gt[g:g + 1] * o_c[:, lanes] + gt[GQA_GROUP + g:GQA_GROUP + g + 1] * o_s[:, lanes]
                 + gt[2 * GQA_GROUP + g:2 * GQA_GROUP + g + 1] * o_w[:, lanes])
            col = (hh * GQA_GROUP + g) * HEAD_DIM
            o_ref[0, :, col:col + HEAD_DIM] = o.T.astype(o_ref.dtype)


def _attn_prompt(qt, gt, ckv, sk, svt, wk, wvt, toe, cmpb, B, S):
    tq = PAGE_ROWS
    nq = S // tq
    n_sub = ckv.shape[1]
    n_slc = -(-S // SLC_BLOCK)
    nsp = -(-n_slc // 16) * 16
    assert S % tq == 0 and nsp <= HEAD_DIM and nq > WINDOW // tq
    ovt = jnp.asarray(_pad_to(_overlap_np(n_sub, n_slc).T, (nsp, n_sub)))
    gw = GQA_GROUP * tq
    hps = PROMPT_HEADS_PER_STEP
    assert N_KV_HEADS % hps == 0
    return pl.pallas_call(
        functools.partial(_attn_prompt_kernel, n_slc=n_slc, n_sub=n_sub, hps=hps),
        grid=(B, N_KV_HEADS // hps, nq),
        in_specs=[pl.BlockSpec((hps, 1, HEAD_DIM, gw), lambda b, h, i: (h, b * nq + i, 0, 0)),
                  pl.BlockSpec((hps, 1, GATE_PAD, tq), lambda b, h, i: (h, b * nq + i, 0, 0)),
                  pl.BlockSpec((1, n_sub, hps * HEAD_DIM), lambda b, h, i: (b, 0, h)),
                  pl.BlockSpec((1, n_sub, hps * HEAD_DIM), lambda b, h, i: (b, 0, N_KV_HEADS // hps + h)),
                  pl.BlockSpec((hps, S, 2 * HEAD_DIM), lambda b, h, i: (h, b, 0)),
                  pl.BlockSpec((hps, nq, HEAD_DIM, tq), lambda b, h, i: (h, b, 0, 0)),
                  pl.BlockSpec((hps, S, HEAD_DIM), lambda b, h, i: (h, b, 0)),
                  pl.BlockSpec((hps, nq, HEAD_DIM, tq), lambda b, h, i: (h, b, 0, 0)),
                  pl.BlockSpec((hps * GQA_GROUP * 2, tq, tq), lambda b, h, i: (h, 0, 0)),
                  pl.BlockSpec((hps * GQA_GROUP, 1, n_sub, tq), lambda b, h, i: (h, i, 0, 0)),
                  pl.BlockSpec(ovt.shape, lambda b, h, i: (0, 0))],
        out_specs=pl.BlockSpec((1, tq, hps * GQA_GROUP * HEAD_DIM), lambda b, h, i: (b, i, h)),
        out_shape=jax.ShapeDtypeStruct((B, S, N_HEADS * HEAD_DIM), BF16),
        scratch_shapes=[pltpu.VMEM((hps, 1, gw), F32), pltpu.VMEM((hps, 1, gw), F32),
                        pltpu.VMEM((hps, HEAD_DIM, gw), F32), pltpu.VMEM((hps, HEAD_DIM, gw), F32)],
        compiler_params=_params("parallel", "parallel", "arbitrary"),
        name="attn_prompt",
    )(qt, gt, ckv, ckv, sk, svt, wk, wvt, toe, cmpb, ovt)


def _attn_sample_kernel(pt_ref, *refs, n_pages, n_slc, n_sub, past_len, win_buf, ds):
    del pt_ref
    page_refs = refs[:n_pages]
    (q_ref, gate_ref, ckv_ref, snew_ref, wst_ref, wnew_ref, bias_ref, ov_ref, e_ref,
     o_ref, wout_ref) = refs[n_pages:]
    nk_s = (n_pages + 1) * PAGE_ROWS
    nk_w = win_buf + PAGE_ROWS
    head_col = lambda r, col, n: r[pl.ds(col, n, stride=ROW_SPLIT), :]

    shift = ds * ROW_SPLIT
    wout_ref[0:win_buf * ROW_SPLIT - shift, :] = wst_ref[shift:win_buf * ROW_SPLIT, :]
    wout_ref[win_buf * ROW_SPLIT - shift:win_buf * ROW_SPLIT, :] = wnew_ref[...]

    t_idx = lax.broadcasted_iota(jnp.int32, (ds, 1), 0)
    qpos = past_len + t_idx
    pad = jnp.zeros((PAGE_ROWS - ds, HEAD_DIM), F32)
    gates = gate_ref[0]
    kv_heads = range(N_KV_HEADS)
    head_rows = GQA_GROUP * ds
    per_head = lambda x, reps: jnp.concatenate([x] * reps, axis=0)
    new_rows = lambda r, col: jnp.concatenate([r[pl.ds(col, ds, stride=ROW_SPLIT), :], pad], axis=0).astype(BF16)

    ks_pos = lax.broadcasted_iota(jnp.int32, (1, nk_s), 1)
    dist_w = win_buf + t_idx - lax.broadcasted_iota(jnp.int32, (1, nk_w), 1)
    ok_w = per_head((dist_w >= 0) & (dist_w < WINDOW), N_HEADS)
    n_idx = lax.broadcasted_iota(jnp.int32, (1, n_sub), 1)
    ok_c = per_head((n_idx * CMP_STRIDE + (CMP_BLOCK - 1) <= qpos) & (n_idx < n_sub - 1), N_HEADS)
    lanes = ov_ref.shape[1]
    j = lax.broadcasted_iota(jnp.int32, (1, lanes), 1)
    qb = per_head(qpos >> int(math.log2(SLC_BLOCK)), N_KV_HEADS)

    def softmax(logits, ok):
        logits = jnp.where(ok, logits, NEG_INF)
        p = jnp.exp(logits - jnp.max(logits, axis=-1, keepdims=True))
        return p / jnp.sum(p, axis=-1, keepdims=True)

    bias = bias_ref[...].reshape(N_HEADS * ds, bias_ref.shape[2])
    qs = [jnp.concatenate([q_ref[0, :, (hk * GQA_GROUP + g) * HEAD_DIM:(hk * GQA_GROUP + g + 1) * HEAD_DIM]
                           for g in range(GQA_GROUP)], axis=0) for hk in kv_heads]
    kcols = [slice(hk * HEAD_DIM, (hk + 1) * HEAD_DIM) for hk in kv_heads]
    vcols = [slice((N_KV_HEADS + hk) * HEAD_DIM, (N_KV_HEADS + hk + 1) * HEAD_DIM) for hk in kv_heads]

    def attend(k_past, v_past, k_new, v_new, bias_cols, ok):
        n = k_past(0).shape[0]
        logits = jnp.concatenate(
            [jnp.concatenate([_dot_nt(qs[hk], k_past(hk).astype(BF16)), _dot_nt(qs[hk], k_new(hk))], axis=1)
             for hk in kv_heads], axis=0) + bias_cols
        p = softmax(logits, ok).astype(BF16)
        return [_dot(p[hk * head_rows:(hk + 1) * head_rows, 0:n], v_past(hk).astype(BF16))
                + _dot(p[hk * head_rows:(hk + 1) * head_rows, n:], v_new(hk)) for hk in kv_heads]

    logits_c = jnp.concatenate([_dot_nt(qs[hk], ckv_ref[0, :, kcols[hk]].astype(BF16)) for hk in kv_heads], axis=0)
    p_c = jnp.where(ok_c, softmax(logits_c + bias[:, nk_s + nk_w:nk_s + nk_w + n_sub], ok_c), 0.0)
    o_c = [_dot(p_c[hk * head_rows:(hk + 1) * head_rows].astype(BF16), ckv_ref[0, :, vcols[hk]].astype(BF16))
           for hk in kv_heads]
    psum = jnp.concatenate(
        [functools.reduce(jnp.add, [p_c[(hk * GQA_GROUP + g) * ds:(hk * GQA_GROUP + g + 1) * ds]
                                    for g in range(GQA_GROUP)]) for hk in kv_heads], axis=0)
    imp = jnp.dot(psum, ov_ref[...], precision=lax.Precision.HIGHEST, preferred_element_type=F32)
    score = _score(imp, j, qb, n_slc)
    rank = jnp.zeros(score.shape, jnp.int32)
    for jj in range(n_slc):
        col = score[:, jj:jj + 1]
        rank = rank + ((col > score) | ((col == score) & (jj < j))).astype(jnp.int32)
    sel = ((rank < min(N_SELECT, n_slc)) & (j < n_slc)).astype(BF16)
    allowed = (_dot(sel, e_ref[...]) > 0.5) & (ks_pos <= per_head(qpos, N_KV_HEADS))
    ok_s = jnp.concatenate([per_head(allowed[hk * ds:(hk + 1) * ds], GQA_GROUP) for hk in kv_heads], axis=0)

    past_s = lambda col: jnp.concatenate([head_col(r, col, PAGE_ROWS) for r in page_refs], axis=0)
    o_s = attend(lambda hk: past_s(hk), lambda hk: past_s(N_KV_HEADS + hk),
                 lambda hk: new_rows(snew_ref, hk), lambda hk: new_rows(snew_ref, N_KV_HEADS + hk),
                 bias[:, 0:nk_s], ok_s)
    o_w = attend(lambda hk: head_col(wst_ref, hk, win_buf), lambda hk: head_col(wst_ref, N_KV_HEADS + hk, win_buf),
                 lambda hk: new_rows(wnew_ref, hk), lambda hk: new_rows(wnew_ref, N_KV_HEADS + hk),
                 bias[:, nk_s:nk_s + nk_w], ok_w)

    for hk in kv_heads:
        gcol = hk * GATE_PAD
        for g in range(GQA_GROUP):
            rows = slice(g * ds, (g + 1) * ds)
            gate = lambda br, g=g: gates[:, gcol + br * GQA_GROUP + g:gcol + br * GQA_GROUP + g + 1]
            o = gate(0) * o_c[hk][rows] + gate(1) * o_s[hk][rows] + gate(2) * o_w[hk][rows]
            col = (hk * GQA_GROUP + g) * HEAD_DIM
            o_ref[0, :, col:col + HEAD_DIM] = o.astype(o_ref.dtype)


def _attn_sample(q, gates, ckv, cache_slc, table, slc_new, win_state, win_new, bias, past_len, win_buf):
    DB, ds, _ = q.shape
    n_pages = table.shape[1]
    n_sub = ckv.shape[1]
    n_slc = -(-(past_len + ds) // SLC_BLOCK)
    lanes = PAGE_ROWS
    assert ds % 8 == 0 and ds <= PAGE_ROWS and n_slc <= lanes and n_sub <= lanes
    assert win_buf <= past_len and win_buf > ds
    nk_s = (n_pages + 1) * PAGE_ROWS
    ov = jnp.asarray(_pad_to(_overlap_np(n_sub, n_slc), (n_sub, lanes)))
    e = (np.arange(nk_s)[None, :] // SLC_BLOCK == np.arange(lanes)[:, None]) & (np.arange(lanes)[:, None] < n_slc)
    e = jnp.asarray(e.astype(np.float32), dtype=BF16)
    page_specs = [pl.BlockSpec((PAGE_ROWS * ROW_SPLIT, HEAD_DIM),
                               functools.partial(lambda b, pt, p: (pt[b, p], 0), p=p)) for p in range(n_pages)]
    per_b3 = lambda shape: pl.BlockSpec((1,) + shape[1:], lambda b, pt: (b, 0, 0))
    rows_b = lambda n: pl.BlockSpec((n * ROW_SPLIT, HEAD_DIM), lambda b, pt: (b, 0))
    const = lambda shape: pl.BlockSpec(shape, lambda b, pt: (0,) * len(shape))
    grid_spec = pltpu.PrefetchScalarGridSpec(
        num_scalar_prefetch=1,
        grid=(DB,),
        in_specs=page_specs + [per_b3(q.shape), per_b3(gates.shape), per_b3(ckv.shape), rows_b(ds),
                               rows_b(win_buf), rows_b(ds), const(bias.shape), const(ov.shape), const(e.shape)],
        out_specs=[per_b3(q.shape), rows_b(win_buf)],
    )
    return pl.pallas_call(
        functools.partial(_attn_sample_kernel, n_pages=n_pages, n_slc=n_slc, n_sub=n_sub,
                          past_len=past_len, win_buf=win_buf, ds=ds),
        grid_spec=grid_spec,
        out_shape=[jax.ShapeDtypeStruct(q.shape, BF16), jax.ShapeDtypeStruct(win_state.shape, F32)],
        compiler_params=_params("parallel"),
        name="attn_sample",
    )(table, *([cache_slc] * n_pages), q, gates, ckv, slc_new, win_state, win_new, bias, ov, e)


def _prompt_buckets(S, n_sub):
    tq = PAGE_ROWS
    k = np.arange(tq)[:, None]
    t = np.arange(tq)[None, :]
    toe = np.concatenate([_rel_bucket_np(t - k + d * tq) for d in range(2)], axis=0)
    kpos_c = np.arange(n_sub)[:, None] * CMP_STRIDE + CMP_BLOCK - 1
    cmp = np.concatenate([_rel_bucket_np(i * tq + t - kpos_c) for i in range(S // tq)], axis=0)
    return toe, cmp


def _sample_buckets(ds, past_len, n_pages, win_buf, n_sub):
    qpos = past_len + np.arange(ds)[:, None]
    ks = np.arange((n_pages + 1) * PAGE_ROWS)[None, :]
    kw = past_len - win_buf + np.arange(win_buf + PAGE_ROWS)[None, :]
    kc = np.arange(n_sub)[None, :] * CMP_STRIDE + CMP_BLOCK - 1
    return np.concatenate([_rel_bucket_np(qpos - ks), _rel_bucket_np(qpos - kw), _rel_bucket_np(qpos - kc)], axis=1)


def kernel(x_prompt, x_sample, state_pool, cache_cmp_kv, cache_slc_kv, state_win_kv, page_table, norms, w_pool,
           pool_scale, kv_norm, w_kv, cmp_pos, cmp_w1, cmp_b1, cmp_w2, w_q, w_o, rel_bias, w_ffn_in, w_ffn_out):
    B, S, D = x_prompt.shape
    DB, DS, _ = x_sample.shape
    n_pages = page_table.shape[1]
    past_len = n_pages * cache_slc_kv.shape[1]
    win_buf = state_win_kv.shape[1]
    assert norms.shape[0] == 2 and w_pool.shape[0] == 1 and w_q.shape[0] == 1, "one pooling + one attention layer"
    assert cache_slc_kv.shape[1] == PAGE_ROWS and S % PAGE_ROWS == 0
    assert past_len % CMP_STRIDE == 0 and DS < CMP_STRIDE and D == N_HEADS * HEAD_DIM
    assert S >= win_buf and N_BRANCH * GQA_GROUP <= GATE_PAD

    qdim = N_HEADS * HEAD_DIM
    w_pool_b = w_pool[0].astype(BF16)
    w_in_b = w_ffn_in.astype(BF16)
    w_out_b = w_ffn_out.astype(BF16)
    w_kv_b = w_kv.astype(BF16)
    w_qq = w_q.astype(BF16)
    w_qg = w_q[0, :, qdim:].reshape(D, N_BRANCH, N_KV_HEADS, GQA_GROUP).transpose(0, 2, 1, 3)
    w_qg = jnp.pad(w_qg.reshape(D, N_KV_HEADS, N_BRANCH * GQA_GROUP),
                   ((0, 0), (0, 0), (0, GATE_PAD - N_BRANCH * GQA_GROUP)))
    w_qg = jnp.pad(w_qg.reshape(D, N_KV_HEADS * GATE_PAD), ((0, 0), (0, 128 - N_KV_HEADS * GATE_PAD))).astype(BF16)
    w_o_b = w_o[0].astype(BF16)
    w1_b = cmp_w1.astype(BF16)
    w1cat = w1_b.reshape(2, 2, CMP_STRIDE, HEAD_DIM, HEAD_DIM).transpose(2, 0, 3, 1, 4)
    w1cat = w1cat.reshape(CMP_STRIDE, 2 * HEAD_DIM, 2 * HEAD_DIM)
    w2cat = cmp_w2.astype(BF16).reshape(2 * HEAD_DIM, HEAD_DIM)
    cmp_bias = _cmp_bias(cmp_pos, w1_b, cmp_b1.reshape(2, 1, HEAD_DIM))

    def layer0(x, buf, pos0):
        Bx, Sx, _ = x.shape
        h, st = _pool_layer(x, buf, norms[0, 0], w_pool_b, pool_scale[0], norms[0, 1], pos0)
        return _ffn(h.reshape(Bx * Sx, D), norms[0, 2], norms[0, 3], w_in_b, w_out_b, 0), st

    def layer1_tail(h1, o):
        h2 = _proj_res(o, w_o_b, norms[1, 1], h1)
        return _ffn(h2, norms[1, 2], norms[1, 3], w_in_b, w_out_b, 1)

    rows2d = lambda a: a.reshape(-1, HEAD_DIM)
    kv5 = lambda r, b, s: r.reshape(b, s, 2, N_KV_HEADS, HEAD_DIM)

    h1, st_p = layer0(x_prompt, jnp.zeros((B, POOL_BUF, D), F32), 0)
    cmp_p, slc_p, win_p, sk, svt, wk, wvt = _kv_proj(h1, kv_norm, w_kv_b, True, S)
    qt, gt = _q_proj(h1, norms[1, 0], w_qq, w_qg, True)
    pages_p = S // PAGE_ROWS
    ckv_p = _compress(cmp_p, jnp.arange(B * pages_p, dtype=jnp.int32).reshape(B, pages_p), w1cat, cmp_bias, w2cat)
    n_sub_p = ckv_p.shape[1]
    toe_bk, cmp_bk = _prompt_buckets(S, n_sub_p)
    toe = _bias_table(rel_bias, toe_bk, True, LOG2E).reshape(N_HEADS * 2, PAGE_ROWS, PAGE_ROWS)
    cmpb = _bias_table(rel_bias, cmp_bk, False, LOG2E).reshape(N_HEADS, pages_p, n_sub_p, PAGE_ROWS)
    o_p = _attn_prompt(qt, gt, ckv_p, sk, svt, wk, wvt, toe, cmpb, B, S)
    y_prompt = layer1_tail(h1, o_p.reshape(B * S, qdim)).reshape(B, S, D)
    win_state_p = win_p.reshape(B, S * ROW_SPLIT, HEAD_DIM)[:, (S - win_buf) * ROW_SPLIT:]

    h1s, st_s = layer0(x_sample, state_pool[0], past_len)
    cmp_s, slc_s, win_s = _kv_proj(h1s, kv_norm, w_kv_b, False, DS)
    q_s, gates_s = _q_proj(h1s, norms[1, 0], w_qq, w_qg, False)
    ckv_s = _compress(rows2d(cache_cmp_kv), page_table, w1cat, cmp_bias, w2cat)
    bias_s = _bias_table(rel_bias, _sample_buckets(DS, past_len, n_pages, win_buf, ckv_s.shape[1]))
    o_s, win_state_s = _attn_sample(
        q_s.reshape(DB, DS, qdim), gates_s.reshape(DB, DS, 128), ckv_s, rows2d(cache_slc_kv), page_table,
        slc_s, rows2d(state_win_kv), win_s, bias_s, past_len, win_buf)
    y_sample = layer1_tail(h1s, o_s.reshape(DB * DS, qdim)).reshape(DB, DS, D)

    return (y_prompt, y_sample, st_p[None], st_s[None],
            kv5(cmp_p, B, S), kv5(cmp_s, DB, DS), kv5(slc_p, B, S), kv5(slc_s, DB, DS),
            kv5(win_state_p, B, win_buf), kv5(win_state_s, DB, win_buf))
```

```python
import functools
import math

import numpy as np
import jax
import jax.numpy as jnp
from jax import lax
from jax.experimental import pallas as pl
from jax.experimental.pallas import tpu as pltpu

F32 = jnp.float32
BF16 = jnp.bfloat16

HEAD_DIM = 128
N_KV_HEADS = 4
GQA_GROUP = 4
N_HEADS = N_KV_HEADS * GQA_GROUP
N_BRANCH = 3
POOL_WINDOWS = (2, 4, 8, 16)
POOL_BUF = max(POOL_WINDOWS) - 1
POOL_HALO = POOL_BUF + 1
CMP_BLOCK = 32
CMP_STRIDE = 16
SLC_BLOCK = 64
N_SELECT = 16
WINDOW = 512
N_BUCKETS = 32
MAX_DISTANCE = 128
RMS_EPS = 1e-6
SEL_BIG = 1e3
NEG_INF = -1e30
LOG2E = math.log2(math.e)
PAGE_ROWS = 128
KV_COLS = 2 * N_KV_HEADS * HEAD_DIM
ROW_SPLIT = 2 * N_KV_HEADS
GATE_PAD = 16
SEL_FAR_TILES = 4
PROMPT_HEADS_PER_STEP = 4

VMEM_LIMIT_BYTES = 56 * 1024 * 1024

_NT = (((1,), (1,)), ((), ()))


def _params(*sem):
    return pltpu.CompilerParams(dimension_semantics=sem, vmem_limit_bytes=VMEM_LIMIT_BYTES)


def _rms(x):
    return x * lax.rsqrt(jnp.mean(x * x, axis=-1, keepdims=True) + RMS_EPS)


def _dot(a, b):
    return jnp.dot(a, b, preferred_element_type=F32)


def _dot_nt(a, b):
    return lax.dot_general(a, b, _NT, preferred_element_type=F32)


def _pool_layer_kernel(h_ref, buf_ref, g0_ref, w_ref, scale_ref, g1_ref, o_ref, st_ref, ext_ref, *, ts, pos0):
    i = pl.program_id(1)
    nb, _, D = h_ref.shape
    groups = w_ref.shape[0]
    gd = D // groups

    @pl.when(i == 0)
    def _():
        ext_ref[:, 0:POOL_HALO, :] = buf_ref[...]

    @pl.when(i > 0)
    def _():
        ext_ref[:, 0:POOL_HALO, :] = ext_ref[:, ts:ts + POOL_HALO, :]

    x = h_ref[...]
    ext_ref[:, POOL_HALO:POOL_HALO + ts, :] = _rms(x) * g0_ref[...]
    pos = pos0 + i * ts + lax.broadcasted_iota(jnp.int32, (1, ts, 1), 1)
    parts = []
    for gi, w in enumerate(POOL_WINDOWS):
        cols = slice(gi * gd, (gi + 1) * gd)
        s = ext_ref[:, :, cols]
        cur = s[:, POOL_HALO:POOL_HALO + ts]
        span = 1
        while span < w:
            s = s + pltpu.roll(s, span, 1)
            span *= 2
        s = s[:, POOL_HALO:POOL_HALO + ts]
        inv_cnt = 1.0 / jnp.minimum(w, pos + 1).astype(F32)
        mix = (s * inv_cnt - cur).astype(BF16).reshape(nb * ts, gd)
        parts.append(_dot(mix, w_ref[gi]))
    y = jnp.concatenate(parts, axis=-1) * scale_ref[...]
    o_ref[...] = x + (_rms(y) * g1_ref[...]).reshape(nb, ts, D)
    st_ref[...] = ext_ref[:, ts:ts + POOL_HALO, :]


def _pool_layer(h, buf, g0, w, scale, g1, pos0):
    B, S, D = h.shape
    rows = 256
    ts = min(S, rows)
    nb = max(1, min(B, rows // ts)) if S == ts else 1
    assert S % ts == 0 and B % nb == 0 and (ts >= POOL_HALO or S == ts) and ts % 8 == 0
    assert w.shape[0] == len(POOL_WINDOWS) and all(pw & (pw - 1) == 0 for pw in POOL_WINDOWS)
    buf16 = jnp.pad(buf, ((0, 0), (POOL_HALO - POOL_BUF, 0), (0, 0)))
    vec = lambda: pl.BlockSpec((1, D), lambda b, i: (0, 0))
    out, st = pl.pallas_call(
        functools.partial(_pool_layer_kernel, ts=ts, pos0=pos0),
        grid=(B // nb, S // ts),
        in_specs=[pl.BlockSpec((nb, ts, D), lambda b, i: (b, i, 0)),
                  pl.BlockSpec((nb, POOL_HALO, D), lambda b, i: (b, 0, 0)),
                  vec(), pl.BlockSpec(w.shape, lambda b, i: (0, 0, 0)), vec(), vec()],
        out_specs=[pl.BlockSpec((nb, ts, D), lambda b, i: (b, i, 0)),
                   pl.BlockSpec((nb, POOL_HALO, D), lambda b, i: (b, 0, 0))],
        out_shape=[jax.ShapeDtypeStruct((B, S, D), F32),
                   jax.ShapeDtypeStruct((B, POOL_HALO, D), F32)],
        scratch_shapes=[pltpu.VMEM((nb, POOL_HALO + ts, D), F32)],
        compiler_params=_params("parallel", "arbitrary"),
        name="pool_layer",
    )(h, buf16, g0.reshape(1, D), w, scale.reshape(1, D), g1.reshape(1, D))
    return out, st[:, POOL_HALO - POOL_BUF:]


def _proj_res_kernel(a_ref, w_ref, g_ref, res_ref, o_ref):
    o_ref[...] = res_ref[...] + _rms(_dot(a_ref[...], w_ref[...])) * g_ref[...]


def _proj_res(a, w, g, res):
    M, D = a.shape
    tm = min(M, 512)
    assert M % tm == 0
    row = lambda i: (i, 0)
    fixed = lambda i: (0, 0)
    return pl.pallas_call(
        _proj_res_kernel,
        grid=(M // tm,),
        in_specs=[pl.BlockSpec((tm, D), row), pl.BlockSpec(w.shape, fixed), pl.BlockSpec((1, D), fixed),
                  pl.BlockSpec((tm, D), row)],
        out_specs=pl.BlockSpec((tm, D), row),
        out_shape=jax.ShapeDtypeStruct((M, D), F32),
        compiler_params=_params("parallel"),
        name="o_proj",
    )(a, w, g.reshape(1, D), res)


def _ffn_kernel(h_ref, gi_ref, go_ref, wg_ref, wu_ref, wo_ref, out_ref, xn_ref, acc_ref):
    f = pl.program_id(1)

    @pl.when(f == 0)
    def _():
        xn_ref[...] = (_rms(h_ref[...]) * gi_ref[...]).astype(xn_ref.dtype)
        acc_ref[...] = jnp.zeros_like(acc_ref)

    xn = xn_ref[...]
    gate = _dot(xn, wg_ref[...])
    up = _dot(xn, wu_ref[...])
    act = (gate * jax.nn.sigmoid(gate) * up).astype(BF16)
    acc_ref[...] += _dot(act, wo_ref[...])

    @pl.when(f == pl.num_programs(1) - 1)
    def _():
        out_ref[...] = h_ref[...] + _rms(acc_ref[...]) * go_ref[...]


def _ffn(h, g_in, g_out, w_in, w_out, layer):
    M, D = h.shape
    F = w_out.shape[1]
    tm = min(M, 512)
    tf = 512
    assert M % tm == 0 and F % tf == 0
    nf = F // tf
    return pl.pallas_call(
        _ffn_kernel,
        grid=(M // tm, nf),
        in_specs=[pl.BlockSpec((tm, D), lambda i, f: (i, 0)),
                  pl.BlockSpec((1, D), lambda i, f: (0, 0)),
                  pl.BlockSpec((1, D), lambda i, f: (0, 0)),
                  pl.BlockSpec((None, D, tf), lambda i, f: (layer, 0, f)),
                  pl.BlockSpec((None, D, tf), lambda i, f: (layer, 0, nf + f)),
                  pl.BlockSpec((None, tf, D), lambda i, f: (layer, f, 0))],
        out_specs=pl.BlockSpec((tm, D), lambda i, f: (i, 0)),
        out_shape=jax.ShapeDtypeStruct((M, D), F32),
        scratch_shapes=[pltpu.VMEM((tm, D), BF16), pltpu.VMEM((tm, D), F32)],
        compiler_params=_params("parallel", "arbitrary"),
        name="ffn",
    )(h, g_in.reshape(1, D), g_out.reshape(1, D), w_in, w_in, w_out)


def _kv_proj_kernel(x_ref, g_ref, *refs, emit_aux, seq):
    w_refs, out_refs = refs[:N_BRANCH], refs[N_BRANCH:]
    tm = x_ref.shape[0]
    xn = (_rms(x_ref[...]) * g_ref[...]).astype(BF16)
    for br in range(N_BRANCH):
        y = _dot(xn, w_refs[br][...])
        for j in range(ROW_SPLIT):
            out_refs[br][pl.ds(j, tm, stride=ROW_SPLIT), :] = y[:, j * HEAD_DIM:(j + 1) * HEAD_DIM]
        if emit_aux and br >= 1:
            kb_ref = out_refs[N_BRANCH + 2 * (br - 1)]
            vt_ref = out_refs[N_BRANCH + 2 * (br - 1) + 1]
            if br == 1:
                pos = lax.rem(pl.program_id(0) * tm, seq) + lax.broadcasted_iota(jnp.int32, (tm, 1), 0)
                col = lax.broadcasted_iota(jnp.int32, (1, HEAD_DIM), 1)
                onehot = (col == (pos >> int(math.log2(SLC_BLOCK)))).astype(BF16)
            for h in range(N_KV_HEADS):
                kh = y[:, h * HEAD_DIM:(h + 1) * HEAD_DIM].astype(BF16)
                kb_ref[h] = jnp.concatenate([kh, onehot], axis=1) if br == 1 else kh
                vcol = (N_KV_HEADS + h) * HEAD_DIM
                for r in range(tm // PAGE_ROWS):
                    v = y[r * PAGE_ROWS:(r + 1) * PAGE_ROWS, vcol:vcol + HEAD_DIM]
                    vt_ref[h, r] = v.T.astype(BF16)


def _kv_proj(x, g, weights, emit_aux, seq):
    M, D = x.shape
    tm = min(M, 512)
    assert M % tm == 0 and (not emit_aux or (tm % PAGE_ROWS == 0 and seq % tm == 0
                                             and -(-seq // SLC_BLOCK) <= HEAD_DIM))
    out_specs = [pl.BlockSpec((tm * ROW_SPLIT, HEAD_DIM), lambda i: (i, 0))] * N_BRANCH
    out_shape = [jax.ShapeDtypeStruct((M * ROW_SPLIT, HEAD_DIM), F32)] * N_BRANCH
    if emit_aux:
        tiles = tm // PAGE_ROWS
        for kw in (2 * HEAD_DIM, HEAD_DIM):
            out_specs += [pl.BlockSpec((N_KV_HEADS, tm, kw), lambda i: (0, i, 0)),
                          pl.BlockSpec((N_KV_HEADS, tiles, HEAD_DIM, PAGE_ROWS), lambda i: (0, i, 0, 0))]
            out_shape += [jax.ShapeDtypeStruct((N_KV_HEADS, M, kw), BF16),
                          jax.ShapeDtypeStruct((N_KV_HEADS, M // PAGE_ROWS, HEAD_DIM, PAGE_ROWS), BF16)]
    return pl.pallas_call(
        functools.partial(_kv_proj_kernel, emit_aux=emit_aux, seq=seq),
        grid=(M // tm,),
        in_specs=[pl.BlockSpec((tm, D), lambda i: (i, 0)), pl.BlockSpec((1, D), lambda i: (0, 0))]
        + [pl.BlockSpec((D, KV_COLS), functools.partial(lambda i, br: (0, br), br=br)) for br in range(N_BRANCH)],
        out_specs=out_specs,
        out_shape=out_shape,
        compiler_params=_params("parallel"),
        name="kv_proj",
    )(x, g.reshape(1, D), *([weights] * N_BRANCH))


def _q_proj_kernel(x_ref, g_ref, wq_ref, wg_ref, q_ref, gate_ref, *, transposed):
    tm = x_ref.shape[0]
    xn = (_rms(x_ref[...]) * g_ref[...]).astype(BF16)
    q = _dot(xn, wq_ref[...]) * (HEAD_DIM ** -0.5 * (LOG2E if transposed else 1.0))
    gates = jax.nn.sigmoid(_dot(xn, wg_ref[...]))
    if not transposed:
        q_ref[...] = q.astype(BF16)
        gate_ref[...] = gates
        return
    for r in range(tm // PAGE_ROWS):
        rows = slice(r * PAGE_ROWS, (r + 1) * PAGE_ROWS)
        gt = gates[rows].T
        for hk in range(N_KV_HEADS):
            gate_ref[hk, r] = gt[hk * GATE_PAD:(hk + 1) * GATE_PAD]
            for g in range(GQA_GROUP):
                col = (hk * GQA_GROUP + g) * HEAD_DIM
                q_ref[hk, r, :, g * PAGE_ROWS:(g + 1) * PAGE_ROWS] = q[rows, col:col + HEAD_DIM].T.astype(BF16)


def _q_proj(x, g, wq, wg, transposed):
    M, D = x.shape
    qdim = N_HEADS * HEAD_DIM
    tm = min(M, 512)
    assert M % tm == 0 and (tm % PAGE_ROWS == 0 or not transposed)
    tiles = tm // PAGE_ROWS
    gw = GQA_GROUP * PAGE_ROWS
    if transposed:
        out_specs = [pl.BlockSpec((N_KV_HEADS, tiles, HEAD_DIM, gw), lambda i: (0, i, 0, 0)),
                     pl.BlockSpec((N_KV_HEADS, tiles, GATE_PAD, PAGE_ROWS), lambda i: (0, i, 0, 0))]
        out_shape = [jax.ShapeDtypeStruct((N_KV_HEADS, M // PAGE_ROWS, HEAD_DIM, gw), BF16),
                     jax.ShapeDtypeStruct((N_KV_HEADS, M // PAGE_ROWS, GATE_PAD, PAGE_ROWS), F32)]
    else:
        out_specs = [pl.BlockSpec((tm, qdim), lambda i: (i, 0)),
                     pl.BlockSpec((tm, wg.shape[1]), lambda i: (i, 0))]
        out_shape = [jax.ShapeDtypeStruct((M, qdim), BF16), jax.ShapeDtypeStruct((M, wg.shape[1]), F32)]
    return pl.pallas_call(
        functools.partial(_q_proj_kernel, transposed=transposed),
        grid=(M // tm,),
        in_specs=[pl.BlockSpec((tm, D), lambda i: (i, 0)), pl.BlockSpec((1, D), lambda i: (0, 0)),
                  pl.BlockSpec((None, D, qdim), lambda i: (0, 0, 0)), pl.BlockSpec(wg.shape, lambda i: (0, 0))],
        out_specs=out_specs,
        out_shape=out_shape,
        compiler_params=_params("parallel"),
        name="q_proj",
    )(x, g.reshape(1, D), wq, wg)


def _gelu_tanh(x):
    return 0.5 * x * (1.0 + jnp.tanh(math.sqrt(2.0 / math.pi) * (x + 0.044715 * (x * x * x))))


def _cmp_bias_kernel(pos_ref, w1_ref, b1_ref, o_ref):
    for c in range(2):
        acc = jnp.zeros((8, HEAD_DIM), F32)
        for j in range(CMP_BLOCK):
            pj = jnp.broadcast_to(pos_ref[c, j:j + 1, :], (8, HEAD_DIM)).astype(BF16)
            acc = acc + _dot(pj, w1_ref[c, j])
        o_ref[c] = acc + b1_ref[c]


def _cmp_bias(pos, w1, b1):
    return pl.pallas_call(
        _cmp_bias_kernel,
        out_shape=jax.ShapeDtypeStruct((2, 8, HEAD_DIM), F32),
        compiler_params=pltpu.CompilerParams(vmem_limit_bytes=VMEM_LIMIT_BYTES),
        name="cmp_bias",
    )(pos, w1, b1)


def _compress_kernel(pt_ref, *refs, n_pages):
    del pt_ref
    page_refs = refs[:n_pages]
    w1_ref, bias_ref, w2_ref, o_ref, scr_ref, sum_ref = refs[n_pages:]
    sub_per_page = PAGE_ROWS // CMP_STRIDE
    n_sub = n_pages * sub_per_page
    sub_rows = CMP_STRIDE * ROW_SPLIT
    pages_per_chunk = 4 if n_pages % 4 == 0 else 1
    chunk_rows = pages_per_chunk * sub_per_page * ROW_SPLIT
    is_k = (lax.broadcasted_iota(jnp.int32, (chunk_rows, 1), 0) & (ROW_SPLIT - 1)) < N_KV_HEADS
    bias = jnp.where(is_k[0:ROW_SPLIT], bias_ref[0], bias_ref[1])
    bias = jnp.concatenate([bias] * (chunk_rows // ROW_SPLIT), axis=0)

    def by_side(x):
        return jnp.concatenate([jnp.where(is_k, x, 0.0), jnp.where(is_k, 0.0, x)], axis=1).astype(BF16)

    for ck in range(n_pages // pages_per_chunk):
        chunk_pages = page_refs[ck * pages_per_chunk:(ck + 1) * pages_per_chunk]
        acc = jnp.zeros((chunk_rows, 2 * HEAD_DIM), F32)
        for j in range(CMP_STRIDE):
            x = jnp.concatenate([r[n * sub_rows + j * ROW_SPLIT:n * sub_rows + (j + 1) * ROW_SPLIT, :]
                                 for r in chunk_pages for n in range(sub_per_page)], axis=0)
            acc = acc + _dot(by_side(x), w1_ref[j])
        scr_ref[ck * chunk_rows:(ck + 1) * chunk_rows, :] = acc
    total = n_sub * ROW_SPLIT
    for ck in range(n_pages // pages_per_chunk):
        rows = slice(ck * chunk_rows, (ck + 1) * chunk_rows)
        first = scr_ref[rows, 0:HEAD_DIM]
        nxt_rows = slice(ck * chunk_rows + ROW_SPLIT, min((ck + 1) * chunk_rows + ROW_SPLIT, total))
        nxt = scr_ref[nxt_rows, HEAD_DIM:2 * HEAD_DIM]
        if nxt.shape[0] < chunk_rows:
            nxt = jnp.concatenate([nxt, jnp.zeros((chunk_rows - nxt.shape[0], HEAD_DIM), F32)], axis=0)
        hid = _gelu_tanh(first + nxt + bias)
        out = _dot(by_side(hid), w2_ref[...])
        if (ck + 1) * chunk_rows == total:
            row = lax.broadcasted_iota(jnp.int32, (chunk_rows, 1), 0)
            out = jnp.where(row < chunk_rows - ROW_SPLIT, out, 0.0)
        sum_ref[rows, :] = out
    for ch in range(ROW_SPLIT):
        o_ref[0, :, ch * HEAD_DIM:(ch + 1) * HEAD_DIM] = sum_ref[pl.ds(ch, n_sub, stride=ROW_SPLIT), :]


def _compress(pages, table, w1cat, bias, w2cat):
    nb, n_pages = table.shape
    n_sub = n_pages * (PAGE_ROWS // CMP_STRIDE)
    page_rows2d = PAGE_ROWS * ROW_SPLIT
    page_specs = [pl.BlockSpec((page_rows2d, HEAD_DIM), functools.partial(lambda b, pt, p: (pt[b, p], 0), p=p))
                  for p in range(n_pages)]
    const = lambda shape: pl.BlockSpec(shape, lambda b, pt: (0,) * len(shape))
    grid_spec = pltpu.PrefetchScalarGridSpec(
        num_scalar_prefetch=1,
        grid=(nb,),
        in_specs=page_specs + [const(w1cat.shape), const(bias.shape), const(w2cat.shape)],
        out_specs=pl.BlockSpec((1, n_sub, KV_COLS), lambda b, pt: (b, 0, 0)),
        scratch_shapes=[pltpu.VMEM((n_sub * ROW_SPLIT, 2 * HEAD_DIM), F32),
                        pltpu.VMEM((n_sub * ROW_SPLIT, HEAD_DIM), F32)],
    )
    return pl.pallas_call(
        functools.partial(_compress_kernel, n_pages=n_pages),
        grid_spec=grid_spec,
        out_shape=jax.ShapeDtypeStruct((nb, n_sub, KV_COLS), F32),
        compiler_params=_params("parallel"),
        name="compress",
    )(table, *([pages] * n_pages), w1cat, bias, w2cat)


def _rel_bucket_np(dist):
    n = np.maximum(dist, 0)
    max_exact = N_BUCKETS // 2
    nf = np.maximum(n, 1).astype(np.float32)
    large = max_exact + (np.log(nf / np.float32(max_exact)) / np.float32(math.log(MAX_DISTANCE / max_exact))
                         * np.float32(N_BUCKETS - max_exact)).astype(np.int32)
    large = np.minimum(large, N_BUCKETS - 1)
    return np.where(n < max_exact, n, large).astype(np.int32)


def _bias_table_kernel(rb_ref, bk_ref, o_ref, *, minus_far, scale):
    bk = bk_ref[...]
    for h in range(N_HEADS):
        v = jnp.full(bk.shape, rb_ref[0, h], F32)
        for b in range(1, N_BUCKETS):
            v = jnp.where(bk == b, rb_ref[b, h], v)
        if minus_far:
            v = v - rb_ref[N_BUCKETS - 1, h]
        o_ref[h] = v if scale == 1.0 else v * scale


def _bias_table(rel_bias, bucket, minus_far=False, scale=1.0):
    R, C = bucket.shape
    tr = min(R, 128)
    assert R % tr == 0
    return pl.pallas_call(
        functools.partial(_bias_table_kernel, minus_far=minus_far, scale=scale),
        grid=(R // tr,),
        in_specs=[pl.BlockSpec(memory_space=pltpu.SMEM), pl.BlockSpec((tr, C), lambda r: (r, 0))],
        out_specs=pl.BlockSpec((N_HEADS, tr, C), lambda r: (0, r, 0)),
        out_shape=jax.ShapeDtypeStruct((N_HEADS, R, C), F32),
        compiler_params=_params("parallel"),
        name="bias_table",
    )(rel_bias, jnp.asarray(bucket))


def _overlap_np(n_sub, n_slc):
    n_cmp = n_sub - 1
    cs = np.arange(n_cmp) * CMP_STRIDE
    ce = cs + CMP_BLOCK - 1
    ss = np.arange(n_slc) * SLC_BLOCK
    se = ss + SLC_BLOCK - 1
    ov = np.minimum(ce[:, None], se[None, :]) - np.maximum(cs[:, None], ss[None, :]) + 1
    out = np.zeros((n_sub, n_slc), np.float32)
    out[:n_cmp] = np.maximum(ov, 0).astype(np.float32) / CMP_BLOCK
    return out


def _pad_to(a, shape):
    return np.pad(a, [(0, s - d) for d, s in zip(a.shape, shape)])


def _score(imp, j, qb, n_slc):
    prio = 2.0 * (j == qb).astype(F32) + (j == qb - 1).astype(F32) + (j == 0).astype(F32)
    score = jnp.where(j <= qb, imp + SEL_BIG * prio, -SEL_BIG)
    return jnp.where(j < n_slc, score, -3e38)


def _attn_prompt_kernel(qt_ref, gt_ref, ck_ref, cv_ref, sk_ref, svt_ref, wk_ref, wvt_ref,
                        toe_ref, cmpb_ref, ovt_ref, o_ref,
                        m_ref, l_ref, acc_ref, accw_ref, *, n_slc, n_sub, hps):
    tq = PAGE_ROWS
    i = pl.program_id(2)
    heads = range(hps)
    t_idx = lax.broadcasted_iota(jnp.int32, (1, tq), 1)
    qpos = i * tq + t_idx
    lane4 = lambda x: jnp.concatenate([x] * GQA_GROUP, axis=1)
    n_idx = lax.broadcasted_iota(jnp.int32, (n_sub, 1), 0)
    valid = lane4((n_idx * CMP_STRIDE + (CMP_BLOCK - 1) <= qpos) & (n_idx < n_sub - 1))
    nsp = ovt_ref.shape[0]
    j = lax.broadcasted_iota(jnp.int32, (nsp, 1), 0)
    qb = qpos >> int(math.log2(SLC_BLOCK))

    qts, qt_sels, o_cs = [], [], []
    for hh in heads:
        qt = qt_ref[hh, 0]
        cols = slice(hh * HEAD_DIM, (hh + 1) * HEAD_DIM)
        kc = ck_ref[0, :, cols].astype(BF16)
        vct = cv_ref[0, :, cols].T.astype(BF16)
        bias_c = jnp.concatenate([cmpb_ref[hh * GQA_GROUP + g, 0] for g in range(GQA_GROUP)], axis=1)
        logits = jnp.where(valid, _dot(kc, qt) + bias_c, NEG_INF)
        p = jnp.exp2(logits - jnp.max(logits, axis=0, keepdims=True))
        p = p * (1.0 / jnp.sum(p, axis=0, keepdims=True))
        p = jnp.where(valid, p, 0.0)
        o_cs.append(_dot(vct, p.astype(BF16)))
        psum = p[:, 0:tq]
        for g in range(1, GQA_GROUP):
            psum = psum + p[:, g * tq:(g + 1) * tq]
        imp = jnp.dot(ovt_ref[...], psum, precision=lax.Precision.HIGHEST, preferred_element_type=F32)
        score = _score(imp, j, qb, n_slc)
        rank = jnp.zeros((nsp, tq), jnp.int32)
        for jj in range(n_slc):
            row = score[jj:jj + 1, :]
            rank = rank + ((row > score) | ((row == score) & (jj < j))).astype(jnp.int32)
        madd = jnp.where(rank < min(N_SELECT, n_slc), 0.0, NEG_INF)
        qts.append(qt)
        qt_sels.append(jnp.concatenate([qt, lane4(madd).astype(BF16),
                                        jnp.zeros((HEAD_DIM - nsp, GQA_GROUP * tq), BF16)], axis=0))

    k_row = lax.broadcasted_iota(jnp.int32, (tq, 1), 0)

    def chunk(hh, k, kinds, k_ref, vt_ref, q_op, out_ref):
        nt = len(kinds)
        k0 = pl.multiple_of(k * tq, tq)
        s = _dot(k_ref[hh, pl.ds(k0, nt * tq), :], q_op)
        parts = []
        for jx, kind in enumerate(kinds):
            sj = s[jx * tq:(jx + 1) * tq]
            if kind in ("near", "diag"):
                d = 0 if kind == "diag" else 1
                sj = sj + jnp.concatenate([toe_ref[(hh * GQA_GROUP + g) * 2 + d] for g in range(GQA_GROUP)],
                                          axis=1)
            if kind == "diag":
                sj = jnp.where(lane4(k_row <= t_idx), sj, NEG_INF)
            elif kind == "edge":
                sj = jnp.where(lane4(k_row > t_idx), sj, NEG_INF)
            parts.append(sj)
        m_new = functools.reduce(jnp.maximum, parts)
        m_new = jnp.max(m_new, axis=0, keepdims=True)
        if out_ref is None:
            m_old = m_ref[hh]
            m_new = jnp.maximum(m_old, m_new)
            alpha = jnp.exp2(m_old - m_new)
        ps = [jnp.exp2(sj - m_new) for sj in parts]
        lsum = jnp.sum(functools.reduce(jnp.add, ps), axis=0, keepdims=True)
        vt = jnp.concatenate([vt_ref[hh, k + jx] for jx in range(nt)], axis=1)
        pv = _dot(vt, jnp.concatenate(ps, axis=0).astype(BF16))
        if out_ref is None:
            l_ref[hh] = alpha * l_ref[hh] + lsum
            acc_ref[hh] = alpha * acc_ref[hh] + pv
            m_ref[hh] = m_new
        else:
            out_ref[hh] = pv * (1.0 / lsum)

    def tail_kinds(n_far):
        return ("far",) * n_far + ("near", "diag")

    far_step = SEL_FAR_TILES
    wt = WINDOW // tq
    m_ref[...] = jnp.full(m_ref.shape, NEG_INF, F32)
    l_ref[...] = jnp.zeros(l_ref.shape, F32)
    acc_ref[...] = jnp.zeros(acc_ref.shape, F32)

    n_far = jnp.maximum(i - 1, 0) // far_step

    def far_body(c, carry):
        for hh in heads:
            chunk(hh, c * far_step, ("far",) * far_step, sk_ref, svt_ref, qt_sels[hh], None)
        return carry

    lax.fori_loop(0, n_far, far_body, 0)

    for c in range(wt):
        @pl.when(i == c)
        def _(c=c):
            kinds = tail_kinds(c - 1) if c >= 1 else ("diag",)
            for hh in heads:
                chunk(hh, 0, kinds, sk_ref, svt_ref, qt_sels[hh], None)
                chunk(hh, 0, kinds, wk_ref, wvt_ref, qts[hh], accw_ref)

    for r in range(far_step):
        @pl.when((i >= wt) & (lax.rem(i + far_step - 1, far_step) == r))
        def _(r=r):
            for hh in heads:
                chunk(hh, i - 1 - r, tail_kinds(r), sk_ref, svt_ref, qt_sels[hh], None)
                chunk(hh, i - wt, ("edge",) + tail_kinds(wt - 2), wk_ref, wvt_ref, qts[hh], accw_ref)

    for hh in heads:
        o_s = acc_ref[hh] * (1.0 / l_ref[hh])
        o_w = accw_ref[hh]
        o_c = o_cs[hh]
        gt = gt_ref[hh, 0]
        for g in range(GQA_GROUP):
            lanes = slice(g * tq, (g + 1) * tq)
            o = (gt[g:g + 1] * o_c[:, lanes] + gt[GQA_GROUP + g:GQA_GROUP + g + 1] * o_s[:, lanes]
                 + gt[2 * GQA_GROUP + g:2 * GQA_GROUP + g + 1] * o_w[:, lanes])
            col = (hh * GQA_GROUP + g) * HEAD_DIM
            o_ref[0, :, col:col + HEAD_DIM] = o.T.astype(o_ref.dtype)


def _attn_prompt(qt, gt, ckv, sk, svt, wk, wvt, toe, cmpb, B, S):
    tq = PAGE_ROWS
    nq = S // tq
    n_sub = ckv.shape[1]
    n_slc = -(-S // SLC_BLOCK)
    nsp = -(-n_slc // 16) * 16
    assert S % tq == 0 and nsp <= HEAD_DIM and nq > WINDOW // tq
    ovt = jnp.asarray(_pad_to(_overlap_np(n_sub, n_slc).T, (nsp, n_sub)))
    gw = GQA_GROUP * tq
    hps = PROMPT_HEADS_PER_STEP
    assert N_KV_HEADS % hps == 0
    return pl.pallas_call(
        functools.partial(_attn_prompt_kernel, n_slc=n_slc, n_sub=n_sub, hps=hps),
        grid=(B, N_KV_HEADS // hps, nq),
        in_specs=[pl.BlockSpec((hps, 1, HEAD_DIM, gw), lambda b, h, i: (h, b * nq + i, 0, 0)),
                  pl.BlockSpec((hps, 1, GATE_PAD, tq), lambda b, h, i: (h, b * nq + i, 0, 0)),
                  pl.BlockSpec((1, n_sub, hps * HEAD_DIM), lambda b, h, i: (b, 0, h)),
                  pl.BlockSpec((1, n_sub, hps * HEAD_DIM), lambda b, h, i: (b, 0, N_KV_HEADS // hps + h)),
                  pl.BlockSpec((hps, S, 2 * HEAD_DIM), lambda b, h, i: (h, b, 0)),
                  pl.BlockSpec((hps, nq, HEAD_DIM, tq), lambda b, h, i: (h, b, 0, 0)),
                  pl.BlockSpec((hps, S, HEAD_DIM), lambda b, h, i: (h, b, 0)),
                  pl.BlockSpec((hps, nq, HEAD_DIM, tq), lambda b, h, i: (h, b, 0, 0)),
                  pl.BlockSpec((hps * GQA_GROUP * 2, tq, tq), lambda b, h, i: (h, 0, 0)),
                  pl.BlockSpec((hps * GQA_GROUP, 1, n_sub, tq), lambda b, h, i: (h, i, 0, 0)),
                  pl.BlockSpec(ovt.shape, lambda b, h, i: (0, 0))],
        out_specs=pl.BlockSpec((1, tq, hps * GQA_GROUP * HEAD_DIM), lambda b, h, i: (b, i, h)),
        out_shape=jax.ShapeDtypeStruct((B, S, N_HEADS * HEAD_DIM), BF16),
        scratch_shapes=[pltpu.VMEM((hps, 1, gw), F32), pltpu.VMEM((hps, 1, gw), F32),
                        pltpu.VMEM((hps, HEAD_DIM, gw), F32), pltpu.VMEM((hps, HEAD_DIM, gw), F32)],
        compiler_params=_params("parallel", "parallel", "arbitrary"),
        name="attn_prompt",
    )(qt, gt, ckv, ckv, sk, svt, wk, wvt, toe, cmpb, ovt)


def _attn_sample_kernel(pt_ref, *refs, n_pages, n_slc, n_sub, past_len, win_buf, ds):
    del pt_ref
    page_refs = refs[:n_pages]
    (q_ref, gate_ref, ckv_ref, snew_ref, wst_ref, wnew_ref, bias_ref, ov_ref, e_ref,
     o_ref, wout_ref) = refs[n_pages:]
    nk_s = (n_pages + 1) * PAGE_ROWS
    nk_w = win_buf + PAGE_ROWS
    head_col = lambda r, col, n: r[pl.ds(col, n, stride=ROW_SPLIT), :]

    shift = ds * ROW_SPLIT
    wout_ref[0:win_buf * ROW_SPLIT - shift, :] = wst_ref[shift:win_buf * ROW_SPLIT, :]
    wout_ref[win_buf * ROW_SPLIT - shift:win_buf * ROW_SPLIT, :] = wnew_ref[...]

    t_idx = lax.broadcasted_iota(jnp.int32, (ds, 1), 0)
    qpos = past_len + t_idx
    pad = jnp.zeros((PAGE_ROWS - ds, HEAD_DIM), F32)
    gates = gate_ref[0]
    kv_heads = range(N_KV_HEADS)
    head_rows = GQA_GROUP * ds
    per_head = lambda x, reps: jnp.concatenate([x] * reps, axis=0)
    new_rows = lambda r, col: jnp.concatenate([r[pl.ds(col, ds, stride=ROW_SPLIT), :], pad], axis=0).astype(BF16)

    ks_pos = lax.broadcasted_iota(jnp.int32, (1, nk_s), 1)
    dist_w = win_buf + t_idx - lax.broadcasted_iota(jnp.int32, (1, nk_w), 1)
    ok_w = per_head((dist_w >= 0) & (dist_w < WINDOW), N_HEADS)
    n_idx = lax.broadcasted_iota(jnp.int32, (1, n_sub), 1)
    ok_c = per_head((n_idx * CMP_STRIDE + (CMP_BLOCK - 1) <= qpos) & (n_idx < n_sub - 1), N_HEADS)
    lanes = ov_ref.shape[1]
    j = lax.broadcasted_iota(jnp.int32, (1, lanes), 1)
    qb = per_head(qpos >> int(math.log2(SLC_BLOCK)), N_KV_HEADS)

    def softmax(logits, ok):
        logits = jnp.where(ok, logits, NEG_INF)
        p = jnp.exp(logits - jnp.max(logits, axis=-1, keepdims=True))
        return p / jnp.sum(p, axis=-1, keepdims=True)

    bias = bias_ref[...].reshape(N_HEADS * ds, bias_ref.shape[2])
    qs = [jnp.concatenate([q_ref[0, :, (hk * GQA_GROUP + g) * HEAD_DIM:(hk * GQA_GROUP + g + 1) * HEAD_DIM]
                           for g in range(GQA_GROUP)], axis=0) for hk in kv_heads]
    kcols = [slice(hk * HEAD_DIM, (hk + 1) * HEAD_DIM) for hk in kv_heads]
    vcols = [slice((N_KV_HEADS + hk) * HEAD_DIM, (N_KV_HEADS + hk + 1) * HEAD_DIM) for hk in kv_heads]

    def attend(k_past, v_past, k_new, v_new, bias_cols, ok):
        n = k_past(0).shape[0]
        logits = jnp.concatenate(
            [jnp.concatenate([_dot_nt(qs[hk], k_past(hk).astype(BF16)), _dot_nt(qs[hk], k_new(hk))], axis=1)
             for hk in kv_heads], axis=0) + bias_cols
        p = softmax(logits, ok).astype(BF16)
        return [_dot(p[hk * head_rows:(hk + 1) * head_rows, 0:n], v_past(hk).astype(BF16))
                + _dot(p[hk * head_rows:(hk + 1) * head_rows, n:], v_new(hk)) for hk in kv_heads]

    logits_c = jnp.concatenate([_dot_nt(qs[hk], ckv_ref[0, :, kcols[hk]].astype(BF16)) for hk in kv_heads], axis=0)
    p_c = jnp.where(ok_c, softmax(logits_c + bias[:, nk_s + nk_w:nk_s + nk_w + n_sub], ok_c), 0.0)
    o_c = [_dot(p_c[hk * head_rows:(hk + 1) * head_rows].astype(BF16), ckv_ref[0, :, vcols[hk]].astype(BF16))
           for hk in kv_heads]
    psum = jnp.concatenate(
        [functools.reduce(jnp.add, [p_c[(hk * GQA_GROUP + g) * ds:(hk * GQA_GROUP + g + 1) * ds]
                                    for g in range(GQA_GROUP)]) for hk in kv_heads], axis=0)
    imp = jnp.dot(psum, ov_ref[...], precision=lax.Precision.HIGHEST, preferred_element_type=F32)
    score = _score(imp, j, qb, n_slc)
    rank = jnp.zeros(score.shape, jnp.int32)
    for jj in range(n_slc):
        col = score[:, jj:jj + 1]
        rank = rank + ((col > score) | ((col == score) & (jj < j))).astype(jnp.int32)
    sel = ((rank < min(N_SELECT, n_slc)) & (j < n_slc)).astype(BF16)
    allowed = (_dot(sel, e_ref[...]) > 0.5) & (ks_pos <= per_head(qpos, N_KV_HEADS))
    ok_s = jnp.concatenate([per_head(allowed[hk * ds:(hk + 1) * ds], GQA_GROUP) for hk in kv_heads], axis=0)

    past_s = lambda col: jnp.concatenate([head_col(r, col, PAGE_ROWS) for r in page_refs], axis=0)
    o_s = attend(lambda hk: past_s(hk), lambda hk: past_s(N_KV_HEADS + hk),
                 lambda hk: new_rows(snew_ref, hk), lambda hk: new_rows(snew_ref, N_KV_HEADS + hk),
                 bias[:, 0:nk_s], ok_s)
    o_w = attend(lambda hk: head_col(wst_ref, hk, win_buf), lambda hk: head_col(wst_ref, N_KV_HEADS + hk, win_buf),
                 lambda hk: new_rows(wnew_ref, hk), lambda hk: new_rows(wnew_ref, N_KV_HEADS + hk),
                 bias[:, nk_s:nk_s + nk_w], ok_w)

    for hk in kv_heads:
        gcol = hk * GATE_PAD
        for g in range(GQA_GROUP):
            rows = slice(g * ds, (g + 1) * ds)
            gate = lambda br, g=g: gates[:, gcol + br * GQA_GROUP + g:gcol + br * GQA_GROUP + g + 1]
            o = gate(0) * o_c[hk][rows] + gate(1) * o_s[hk][rows] + gate(2) * o_w[hk][rows]
            col = (hk * GQA_GROUP + g) * HEAD_DIM
            o_ref[0, :, col:col + HEAD_DIM] = o.astype(o_ref.dtype)


def _attn_sample(q, gates, ckv, cache_slc, table, slc_new, win_state, win_new, bias, past_len, win_buf):
    DB, ds, _ = q.shape
    n_pages = table.shape[1]
    n_sub = ckv.shape[1]
    n_slc = -(-(past_len + ds) // SLC_BLOCK)
    lanes = PAGE_ROWS
    assert ds % 8 == 0 and ds <= PAGE_ROWS and n_slc <= lanes and n_sub <= lanes
    assert win_buf <= past_len and win_buf > ds
    nk_s = (n_pages + 1) * PAGE_ROWS
    ov = jnp.asarray(_pad_to(_overlap_np(n_sub, n_slc), (n_sub, lanes)))
    e = (np.arange(nk_s)[None, :] // SLC_BLOCK == np.arange(lanes)[:, None]) & (np.arange(lanes)[:, None] < n_slc)
    e = jnp.asarray(e.astype(np.float32), dtype=BF16)
    page_specs = [pl.BlockSpec((PAGE_ROWS * ROW_SPLIT, HEAD_DIM),
                               functools.partial(lambda b, pt, p: (pt[b, p], 0), p=p)) for p in range(n_pages)]
    per_b3 = lambda shape: pl.BlockSpec((1,) + shape[1:], lambda b, pt: (b, 0, 0))
    rows_b = lambda n: pl.BlockSpec((n * ROW_SPLIT, HEAD_DIM), lambda b, pt: (b, 0))
    const = lambda shape: pl.BlockSpec(shape, lambda b, pt: (0,) * len(shape))
    grid_spec = pltpu.PrefetchScalarGridSpec(
        num_scalar_prefetch=1,
        grid=(DB,),
        in_specs=page_specs + [per_b3(q.shape), per_b3(gates.shape), per_b3(ckv.shape), rows_b(ds),
                               rows_b(win_buf), rows_b(ds), const(bias.shape), const(ov.shape), const(e.shape)],
        out_specs=[per_b3(q.shape), rows_b(win_buf)],
    )
    return pl.pallas_call(
        functools.partial(_attn_sample_kernel, n_pages=n_pages, n_slc=n_slc, n_sub=n_sub,
                          past_len=past_len, win_buf=win_buf, ds=ds),
        grid_spec=grid_spec,
        out_shape=[jax.ShapeDtypeStruct(q.shape, BF16), jax.ShapeDtypeStruct(win_state.shape, F32)],
        compiler_params=_params("parallel"),
        name="attn_sample",
    )(table, *([cache_slc] * n_pages), q, gates, ckv, slc_new, win_state, win_new, bias, ov, e)


def _prompt_buckets(S, n_sub):
    tq = PAGE_ROWS
    k = np.arange(tq)[:, None]
    t = np.arange(tq)[None, :]
    toe = np.concatenate([_rel_bucket_np(t - k + d * tq) for d in range(2)], axis=0)
    kpos_c = np.arange(n_sub)[:, None] * CMP_STRIDE + CMP_BLOCK - 1
    cmp = np.concatenate([_rel_bucket_np(i * tq + t - kpos_c) for i in range(S // tq)], axis=0)
    return toe, cmp


def _sample_buckets(ds, past_len, n_pages, win_buf, n_sub):
    qpos = past_len + np.arange(ds)[:, None]
    ks = np.arange((n_pages + 1) * PAGE_ROWS)[None, :]
    kw = past_len - win_buf + np.arange(win_buf + PAGE_ROWS)[None, :]
    kc = np.arange(n_sub)[None, :] * CMP_STRIDE + CMP_BLOCK - 1
    return np.concatenate([_rel_bucket_np(qpos - ks), _rel_bucket_np(qpos - kw), _rel_bucket_np(qpos - kc)], axis=1)


def kernel(x_prompt, x_sample, state_pool, cache_cmp_kv, cache_slc_kv, state_win_kv, page_table, norms, w_pool,
           pool_scale, kv_norm, w_kv, cmp_pos, cmp_w1, cmp_b1, cmp_w2, w_q, w_o, rel_bias, w_ffn_in, w_ffn_out):
    B, S, D = x_prompt.shape
    DB, DS, _ = x_sample.shape
    n_pages = page_table.shape[1]
    past_len = n_pages * cache_slc_kv.shape[1]
    win_buf = state_win_kv.shape[1]
    assert norms.shape[0] == 2 and w_pool.shape[0] == 1 and w_q.shape[0] == 1, "one pooling + one attention layer"
    assert cache_slc_kv.shape[1] == PAGE_ROWS and S % PAGE_ROWS == 0
    assert past_len % CMP_STRIDE == 0 and DS < CMP_STRIDE and D == N_HEADS * HEAD_DIM
    assert S >= win_buf and N_BRANCH * GQA_GROUP <= GATE_PAD

    qdim = N_HEADS * HEAD_DIM
    w_pool_b = w_pool[0].astype(BF16)
    w_in_b = w_ffn_in.astype(BF16)
    w_out_b = w_ffn_out.astype(BF16)
    w_kv_b = w_kv.astype(BF16)
    w_qq = w_q.astype(BF16)
    w_qg = w_q[0, :, qdim:].reshape(D, N_BRANCH, N_KV_HEADS, GQA_GROUP).transpose(0, 2, 1, 3)
    w_qg = jnp.pad(w_qg.reshape(D, N_KV_HEADS, N_BRANCH * GQA_GROUP),
                   ((0, 0), (0, 0), (0, GATE_PAD - N_BRANCH * GQA_GROUP)))
    w_qg = jnp.pad(w_qg.reshape(D, N_KV_HEADS * GATE_PAD), ((0, 0), (0, 128 - N_KV_HEADS * GATE_PAD))).astype(BF16)
    w_o_b = w_o[0].astype(BF16)
    w1_b = cmp_w1.astype(BF16)
    w1cat = w1_b.reshape(2, 2, CMP_STRIDE, HEAD_DIM, HEAD_DIM).transpose(2, 0, 3, 1, 4)
    w1cat = w1cat.reshape(CMP_STRIDE, 2 * HEAD_DIM, 2 * HEAD_DIM)
    w2cat = cmp_w2.astype(BF16).reshape(2 * HEAD_DIM, HEAD_DIM)
    cmp_bias = _cmp_bias(cmp_pos, w1_b, cmp_b1.reshape(2, 1, HEAD_DIM))

    def layer0(x, buf, pos0):
        Bx, Sx, _ = x.shape
        h, st = _pool_layer(x, buf, norms[0, 0], w_pool_b, pool_scale[0], norms[0, 1], pos0)
        return _ffn(h.reshape(Bx * Sx, D), norms[0, 2], norms[0, 3], w_in_b, w_out_b, 0), st

    def layer1_tail(h1, o):
        h2 = _proj_res(o, w_o_b, norms[1, 1], h1)
        return _ffn(h2, norms[1, 2], norms[1, 3], w_in_b, w_out_b, 1)

    rows2d = lambda a: a.reshape(-1, HEAD_DIM)
    kv5 = lambda r, b, s: r.reshape(b, s, 2, N_KV_HEADS, HEAD_DIM)

    h1, st_p = layer0(x_prompt, jnp.zeros((B, POOL_BUF, D), F32), 0)
    cmp_p, slc_p, win_p, sk, svt, wk, wvt = _kv_proj(h1, kv_norm, w_kv_b, True, S)
    qt, gt = _q_proj(h1, norms[1, 0], w_qq, w_qg, True)
    pages_p = S // PAGE_ROWS
    ckv_p = _compress(cmp_p, jnp.arange(B * pages_p, dtype=jnp.int32).reshape(B, pages_p), w1cat, cmp_bias, w2cat)
    n_sub_p = ckv_p.shape[1]
    toe_bk, cmp_bk = _prompt_buckets(S, n_sub_p)
    toe = _bias_table(rel_bias, toe_bk, True, LOG2E).reshape(N_HEADS * 2, PAGE_ROWS, PAGE_ROWS)
    cmpb = _bias_table(rel_bias, cmp_bk, False, LOG2E).reshape(N_HEADS, pages_p, n_sub_p, PAGE_ROWS)
    o_p = _attn_prompt(qt, gt, ckv_p, sk, svt, wk, wvt, toe, cmpb, B, S)
    y_prompt = layer1_tail(h1, o_p.reshape(B * S, qdim)).reshape(B, S, D)
    win_state_p = win_p.reshape(B, S * ROW_SPLIT, HEAD_DIM)[:, (S - win_buf) * ROW_SPLIT:]

    h1s, st_s = layer0(x_sample, state_pool[0], past_len)
    cmp_s, slc_s, win_s = _kv_proj(h1s, kv_norm, w_kv_b, False, DS)
    q_s, gates_s = _q_proj(h1s, norms[1, 0], w_qq, w_qg, False)
    ckv_s = _compress(rows2d(cache_cmp_kv), page_table, w1cat, cmp_bias, w2cat)
    bias_s = _bias_table(rel_bias, _sample_buckets(DS, past_len, n_pages, win_buf, ckv_s.shape[1]))
    o_s, win_state_s = _attn_sample(
        q_s.reshape(DB, DS, qdim), gates_s.reshape(DB, DS, 128), ckv_s, rows2d(cache_slc_kv), page_table,
        slc_s, rows2d(state_win_kv), win_s, bias_s, past_len, win_buf)
    y_sample = layer1_tail(h1s, o_s.reshape(DB * DS, qdim)).reshape(DB, DS, D)

    return (y_prompt, y_sample, st_p[None], st_s[None],
            kv5(cmp_p, B, S), kv5(cmp_s, DB, DS), kv5(slc_p, B, S), kv5(slc_s, DB, DS),
            kv5(win_state_p, B, win_buf), kv5(win_state_s, DB, win_buf))
```

```python
import functools
import math

import numpy as np
import jax
import jax.numpy as jnp
from jax import lax
from jax.experimental import pallas as pl
from jax.experimental.pallas import tpu as pltpu

F32 = jnp.float32
BF16 = jnp.bfloat16

HEAD_DIM = 128
N_KV_HEADS = 4
GQA_GROUP = 4
N_HEADS = N_KV_HEADS * GQA_GROUP
N_BRANCH = 3
POOL_WINDOWS = (2, 4, 8, 16)
POOL_BUF = max(POOL_WINDOWS) - 1
POOL_HALO = POOL_BUF + 1
CMP_BLOCK = 32
CMP_STRIDE = 16
SLC_BLOCK = 64
N_SELECT = 16
WINDOW = 512
N_BUCKETS = 32
MAX_DISTANCE = 128
RMS_EPS = 1e-6
SEL_BIG = 1e3
NEG_INF = -1e30
LOG2E = math.log2(math.e)
PAGE_ROWS = 128
KV_COLS = 2 * N_KV_HEADS * HEAD_DIM
ROW_SPLIT = 2 * N_KV_HEADS
GATE_PAD = 16
SEL_FAR_TILES = 4
PROMPT_HEADS_PER_STEP = 4

VMEM_LIMIT_BYTES = 56 * 1024 * 1024

_NT = (((1,), (1,)), ((), ()))


def _params(*sem):
    return pltpu.CompilerParams(dimension_semantics=sem, vmem_limit_bytes=VMEM_LIMIT_BYTES)


def _rms(x):
    return x * lax.rsqrt(jnp.mean(x * x, axis=-1, keepdims=True) + RMS_EPS)


def _dot(a, b):
    return jnp.dot(a, b, preferred_element_type=F32)


def _dot_nt(a, b):
    return lax.dot_general(a, b, _NT, preferred_element_type=F32)


def _pool_layer_kernel(h_ref, buf_ref, g0_ref, w_ref, scale_ref, g1_ref, o_ref, st_ref, ext_ref, *, ts, pos0):
    i = pl.program_id(1)
    nb, _, D = h_ref.shape
    groups = w_ref.shape[0]
    gd = D // groups

    @pl.when(i == 0)
    def _():
        ext_ref[:, 0:POOL_HALO, :] = buf_ref[...]

    @pl.when(i > 0)
    def _():
        ext_ref[:, 0:POOL_HALO, :] = ext_ref[:, ts:ts + POOL_HALO, :]

    x = h_ref[...]
    ext_ref[:, POOL_HALO:POOL_HALO + ts, :] = _rms(x) * g0_ref[...]
    pos = pos0 + i * ts + lax.broadcasted_iota(jnp.int32, (1, ts, 1), 1)
    parts = []
    for gi, w in enumerate(POOL_WINDOWS):
        cols = slice(gi * gd, (gi + 1) * gd)
        s = ext_ref[:, :, cols]
        cur = s[:, POOL_HALO:POOL_HALO + ts]
        span = 1
        while span < w:
            s = s + pltpu.roll(s, span, 1)
            span *= 2
        s = s[:, POOL_HALO:POOL_HALO + ts]
        inv_cnt = 1.0 / jnp.minimum(w, pos + 1).astype(F32)
        mix = (s * inv_cnt - cur).astype(BF16).reshape(nb * ts, gd)
        parts.append(_dot(mix, w_ref[gi]))
    y = jnp.concatenate(parts, axis=-1) * scale_ref[...]
    o_ref[...] = x + (_rms(y) * g1_ref[...]).reshape(nb, ts, D)
    st_ref[...] = ext_ref[:, ts:ts + POOL_HALO, :]


def _pool_layer(h, buf, g0, w, scale, g1, pos0):
    B, S, D = h.shape
    rows = 256
    ts = min(S, rows)
    nb = max(1, min(B, rows // ts)) if S == ts else 1
    assert S % ts == 0 and B % nb == 0 and (ts >= POOL_HALO or S == ts) and ts % 8 == 0
    assert w.shape[0] == len(POOL_WINDOWS) and all(pw & (pw - 1) == 0 for pw in POOL_WINDOWS)
    buf16 = jnp.pad(buf, ((0, 0), (POOL_HALO - POOL_BUF, 0), (0, 0)))
    vec = lambda: pl.BlockSpec((1, D), lambda b, i: (0, 0))
    out, st = pl.pallas_call(
        functools.partial(_pool_layer_kernel, ts=ts, pos0=pos0),
        grid=(B // nb, S // ts),
        in_specs=[pl.BlockSpec((nb, ts, D), lambda b, i: (b, i, 0)),
                  pl.BlockSpec((nb, POOL_HALO, D), lambda b, i: (b, 0, 0)),
                  vec(), pl.BlockSpec(w.shape, lambda b, i: (0, 0, 0)), vec(), vec()],
        out_specs=[pl.BlockSpec((nb, ts, D), lambda b, i: (b, i, 0)),
                   pl.BlockSpec((nb, POOL_HALO, D), lambda b, i: (b, 0, 0))],
        out_shape=[jax.ShapeDtypeStruct((B, S, D), F32),
                   jax.ShapeDtypeStruct((B, POOL_HALO, D), F32)],
        scratch_shapes=[pltpu.VMEM((nb, POOL_HALO + ts, D), F32)],
        compiler_params=_params("parallel", "arbitrary"),
        name="pool_layer",
    )(h, buf16, g0.reshape(1, D), w, scale.reshape(1, D), g1.reshape(1, D))
    return out, st[:, POOL_HALO - POOL_BUF:]


def _proj_res_kernel(a_ref, w_ref, g_ref, res_ref, o_ref):
    o_ref[...] = res_ref[...] + _rms(_dot(a_ref[...], w_ref[...])) * g_ref[...]


def _proj_res(a, w, g, res):
    M, D = a.shape
    tm = min(M, 512)
    assert M % tm == 0
    row = lambda i: (i, 0)
    fixed = lambda i: (0, 0)
    return pl.pallas_call(
        _proj_res_kernel,
        grid=(M // tm,),
        in_specs=[pl.BlockSpec((tm, D), row), pl.BlockSpec(w.shape, fixed), pl.BlockSpec((1, D), fixed),
                  pl.BlockSpec((tm, D), row)],
        out_specs=pl.BlockSpec((tm, D), row),
        out_shape=jax.ShapeDtypeStruct((M, D), F32),
        compiler_params=_params("parallel"),
        name="o_proj",
    )(a, w, g.reshape(1, D), res)


def _ffn_kernel(h_ref, gi_ref, go_ref, wg_ref, wu_ref, wo_ref, out_ref, xn_ref, acc_ref):
    f = pl.program_id(1)

    @pl.when(f == 0)
    def _():
        xn_ref[...] = (_rms(h_ref[...]) * gi_ref[...]).astype(xn_ref.dtype)
        acc_ref[...] = jnp.zeros_like(acc_ref)

    xn = xn_ref[...]
    gate = _dot(xn, wg_ref[...])
    up = _dot(xn, wu_ref[...])
    act = (gate * jax.nn.sigmoid(gate) * up).astype(BF16)
    acc_ref[...] += _dot(act, wo_ref[...])

    @pl.when(f == pl.num_programs(1) - 1)
    def _():
        out_ref[...] = h_ref[...] + _rms(acc_ref[...]) * go_ref[...]


def _ffn(h, g_in, g_out, w_in, w_out, layer):
    M, D = h.shape
    F = w_out.shape[1]
    tm = min(M, 512)
    tf = 512
    assert M % tm == 0 and F % tf == 0
    nf = F // tf
    return pl.pallas_call(
        _ffn_kernel,
        grid=(M // tm, nf),
        in_specs=[pl.BlockSpec((tm, D), lambda i, f: (i, 0)),
                  pl.BlockSpec((1, D), lambda i, f: (0, 0)),
                  pl.BlockSpec((1, D), lambda i, f: (0, 0)),
                  pl.BlockSpec((None, D, tf), lambda i, f: (layer, 0, f)),
                  pl.BlockSpec((None, D, tf), lambda i, f: (layer, 0, nf + f)),
                  pl.BlockSpec((None, tf, D), lambda i, f: (layer, f, 0))],
        out_specs=pl.BlockSpec((tm, D), lambda i, f: (i, 0)),
        out_shape=jax.ShapeDtypeStruct((M, D), F32),
        scratch_shapes=[pltpu.VMEM((tm, D), BF16), pltpu.VMEM((tm, D), F32)],
        compiler_params=_params("parallel", "arbitrary"),
        name="ffn",
    )(h, g_in.reshape(1, D), g_out.reshape(1, D), w_in, w_in, w_out)


def _kv_proj_kernel(x_ref, g_ref, *refs, emit_aux, seq):
    w_refs, out_refs = refs[:N_BRANCH], refs[N_BRANCH:]
    tm = x_ref.shape[0]
    xn = (_rms(x_ref[...]) * g_ref[...]).astype(BF16)
    for br in range(N_BRANCH):
        y = _dot(xn, w_refs[br][...])
        for j in range(ROW_SPLIT):
            out_refs[br][pl.ds(j, tm, stride=ROW_SPLIT), :] = y[:, j * HEAD_DIM:(j + 1) * HEAD_DIM]
        if emit_aux and br >= 1:
            kb_ref = out_refs[N_BRANCH + 2 * (br - 1)]
            vt_ref = out_refs[N_BRANCH + 2 * (br - 1) + 1]
            if br == 1:
                pos = lax.rem(pl.program_id(0) * tm, seq) + lax.broadcasted_iota(jnp.int32, (tm, 1), 0)
                col = lax.broadcasted_iota(jnp.int32, (1, HEAD_DIM), 1)
                onehot = (col == (pos >> int(math.log2(SLC_BLOCK)))).astype(BF16)
            for h in range(N_KV_HEADS):
                kh = y[:, h * HEAD_DIM:(h + 1) * HEAD_DIM].astype(BF16)
                kb_ref[h] = jnp.concatenate([kh, onehot], axis=1) if br == 1 else kh
                vcol = (N_KV_HEADS + h) * HEAD_DIM
                for r in range(tm // PAGE_ROWS):
                    v = y[r * PAGE_ROWS:(r + 1) * PAGE_ROWS, vcol:vcol + HEAD_DIM]
                    vt_ref[h, r] = v.T.astype(BF16)


def _kv_proj(x, g, weights, emit_aux, seq):
    M, D = x.shape
    tm = min(M, 512)
    assert M % tm == 0 and (not emit_aux or (tm % PAGE_ROWS == 0 and seq % tm == 0
                                             and -(-seq // SLC_BLOCK) <= HEAD_DIM))
    out_specs = [pl.BlockSpec((tm * ROW_SPLIT, HEAD_DIM), lambda i: (i, 0))] * N_BRANCH
    out_shape = [jax.ShapeDtypeStruct((M * ROW_SPLIT, HEAD_DIM), F32)] * N_BRANCH
    if emit_aux:
        tiles = tm // PAGE_ROWS
        for kw in (2 * HEAD_DIM, HEAD_DIM):
            out_specs += [pl.BlockSpec((N_KV_HEADS, tm, kw), lambda i: (0, i, 0)),
                          pl.BlockSpec((N_KV_HEADS, tiles, HEAD_DIM, PAGE_ROWS), lambda i: (0, i, 0, 0))]
            out_shape += [jax.ShapeDtypeStruct((N_KV_HEADS, M, kw), BF16),
                          jax.ShapeDtypeStruct((N_KV_HEADS, M // PAGE_ROWS, HEAD_DIM, PAGE_ROWS), BF16)]
    return pl.pallas_call(
        functools.partial(_kv_proj_kernel, emit_aux=emit_aux, seq=seq),
        grid=(M // tm,),
        in_specs=[pl.BlockSpec((tm, D), lambda i: (i, 0)), pl.BlockSpec((1, D), lambda i: (0, 0))]
        + [pl.BlockSpec((D, KV_COLS), functools.partial(lambda i, br: (0, br), br=br)) for br in range(N_BRANCH)],
        out_specs=out_specs,
        out_shape=out_shape,
        compiler_params=_params("parallel"),
        name="kv_proj",
    )(x, g.reshape(1, D), *([weights] * N_BRANCH))


def _q_proj_kernel(x_ref, g_ref, wq_ref, wg_ref, q_ref, gate_ref, *, transposed):
    tm = x_ref.shape[0]
    xn = (_rms(x_ref[...]) * g_ref[...]).astype(BF16)
    q = _dot(xn, wq_ref[...]) * (HEAD_DIM ** -0.5 * (LOG2E if transposed else 1.0))
    gates = jax.nn.sigmoid(_dot(xn, wg_ref[...]))
    if not transposed:
        q_ref[...] = q.astype(BF16)
        gate_ref[...] = gates
        return
    for r in range(tm // PAGE_ROWS):
        rows = slice(r * PAGE_ROWS, (r + 1) * PAGE_ROWS)
        gt = gates[rows].T
        for hk in range(N_KV_HEADS):
            gate_ref[hk, r] = gt[hk * GATE_PAD:(hk + 1) * GATE_PAD]
            for g in range(GQA_GROUP):
                col = (hk * GQA_GROUP + g) * HEAD_DIM
                q_ref[hk, r, :, g * PAGE_ROWS:(g + 1) * PAGE_ROWS] = q[rows, col:col + HEAD_DIM].T.astype(BF16)


def _q_proj(x, g, wq, wg, transposed):
    M, D = x.shape
    qdim = N_HEADS * HEAD_DIM
    tm = min(M, 512)
    assert M % tm == 0 and (tm % PAGE_ROWS == 0 or not transposed)
    tiles = tm // PAGE_ROWS
    gw = GQA_GROUP * PAGE_ROWS
    if transposed:
        out_specs = [pl.BlockSpec((N_KV_HEADS, tiles, HEAD_DIM, gw), lambda i: (0, i, 0, 0)),
                     pl.BlockSpec((N_KV_HEADS, tiles, GATE_PAD, PAGE_ROWS), lambda i: (0, i, 0, 0))]
        out_shape = [jax.ShapeDtypeStruct((N_KV_HEADS, M // PAGE_ROWS, HEAD_DIM, gw), BF16),
                     jax.ShapeDtypeStruct((N_KV_HEADS, M // PAGE_ROWS, GATE_PAD, PAGE_ROWS), F32)]
    else:
        out_specs = [pl.BlockSpec((tm, qdim), lambda i: (i, 0)),
                     pl.BlockSpec((tm, wg.shape[1]), lambda i: (i, 0))]
        out_shape = [jax.ShapeDtypeStruct((M, qdim), BF16), jax.ShapeDtypeStruct((M, wg.shape[1]), F32)]
    return pl.pallas_call(
        functools.partial(_q_proj_kernel, transposed=transposed),
        grid=(M // tm,),
        in_specs=[pl.BlockSpec((tm, D), lambda i: (i, 0)), pl.BlockSpec((1, D), lambda i: (0, 0)),
                  pl.BlockSpec((None, D, qdim), lambda i: (0, 0, 0)), pl.BlockSpec(wg.shape, lambda i: (0, 0))],
        out_specs=out_specs,
        out_shape=out_shape,
        compiler_params=_params("parallel"),
        name="q_proj",
    )(x, g.reshape(1, D), wq, wg)


def _gelu_tanh(x):
    return 0.5 * x * (1.0 + jnp.tanh(math.sqrt(2.0 / math.pi) * (x + 0.044715 * (x * x * x))))


def _cmp_bias_kernel(pos_ref, w1_ref, b1_ref, o_ref):
    for c in range(2):
        acc = jnp.zeros((8, HEAD_DIM), F32)
        for j in range(CMP_BLOCK):
            pj = jnp.broadcast_to(pos_ref[c, j:j + 1, :], (8, HEAD_DIM)).astype(BF16)
            acc = acc + _dot(pj, w1_ref[c, j])
        o_ref[c] = acc + b1_ref[c]


def _cmp_bias(pos, w1, b1):
    return pl.pallas_call(
        _cmp_bias_kernel,
        out_shape=jax.ShapeDtypeStruct((2, 8, HEAD_DIM), F32),
        compiler_params=pltpu.CompilerParams(vmem_limit_bytes=VMEM_LIMIT_BYTES),
        name="cmp_bias",
    )(pos, w1, b1)


def _compress_kernel(pt_ref, *refs, n_pages):
    del pt_ref
    page_refs = refs[:n_pages]
    w1_ref, bias_ref, w2_ref, o_ref, scr_ref, sum_ref = refs[n_pages:]
    sub_per_page = PAGE_ROWS // CMP_STRIDE
    n_sub = n_pages * sub_per_page
    sub_rows = CMP_STRIDE * ROW_SPLIT
    pages_per_chunk = 4 if n_pages % 4 == 0 else 1
    chunk_rows = pages_per_chunk * sub_per_page * ROW_SPLIT
    is_k = (lax.broadcasted_iota(jnp.int32, (chunk_rows, 1), 0) & (ROW_SPLIT - 1)) < N_KV_HEADS
    bias = jnp.where(is_k[0:ROW_SPLIT], bias_ref[0], bias_ref[1])
    bias = jnp.concatenate([bias] * (chunk_rows // ROW_SPLIT), axis=0)

    def by_side(x):
        return jnp.concatenate([jnp.where(is_k, x, 0.0), jnp.where(is_k, 0.0, x)], axis=1).astype(BF16)

    for ck in range(n_pages // pages_per_chunk):
        chunk_pages = page_refs[ck * pages_per_chunk:(ck + 1) * pages_per_chunk]
        acc = jnp.zeros((chunk_rows, 2 * HEAD_DIM), F32)
        for j in range(CMP_STRIDE):
            x = jnp.concatenate([r[n * sub_rows + j * ROW_SPLIT:n * sub_rows + (j + 1) * ROW_SPLIT, :]
                                 for r in chunk_pages for n in range(sub_per_page)], axis=0)
            acc = acc + _dot(by_side(x), w1_ref[j])
        scr_ref[ck * chunk_rows:(ck + 1) * chunk_rows, :] = acc
    total = n_sub * ROW_SPLIT
    for ck in range(n_pages // pages_per_chunk):
        rows = slice(ck * chunk_rows, (ck + 1) * chunk_rows)
        first = scr_ref[rows, 0:HEAD_DIM]
        nxt_rows = slice(ck * chunk_rows + ROW_SPLIT, min((ck + 1) * chunk_rows + ROW_SPLIT, total))
        nxt = scr_ref[nxt_rows, HEAD_DIM:2 * HEAD_DIM]
        if nxt.shape[0] < chunk_rows:
            nxt = jnp.concatenate([nxt, jnp.zeros((chunk_rows - nxt.shape[0], HEAD_DIM), F32)], axis=0)
        hid = _gelu_tanh(first + nxt + bias)
        out = _dot(by_side(hid), w2_ref[...])
        if (ck + 1) * chunk_rows == total:
            row = lax.broadcasted_iota(jnp.int32, (chunk_rows, 1), 0)
            out = jnp.where(row < chunk_rows - ROW_SPLIT, out, 0.0)
        sum_ref[rows, :] = out
    for ch in range(ROW_SPLIT):
        o_ref[0, :, ch * HEAD_DIM:(ch + 1) * HEAD_DIM] = sum_ref[pl.ds(ch, n_sub, stride=ROW_SPLIT), :]


def _compress(pages, table, w1cat, bias, w2cat):
    nb, n_pages = table.shape
    n_sub = n_pages * (PAGE_ROWS // CMP_STRIDE)
    page_rows2d = PAGE_ROWS * ROW_SPLIT
    page_specs = [pl.BlockSpec((page_rows2d, HEAD_DIM), functools.partial(lambda b, pt, p: (pt[b, p], 0), p=p))
                  for p in range(n_pages)]
    const = lambda shape: pl.BlockSpec(shape, lambda b, pt: (0,) * len(shape))
    grid_spec = pltpu.PrefetchScalarGridSpec(
        num_scalar_prefetch=1,
        grid=(nb,),
        in_specs=page_specs + [const(w1cat.shape), const(bias.shape), const(w2cat.shape)],
        out_specs=pl.BlockSpec((1, n_sub, KV_COLS), lambda b, pt: (b, 0, 0)),
        scratch_shapes=[pltpu.VMEM((n_sub * ROW_SPLIT, 2 * HEAD_DIM), F32),
                        pltpu.VMEM((n_sub * ROW_SPLIT, HEAD_DIM), F32)],
    )
    return pl.pallas_call(
        functools.partial(_compress_kernel, n_pages=n_pages),
        grid_spec=grid_spec,
        out_shape=jax.ShapeDtypeStruct((nb, n_sub, KV_COLS), F32),
        compiler_params=_params("parallel"),
        name="compress",
    )(table, *([pages] * n_pages), w1cat, bias, w2cat)


def _rel_bucket_np(dist):
    n = np.maximum(dist, 0)
    max_exact = N_BUCKETS // 2
    nf = np.maximum(n, 1).astype(np.float32)
    large = max_exact + (np.log(nf / np.float32(max_exact)) / np.float32(math.log(MAX_DISTANCE / max_exact))
                         * np.float32(N_BUCKETS - max_exact)).astype(np.int32)
    large = np.minimum(large, N_BUCKETS - 1)
    return np.where(n < max_exact, n, large).astype(np.int32)


def _bias_table_kernel(rb_ref, bk_ref, o_ref, *, minus_far, scale):
    bk = bk_ref[...]
    for h in range(N_HEADS):
        v = jnp.full(bk.shape, rb_ref[0, h], F32)
        for b in range(1, N_BUCKETS):
            v = jnp.where(bk == b, rb_ref[b, h], v)
        if minus_far:
            v = v - rb_ref[N_BUCKETS - 1, h]
        o_ref[h] = v if scale == 1.0 else v * scale


def _bias_table(rel_bias, bucket, minus_far=False, scale=1.0):
    R, C = bucket.shape
    tr = min(R, 128)
    assert R % tr == 0
    return pl.pallas_call(
        functools.partial(_bias_table_kernel, minus_far=minus_far, scale=scale),
        grid=(R // tr,),
        in_specs=[pl.BlockSpec(memory_space=pltpu.SMEM), pl.BlockSpec((tr, C), lambda r: (r, 0))],
        out_specs=pl.BlockSpec((N_HEADS, tr, C), lambda r: (0, r, 0)),
        out_shape=jax.ShapeDtypeStruct((N_HEADS, R, C), F32),
        compiler_params=_params("parallel"),
        name="bias_table",
    )(rel_bias, jnp.asarray(bucket))


def _overlap_np(n_sub, n_slc):
    n_cmp = n_sub - 1
    cs = np.arange(n_cmp) * CMP_STRIDE
    ce = cs + CMP_BLOCK - 1
    ss = np.arange(n_slc) * SLC_BLOCK
    se = ss + SLC_BLOCK - 1
    ov = np.minimum(ce[:, None], se[None, :]) - np.maximum(cs[:, None], ss[None, :]) + 1
    out = np.zeros((n_sub, n_slc), np.float32)
    out[:n_cmp] = np.maximum(ov, 0).astype(np.float32) / CMP_BLOCK
    return out


def _pad_to(a, shape):
    return np.pad(a, [(0, s - d) for d, s in zip(a.shape, shape)])


def _score(imp, j, qb, n_slc):
    prio = 2.0 * (j == qb).astype(F32) + (j == qb - 1).astype(F32) + (j == 0).astype(F32)
    score = jnp.where(j <= qb, imp + SEL_BIG * prio, -SEL_BIG)
    return jnp.where(j < n_slc, score, -3e38)


def _attn_prompt_kernel(qt_ref, gt_ref, ck_ref, cv_ref, sk_ref, svt_ref, wk_ref, wvt_ref,
                        toe_ref, cmpb_ref, ovt_ref, o_ref,
                        m_ref, l_ref, acc_ref, accw_ref, *, n_slc, n_sub, hps):
    tq = PAGE_ROWS
    i = pl.program_id(2)
    heads = range(hps)
    t_idx = lax.broadcasted_iota(jnp.int32, (1, tq), 1)
    qpos = i * tq + t_idx
    lane4 = lambda x: jnp.concatenate([x] * GQA_GROUP, axis=1)
    n_idx = lax.broadcasted_iota(jnp.int32, (n_sub, 1), 0)
    valid = lane4((n_idx * CMP_STRIDE + (CMP_BLOCK - 1) <= qpos) & (n_idx < n_sub - 1))
    nsp = ovt_ref.shape[0]
    j = lax.broadcasted_iota(jnp.int32, (nsp, 1), 0)
    qb = qpos >> int(math.log2(SLC_BLOCK))

    qts, qt_sels, o_cs = [], [], []
    for hh in heads:
        qt = qt_ref[hh, 0]
        cols = slice(hh * HEAD_DIM, (hh + 1) * HEAD_DIM)
        kc = ck_ref[0, :, cols].astype(BF16)
        vct = cv_ref[0, :, cols].T.astype(BF16)
        bias_c = jnp.concatenate([cmpb_ref[hh * GQA_GROUP + g, 0] for g in range(GQA_GROUP)], axis=1)
        logits = jnp.where(valid, _dot(kc, qt) + bias_c, NEG_INF)
        p = jnp.exp2(logits - jnp.max(logits, axis=0, keepdims=True))
        p = p * (1.0 / jnp.sum(p, axis=0, keepdims=True))
        p = jnp.where(valid, p, 0.0)
        o_cs.append(_dot(vct, p.astype(BF16)))
        psum = p[:, 0:tq]
        for g in range(1, GQA_GROUP):
            psum = psum + p[:, g * tq:(g + 1) * tq]
        imp = jnp.dot(ovt_ref[...], psum, precision=lax.Precision.HIGHEST, preferred_element_type=F32)
        score = _score(imp, j, qb, n_slc)
        rank = jnp.zeros((nsp, tq), jnp.int32)
        for jj in range(n_slc):
            row = score[jj:jj + 1, :]
            rank = rank + ((row > score) | ((row == score) & (jj < j))).astype(jnp.int32)
        madd = jnp.where(rank < min(N_SELECT, n_slc), 0.0, NEG_INF)
        qts.append(qt)
        qt_sels.append(jnp.concatenate([qt, lane4(madd).astype(BF16),
                                        jnp.zeros((HEAD_DIM - nsp, GQA_GROUP * tq), BF16)], axis=0))

    k_row = lax.broadcasted_iota(jnp.int32, (tq, 1), 0)

    def chunk(hh, k, kinds, k_ref, vt_ref, q_op, out_ref):
        nt = len(kinds)
        k0 = pl.multiple_of(k * tq, tq)
        s = _dot(k_ref[hh, pl.ds(k0, nt * tq), :], q_op)
        parts = []
        for jx, kind in enumerate(kinds):
            sj = s[jx * tq:(jx + 1) * tq]
            if kind in ("near", "diag"):
                d = 0 if kind == "diag" else 1
                sj = sj + jnp.concatenate([toe_ref[(hh * GQA_GROUP + g) * 2 + d] for g in range(GQA_GROUP)],
                                          axis=1)
            if kind == "diag":
                sj = jnp.where(lane4(k_row <= t_idx), sj, NEG_INF)
            elif kind == "edge":
                sj = jnp.where(lane4(k_row > t_idx), sj, NEG_INF)
            parts.append(sj)
        m_new = functools.reduce(jnp.maximum, parts)
        m_new = jnp.max(m_new, axis=0, keepdims=True)
        if out_ref is None:
            m_old = m_ref[hh]
            m_new = jnp.maximum(m_old, m_new)
            alpha = jnp.exp2(m_old - m_new)
        ps = [jnp.exp2(sj - m_new) for sj in parts]
        lsum = jnp.sum(functools.reduce(jnp.add, ps), axis=0, keepdims=True)
        vt = jnp.concatenate([vt_ref[hh, k + jx] for jx in range(nt)], axis=1)
        pv = _dot(vt, jnp.concatenate(ps, axis=0).astype(BF16))
        if out_ref is None:
            l_ref[hh] = alpha * l_ref[hh] + lsum
            acc_ref[hh] = alpha * acc_ref[hh] + pv
            m_ref[hh] = m_new
        else:
            out_ref[hh] = pv * (1.0 / lsum)

    def tail_kinds(n_far):
        return ("far",) * n_far + ("near", "diag")

    far_step = SEL_FAR_TILES
    wt = WINDOW // tq
    m_ref[...] = jnp.full(m_ref.shape, NEG_INF, F32)
    l_ref[...] = jnp.zeros(l_ref.shape, F32)
    acc_ref[...] = jnp.zeros(acc_ref.shape, F32)

    n_far = jnp.maximum(i - 1, 0) // far_step

    def far_body(c, carry):
        for hh in heads:
            chunk(hh, c * far_step, ("far",) * far_step, sk_ref, svt_ref, qt_sels[hh], None)
        return carry

    lax.fori_loop(0, n_far, far_body, 0)

    for c in range(wt):
        @pl.when(i == c)
        def _(c=c):
            kinds = tail_kinds(c - 1) if c >= 1 else ("diag",)
            for hh in heads:
                chunk(hh, 0, kinds, sk_ref, svt_ref, qt_sels[hh], None)
                chunk(hh, 0, kinds, wk_ref, wvt_ref, qts[hh], accw_ref)

    for r in range(far_step):
        @pl.when((i >= wt) & (lax.rem(i + far_step - 1, far_step) == r))
        def _(r=r):
            for hh in heads:
                chunk(hh, i - 1 - r, tail_kinds(r), sk_ref, svt_ref, qt_sels[hh], None)
                chunk(hh, i - wt, ("edge",) + tail_kinds(wt - 2), wk_ref, wvt_ref, qts[hh], accw_ref)

    for hh in heads:
        o_s = acc_ref[hh] * (1.0 / l_ref[hh])
        o_w = accw_ref[hh]
        o_c = o_cs[hh]
        gt = gt_ref[hh, 0]
        for g in range(GQA_GROUP):
            lanes = slice(g * tq, (g + 1) * tq)
            o = (gt[g:g + 1] * o_c[:, lanes] + gt[GQA_GROUP + g:GQA_GROUP + g + 1] * o_s[:, lanes]
                 + gt[2 * GQA_GROUP + g:2 * GQA_GROUP + g + 1] * o_w[:, lanes])
            col = (hh * GQA_GROUP + g) * HEAD_DIM
            o_ref[0, :, col:col + HEAD_DIM] = o.T.astype(o_ref.dtype)


def _attn_prompt(qt, gt, ckv, sk, svt, wk, wvt, toe, cmpb, B, S):
    tq = PAGE_ROWS
    nq = S // tq
    n_sub = ckv.shape[1]
    n_slc = -(-S // SLC_BLOCK)
    nsp = -(-n_slc // 16) * 16
    assert S % tq == 0 and nsp <= HEAD_DIM and nq > WINDOW // tq
    ovt = jnp.asarray(_pad_to(_overlap_np(n_sub, n_slc).T, (nsp, n_sub)))
    gw = GQA_GROUP * tq
    hps = PROMPT_HEADS_PER_STEP
    assert N_KV_HEADS % hps == 0
    return pl.pallas_call(
        functools.partial(_attn_prompt_kernel, n_slc=n_slc, n_sub=n_sub, hps=hps),
        grid=(B, N_KV_HEADS // hps, nq),
        in_specs=[pl.BlockSpec((hps, 1, HEAD_DIM, gw), lambda b, h, i: (h, b * nq + i, 0, 0)),
                  pl.BlockSpec((hps, 1, GATE_PAD, tq), lambda b, h, i: (h, b * nq + i, 0, 0)),
                  pl.BlockSpec((1, n_sub, hps * HEAD_DIM), lambda b, h, i: (b, 0, h)),
                  pl.BlockSpec((1, n_sub, hps * HEAD_DIM), lambda b, h, i: (b, 0, N_KV_HEADS // hps + h)),
                  pl.BlockSpec((hps, S, 2 * HEAD_DIM), lambda b, h, i: (h, b, 0)),
                  pl.BlockSpec((hps, nq, HEAD_DIM, tq), lambda b, h, i: (h, b, 0, 0)),
                  pl.BlockSpec((hps, S, HEAD_DIM), lambda b, h, i: (h, b, 0)),
                  pl.BlockSpec((hps, nq, HEAD_DIM, tq), lambda b, h, i: (h, b, 0, 0)),
                  pl.BlockSpec((hps * GQA_GROUP * 2, tq, tq), lambda b, h, i: (h, 0, 0)),
                  pl.BlockSpec((hps * GQA_GROUP, 1, n_sub, tq), lambda b, h, i: (h, i, 0, 0)),
                  pl.BlockSpec(ovt.shape, lambda b, h, i: (0, 0))],
        out_specs=pl.BlockSpec((1, tq, hps * GQA_GROUP * HEAD_DIM), lambda b, h, i: (b, i, h)),
        out_shape=jax.ShapeDtypeStruct((B, S, N_HEADS * HEAD_DIM), BF16),
        scratch_shapes=[pltpu.VMEM((hps, 1, gw), F32), pltpu.VMEM((hps, 1, gw), F32),
                        pltpu.VMEM((hps, HEAD_DIM, gw), F32), pltpu.VMEM((hps, HEAD_DIM, gw), F32)],
        compiler_params=_params("parallel", "parallel", "arbitrary"),
        name="attn_prompt",
    )(qt, gt, ckv, ckv, sk, svt, wk, wvt, toe, cmpb, ovt)


def _attn_sample_kernel(pt_ref, *refs, n_pages, n_slc, n_sub, past_len, win_buf, ds):
    del pt_ref
    page_refs = refs[:n_pages]
    (q_ref, gate_ref, ckv_ref, snew_ref, wst_ref, wnew_ref, bias_ref, ov_ref, e_ref, wst_hbm,
     o_ref, wout_hbm, sem_o) = refs[n_pages:]
    nk_s = (n_pages + 1) * PAGE_ROWS
    nk_w = win_buf + PAGE_ROWS
    head_col = lambda r, col, n: r[pl.ds(col, n, stride=ROW_SPLIT), :]

    b = pl.program_id(0)
    state_rows = win_buf * ROW_SPLIT
    shift = ds * ROW_SPLIT
    keep = pltpu.make_async_copy(wst_hbm.at[pl.ds(b * state_rows + shift, state_rows - shift)],
                                 wout_hbm.at[pl.ds(b * state_rows, state_rows - shift)], sem_o.at[0])
    append = pltpu.make_async_copy(wnew_ref, wout_hbm.at[pl.ds((b + 1) * state_rows - shift, shift)], sem_o.at[1])
    keep.start()
    append.start()

    t_idx = lax.broadcasted_iota(jnp.int32, (ds, 1), 0)
    qpos = past_len + t_idx
    pad = jnp.zeros((PAGE_ROWS - ds, HEAD_DIM), F32)
    gates = gate_ref[0]
    kv_heads = range(N_KV_HEADS)
    head_rows = GQA_GROUP * ds
    per_head = lambda x, reps: jnp.concatenate([x] * reps, axis=0)
    new_rows = lambda r, col: jnp.concatenate([r[pl.ds(col, ds, stride=ROW_SPLIT), :], pad], axis=0).astype(BF16)

    ks_pos = lax.broadcasted_iota(jnp.int32, (1, nk_s), 1)
    dist_w = win_buf + t_idx - lax.broadcasted_iota(jnp.int32, (1, nk_w), 1)
    ok_w = per_head((dist_w >= 0) & (dist_w < WINDOW), N_HEADS)
    n_idx = lax.broadcasted_iota(jnp.int32, (1, n_sub), 1)
    ok_c = per_head((n_idx * CMP_STRIDE + (CMP_BLOCK - 1) <= qpos) & (n_idx < n_sub - 1), N_HEADS)
    lanes = ov_ref.shape[1]
    j = lax.broadcasted_iota(jnp.int32, (1, lanes), 1)
    qb = per_head(qpos >> int(math.log2(SLC_BLOCK)), N_KV_HEADS)

    def softmax(logits, ok):
        logits = jnp.where(ok, logits, NEG_INF)
        p = jnp.exp(logits - jnp.max(logits, axis=-1, keepdims=True))
        return p / jnp.sum(p, axis=-1, keepdims=True)

    bias = bias_ref[...].reshape(N_HEADS * ds, bias_ref.shape[2])
    qs = [jnp.concatenate([q_ref[0, :, (hk * GQA_GROUP + g) * HEAD_DIM:(hk * GQA_GROUP + g + 1) * HEAD_DIM]
                           for g in range(GQA_GROUP)], axis=0) for hk in kv_heads]
    kcols = [slice(hk * HEAD_DIM, (hk + 1) * HEAD_DIM) for hk in kv_heads]
    vcols = [slice((N_KV_HEADS + hk) * HEAD_DIM, (N_KV_HEADS + hk + 1) * HEAD_DIM) for hk in kv_heads]

    def attend(k_past, v_past, k_new, v_new, bias_cols, ok):
        n = k_past(0).shape[0]
        logits = jnp.concatenate(
            [jnp.concatenate([_dot_nt(qs[hk], k_past(hk).astype(BF16)), _dot_nt(qs[hk], k_new(hk))], axis=1)
             for hk in kv_heads], axis=0) + bias_cols
        p = softmax(logits, ok).astype(BF16)
        return [_dot(p[hk * head_rows:(hk + 1) * head_rows, 0:n], v_past(hk).astype(BF16))
                + _dot(p[hk * head_rows:(hk + 1) * head_rows, n:], v_new(hk)) for hk in kv_heads]

    logits_c = jnp.concatenate([_dot_nt(qs[hk], ckv_ref[0, :, kcols[hk]].astype(BF16)) for hk in kv_heads], axis=0)
    p_c = jnp.where(ok_c, softmax(logits_c + bias[:, nk_s + nk_w:nk_s + nk_w + n_sub], ok_c), 0.0)
    o_c = [_dot(p_c[hk * head_rows:(hk + 1) * head_rows].astype(BF16), ckv_ref[0, :, vcols[hk]].astype(BF16))
           for hk in kv_heads]
    psum = jnp.concatenate(
        [functools.reduce(jnp.add, [p_c[(hk * GQA_GROUP + g) * ds:(hk * GQA_GROUP + g + 1) * ds]
                                    for g in range(GQA_GROUP)]) for hk in kv_heads], axis=0)
    imp = jnp.dot(psum, ov_ref[...], precision=lax.Precision.HIGHEST, preferred_element_type=F32)
    score = _score(imp, j, qb, n_slc)
    rank = jnp.zeros(score.shape, jnp.int32)
    for jj in range(n_slc):
        col = score[:, jj:jj + 1]
        rank = rank + ((col > score) | ((col == score) & (jj < j))).astype(jnp.int32)
    sel = ((rank < min(N_SELECT, n_slc)) & (j < n_slc)).astype(BF16)
    allowed = (_dot(sel, e_ref[...]) > 0.5) & (ks_pos <= per_head(qpos, N_KV_HEADS))
    ok_s = jnp.concatenate([per_head(allowed[hk * ds:(hk + 1) * ds], GQA_GROUP) for hk in kv_heads], axis=0)

    past_s = lambda col: jnp.concatenate([head_col(r, col, PAGE_ROWS) for r in page_refs], axis=0)
    o_s = attend(lambda hk: past_s(hk), lambda hk: past_s(N_KV_HEADS + hk),
                 lambda hk: new_rows(snew_ref, hk), lambda hk: new_rows(snew_ref, N_KV_HEADS + hk),
                 bias[:, 0:nk_s], ok_s)
    o_w = attend(lambda hk: head_col(wst_ref, hk, win_buf), lambda hk: head_col(wst_ref, N_KV_HEADS + hk, win_buf),
                 lambda hk: new_rows(wnew_ref, hk), lambda hk: new_rows(wnew_ref, N_KV_HEADS + hk),
                 bias[:, nk_s:nk_s + nk_w], ok_w)

    for hk in kv_heads:
        gcol = hk * GATE_PAD
        for g in range(GQA_GROUP):
            rows = slice(g * ds, (g + 1) * ds)
            gate = lambda br, g=g: gates[:, gcol + br * GQA_GROUP + g:gcol + br * GQA_GROUP + g + 1]
            o = gate(0) * o_c[hk][rows] + gate(1) * o_s[hk][rows] + gate(2) * o_w[hk][rows]
            col = (hk * GQA_GROUP + g) * HEAD_DIM
            o_ref[0, :, col:col + HEAD_DIM] = o.astype(o_ref.dtype)

    keep.wait()
    append.wait()


def _attn_sample(q, gates, ckv, cache_slc, table, slc_new, win_state, win_new, bias, past_len, win_buf):
    DB, ds, _ = q.shape
    n_pages = table.shape[1]
    n_sub = ckv.shape[1]
    n_slc = -(-(past_len + ds) // SLC_BLOCK)
    lanes = PAGE_ROWS
    assert ds % 8 == 0 and ds <= PAGE_ROWS and n_slc <= lanes and n_sub <= lanes
    assert win_buf <= past_len and win_buf > ds
    nk_s = (n_pages + 1) * PAGE_ROWS
    ov = jnp.asarray(_pad_to(_overlap_np(n_sub, n_slc), (n_sub, lanes)))
    e = (np.arange(nk_s)[None, :] // SLC_BLOCK == np.arange(lanes)[:, None]) & (np.arange(lanes)[:, None] < n_slc)
    e = jnp.asarray(e.astype(np.float32), dtype=BF16)
    page_specs = [pl.BlockSpec((PAGE_ROWS * ROW_SPLIT, HEAD_DIM),
                               functools.partial(lambda b, pt, p: (pt[b, p], 0), p=p)) for p in range(n_pages)]
    per_b3 = lambda shape: pl.BlockSpec((1,) + shape[1:], lambda b, pt: (b, 0, 0))
    rows_b = lambda n: pl.BlockSpec((n * ROW_SPLIT, HEAD_DIM), lambda b, pt: (b, 0))
    const = lambda shape: pl.BlockSpec(shape, lambda b, pt: (0,) * len(shape))
    grid_spec = pltpu.PrefetchScalarGridSpec(
        num_scalar_prefetch=1,
        grid=(DB,),
        in_specs=page_specs + [per_b3(q.shape), per_b3(gates.shape), per_b3(ckv.shape), rows_b(ds),
                               rows_b(win_buf), rows_b(ds), const(bias.shape), const(ov.shape), const(e.shape),
                               pl.BlockSpec(memory_space=pl.ANY)],
        out_specs=[per_b3(q.shape), pl.BlockSpec(memory_space=pl.ANY)],
        scratch_shapes=[pltpu.SemaphoreType.DMA((2,))],
    )
    return pl.pallas_call(
        functools.partial(_attn_sample_kernel, n_pages=n_pages, n_slc=n_slc, n_sub=n_sub,
                          past_len=past_len, win_buf=win_buf, ds=ds),
        grid_spec=grid_spec,
        out_shape=[jax.ShapeDtypeStruct(q.shape, BF16), jax.ShapeDtypeStruct(win_state.shape, F32)],
        compiler_params=_params("parallel"),
        name="attn_sample",
    )(table, *([cache_slc] * n_pages), q, gates, ckv, slc_new, win_state, win_new, bias, ov, e, win_state)


def _prompt_buckets(S, n_sub):
    tq = PAGE_ROWS
    k = np.arange(tq)[:, None]
    t = np.arange(tq)[None, :]
    toe = np.concatenate([_rel_bucket_np(t - k + d * tq) for d in range(2)], axis=0)
    kpos_c = np.arange(n_sub)[:, None] * CMP_STRIDE + CMP_BLOCK - 1
    cmp = np.concatenate([_rel_bucket_np(i * tq + t - kpos_c) for i in range(S // tq)], axis=0)
    return toe, cmp


def _sample_buckets(ds, past_len, n_pages, win_buf, n_sub):
    qpos = past_len + np.arange(ds)[:, None]
    ks = np.arange((n_pages + 1) * PAGE_ROWS)[None, :]
    kw = past_len - win_buf + np.arange(win_buf + PAGE_ROWS)[None, :]
    kc = np.arange(n_sub)[None, :] * CMP_STRIDE + CMP_BLOCK - 1
    return np.concatenate([_rel_bucket_np(qpos - ks), _rel_bucket_np(qpos - kw), _rel_bucket_np(qpos - kc)], axis=1)


def kernel(x_prompt, x_sample, state_pool, cache_cmp_kv, cache_slc_kv, state_win_kv, page_table, norms, w_pool,
           pool_scale, kv_norm, w_kv, cmp_pos, cmp_w1, cmp_b1, cmp_w2, w_q, w_o, rel_bias, w_ffn_in, w_ffn_out):
    B, S, D = x_prompt.shape
    DB, DS, _ = x_sample.shape
    n_pages = page_table.shape[1]
    past_len = n_pages * cache_slc_kv.shape[1]
    win_buf = state_win_kv.shape[1]
    assert norms.shape[0] == 2 and w_pool.shape[0] == 1 and w_q.shape[0] == 1, "one pooling + one attention layer"
    assert cache_slc_kv.shape[1] == PAGE_ROWS and S % PAGE_ROWS == 0
    assert past_len % CMP_STRIDE == 0 and DS < CMP_STRIDE and D == N_HEADS * HEAD_DIM
    assert S >= win_buf and N_BRANCH * GQA_GROUP <= GATE_PAD

    qdim = N_HEADS * HEAD_DIM
    w_pool_b = w_pool[0].astype(BF16)
    w_in_b = w_ffn_in.astype(BF16)
    w_out_b = w_ffn_out.astype(BF16)
    w_kv_b = w_kv.astype(BF16)
    w_qq = w_q.astype(BF16)
    w_qg = w_q[0, :, qdim:].reshape(D, N_BRANCH, N_KV_HEADS, GQA_GROUP).transpose(0, 2, 1, 3)
    w_qg = jnp.pad(w_qg.reshape(D, N_KV_HEADS, N_BRANCH * GQA_GROUP),
                   ((0, 0), (0, 0), (0, GATE_PAD - N_BRANCH * GQA_GROUP)))
    w_qg = jnp.pad(w_qg.reshape(D, N_KV_HEADS * GATE_PAD), ((0, 0), (0, 128 - N_KV_HEADS * GATE_PAD))).astype(BF16)
    w_o_b = w_o[0].astype(BF16)
    w1_b = cmp_w1.astype(BF16)
    w1cat = w1_b.reshape(2, 2, CMP_STRIDE, HEAD_DIM, HEAD_DIM).transpose(2, 0, 3, 1, 4)
    w1cat = w1cat.reshape(CMP_STRIDE, 2 * HEAD_DIM, 2 * HEAD_DIM)
    w2cat = cmp_w2.astype(BF16).reshape(2 * HEAD_DIM, HEAD_DIM)
    cmp_bias = _cmp_bias(cmp_pos, w1_b, cmp_b1.reshape(2, 1, HEAD_DIM))

    def layer0(x, buf, pos0):
        Bx, Sx, _ = x.shape
        h, st = _pool_layer(x, buf, norms[0, 0], w_pool_b, pool_scale[0], norms[0, 1], pos0)
        return _ffn(h.reshape(Bx * Sx, D), norms[0, 2], norms[0, 3], w_in_b, w_out_b, 0), st

    def layer1_tail(h1, o):
        h2 = _proj_res(o, w_o_b, norms[1, 1], h1)
        return _ffn(h2, norms[1, 2], norms[1, 3], w_in_b, w_out_b, 1)

    rows2d = lambda a: a.reshape(-1, HEAD_DIM)
    kv5 = lambda r, b, s: r.reshape(b, s, 2, N_KV_HEADS, HEAD_DIM)

    h1, st_p = layer0(x_prompt, jnp.zeros((B, POOL_BUF, D), F32), 0)
    cmp_p, slc_p, win_p, sk, svt, wk, wvt = _kv_proj(h1, kv_norm, w_kv_b, True, S)
    qt, gt = _q_proj(h1, norms[1, 0], w_qq, w_qg, True)
    pages_p = S // PAGE_ROWS
    ckv_p = _compress(cmp_p, jnp.arange(B * pages_p, dtype=jnp.int32).reshape(B, pages_p), w1cat, cmp_bias, w2cat)
    n_sub_p = ckv_p.shape[1]
    toe_bk, cmp_bk = _prompt_buckets(S, n_sub_p)
    toe = _bias_table(rel_bias, toe_bk, True, LOG2E).reshape(N_HEADS * 2, PAGE_ROWS, PAGE_ROWS)
    cmpb = _bias_table(rel_bias, cmp_bk, False, LOG2E).reshape(N_HEADS, pages_p, n_sub_p, PAGE_ROWS)
    o_p = _attn_prompt(qt, gt, ckv_p, sk, svt, wk, wvt, toe, cmpb, B, S)
    y_prompt = layer1_tail(h1, o_p.reshape(B * S, qdim)).reshape(B, S, D)
    win_state_p = win_p.reshape(B, S * ROW_SPLIT, HEAD_DIM)[:, (S - win_buf) * ROW_SPLIT:]

    h1s, st_s = layer0(x_sample, state_pool[0], past_len)
    cmp_s, slc_s, win_s = _kv_proj(h1s, kv_norm, w_kv_b, False, DS)
    q_s, gates_s = _q_proj(h1s, norms[1, 0], w_qq, w_qg, False)
    ckv_s = _compress(rows2d(cache_cmp_kv), page_table, w1cat, cmp_bias, w2cat)
    bias_s = _bias_table(rel_bias, _sample_buckets(DS, past_len, n_pages, win_buf, ckv_s.shape[1]))
    o_s, win_state_s = _attn_sample(
        q_s.reshape(DB, DS, qdim), gates_s.reshape(DB, DS, 128), ckv_s, rows2d(cache_slc_kv), page_table,
        slc_s, rows2d(state_win_kv), win_s, bias_s, past_len, win_buf)
    y_sample = layer1_tail(h1s, o_s.reshape(DB * DS, qdim)).reshape(DB, DS, D)

    return (y_prompt, y_sample, st_p[None], st_s[None],
            kv5(cmp_p, B, S), kv5(cmp_s, DB, DS), kv5(slc_p, B, S), kv5(slc_s, DB, DS),
            kv5(win_state_p, B, win_buf), kv5(win_state_s, DB, win_buf))
```
